```python
import functools
import jax, jax.numpy as jnp
from jax import lax
import numpy as np

D_MODEL = 1024
BATCH = 4
SEQ = 4096
DEPTH = 1
DEC_BATCH = 128
DEC_SEQ = 1
PAST_LEN = 2048
PAGE_SIZE = 128

HEAD_DIM = 64
N_HEADS_A = 8
N_HEADS_B = 8
N_IDX_HEADS = 8
IDX_DIM = 64
TOPK_MAX = 256
D_FF = 4 * D_MODEL
Q_BLOCK = 128
FORGET_BIAS = 2.0
EPS = 1e-6
MASK_VALUE = -1e30
WIDTH_A = N_HEADS_A * HEAD_DIM
WIDTH_B = N_HEADS_B * HEAD_DIM
IN_SIZES = (WIDTH_A, WIDTH_A, WIDTH_A, WIDTH_B, WIDTH_B, WIDTH_B, N_IDX_HEADS * IDX_DIM, IDX_DIM, N_IDX_HEADS, N_HEADS_B, D_MODEL, D_MODEL)
D_IN = 3 * WIDTH_A + 3 * WIDTH_B + N_IDX_HEADS * IDX_DIM + IDX_DIM + N_IDX_HEADS + N_HEADS_B + 2 * D_MODEL

kernel_name = 'hybrid_dsa_fox_decode_step'


def rms_norm(x, g):
    xf = x.astype(jnp.float32)
    y = xf * lax.rsqrt(jnp.mean(xf * xf, axis=-1, keepdims=True) + EPS)
    return (y * g.astype(jnp.float32)).astype(x.dtype)


def topk_count(n_keys):
    return max(1, min(TOPK_MAX, n_keys // 4))


def alibi_slopes():
    return jnp.exp2(-8.0 * jnp.arange(1, N_HEADS_A + 1, dtype=jnp.float32) / N_HEADS_A)


def take_rows(a, idx):
    return jax.vmap(lambda ab, ib: ab[ib])(a, idx)


def project(h, w_in, b_forget, g_qa, g_ka, g_qb, g_kb):
    B, T, _ = h.shape
    z = h @ w_in
    cuts = [int(i) for i in np.cumsum(IN_SIZES)[:-1]]
    qa, ka, va, qb, kb, vb, qi, ki, wi, fl, ga, gb = jnp.split(z, cuts, axis=-1)
    heads = lambda a, n: a.reshape(B, T, n, HEAD_DIM)
    qa = rms_norm(heads(qa, N_HEADS_A), g_qa)
    ka = rms_norm(heads(ka, N_HEADS_A), g_ka)
    va = heads(va, N_HEADS_A)
    qb = rms_norm(heads(qb, N_HEADS_B), g_qb)
    kb = rms_norm(heads(kb, N_HEADS_B), g_kb)
    vb = heads(vb, N_HEADS_B)
    qi = qi.reshape(B, T, N_IDX_HEADS, IDX_DIM)
    wi = wi * (N_IDX_HEADS ** -0.5)
    logf = jax.nn.log_sigmoid((fl + b_forget).astype(jnp.float32)).astype(h.dtype)
    return qa, ka, va, qb, kb, vb, qi, ki, wi, logf, ga, gb


def index_scores(qi, wi, ki, qpos, kpos):
    s = jnp.einsum('bthd,bsd->bths', qi, ki).astype(jnp.float32) * (IDX_DIM ** -0.5)
    s = jnp.einsum('bths,bth->bts', jax.nn.relu(s), wi.astype(jnp.float32))
    return jnp.where(kpos[None, None, :] <= qpos[None, :, None], s, MASK_VALUE)


def sparse_attend(q, k_sel, v_sel, qpos, sel):
    logits = jnp.einsum('bthd,btkhd->bhtk', q, k_sel).astype(jnp.float32) * (HEAD_DIM ** -0.5)
    dist = (qpos[None, :, None] - sel).astype(jnp.float32)
    logits = logits - alibi_slopes()[None, :, None, None] * dist[:, None]
    logits = jnp.where((dist >= 0)[:, None], logits, MASK_VALUE)
    p = jax.nn.softmax(logits, axis=-1).astype(v_sel.dtype)
    return jnp.einsum('bhtk,btkhd->bthd', p, v_sel)


def forget_attend(q, k, v, Fq, Fk, qpos, kpos):
    logits = jnp.einsum('bthd,bshd->bhts', q, k).astype(jnp.float32) * (HEAD_DIM ** -0.5)
    logits = logits + (jnp.transpose(Fq, (0, 2, 1))[:, :, :, None] - jnp.transpose(Fk, (0, 2, 1))[:, :, None, :])
    logits = jnp.where((kpos[None, :] <= qpos[:, None])[None, None], logits, MASK_VALUE)
    p = jax.nn.softmax(logits, axis=-1).astype(v.dtype)
    return jnp.einsum('bhts,bshd->bthd', p, v)


def prompt_mix(qa, ka, va, qb, kb, vb, qi, ki, wi, logf):
    B, S = qa.shape[:2]
    n_blk = S // Q_BLOCK
    k_top = topk_count(S)
    pos = jnp.arange(S)
    F = jnp.cumsum(logf.astype(jnp.float32), axis=1)

    def blocks(a):
        return jnp.moveaxis(a.reshape(B, n_blk, Q_BLOCK, *a.shape[2:]), 1, 0)

    def one_block(args):
        qa_b, qi_b, wi_b, qb_b, F_b, qpos = args
        scores = index_scores(qi_b, wi_b, ki, qpos, pos)
        _, sel = lax.top_k(scores, k_top)
        oa_b = sparse_attend(qa_b, take_rows(ka, sel), take_rows(va, sel), qpos, sel)
        ob_b = forget_attend(qb_b, kb, vb, F_b, F, qpos, pos)
        return oa_b, ob_b

    oa, ob = lax.map(one_block, (blocks(qa), blocks(qi), blocks(wi), blocks(qb), blocks(F), pos.reshape(n_blk, Q_BLOCK)))
    unblock = lambda a: jnp.moveaxis(a, 0, 1).reshape(B, S, *a.shape[3:])
    return unblock(oa), unblock(ob)


def sample_mix(layer, cache_k_a, cache_v_a, cache_k_idx, cache_k_b, cache_v_b, cache_logf_b, page_table,
               qa, ka, va, qb, kb, vb, qi, ki, wi, logf):
    DB, T = qa.shape[:2]
    page = cache_k_a.shape[2]
    past = page_table.shape[1] * page
    L = past + T
    qpos = past + jnp.arange(T)
    kpos = jnp.arange(L)

    def paged_rows(pool):
        return pool[layer, page_table].reshape(DB, past, *pool.shape[3:])

    def gather_selected(pool, new, sel):
        in_past = sel < past
        ps = jnp.minimum(sel, past - 1)
        phys = jax.vmap(lambda pt, i: pt[i])(page_table, ps // page)
        rows_past = pool[layer, phys, ps % page]
        rows_new = take_rows(new, jnp.clip(sel - past, 0, T - 1))
        return jnp.where(in_past[..., None, None], rows_past, rows_new)

    ki_all = jnp.concatenate([paged_rows(cache_k_idx), ki], axis=1)
    scores = index_scores(qi, wi, ki_all, qpos, kpos)
    _, sel = lax.top_k(scores, topk_count(L))
    oa = sparse_attend(qa, gather_selected(cache_k_a, ka, sel), gather_selected(cache_v_a, va, sel), qpos, sel)
    kb_all = jnp.concatenate([paged_rows(cache_k_b), kb], axis=1)
    vb_all = jnp.concatenate([paged_rows(cache_v_b), vb], axis=1)
    F = jnp.cumsum(jnp.concatenate([paged_rows(cache_logf_b), logf], axis=1).astype(jnp.float32), axis=1)
    ob = forget_attend(qb, kb_all, vb_all, F[:, past:], F, qpos, kpos)
    return oa, ob


def run_layer(x, c, mix, w_ada, b_ada, g_norm1, w_in, b_forget, g_qa, g_ka, g_qb, g_kb,
              w_o_a, w_o_b, w_out, g_norm2, w_up, w_down):
    B, T, _ = x.shape
    mod = (jax.nn.silu(c) @ w_ada + b_ada)[:, None, :]
    shift1, scale1, gate1, shift2, scale2, gate2 = jnp.split(mod, 6, axis=-1)
    h = rms_norm(x, g_norm1) * (1.0 + scale1) + shift1
    qa, ka, va, qb, kb, vb, qi, ki, wi, logf, ga, gb = project(h, w_in, b_forget, g_qa, g_ka, g_qb, g_kb)
    oa, ob = mix(qa, ka, va, qb, kb, vb, qi, ki, wi, logf)
    merged = (jax.nn.sigmoid(ga) * (oa.reshape(B, T, WIDTH_A) @ w_o_a)
              + jax.nn.sigmoid(gb) * (ob.reshape(B, T, WIDTH_B) @ w_o_b))
    x = x + gate1 * (merged @ w_out)
    h2 = rms_norm(x, g_norm2) * (1.0 + scale2) + shift2
    x = x + gate2 * (jnp.square(jax.nn.relu(h2 @ w_up)) @ w_down)
    return x, (ka, va, ki, kb, vb, logf)


def setup_inputs(seed: int = 0) -> dict:
    key = jax.random.key(seed)
    ks = jax.random.split(key, 32)
    n_pages = PAST_LEN // PAGE_SIZE
    n_pool = (5 * DEC_BATCH * n_pages) // 4
    f32 = jnp.float32
    nrm = lambda k, shape, s=1.0: (s * jax.random.normal(k, shape)).astype(f32)
    gain = lambda k, shape: (1.0 + 0.05 * jax.random.normal(k, shape)).astype(f32)
    x_prompt = nrm(ks[0], (BATCH, SEQ, D_MODEL))
    x_sample = nrm(ks[1], (DEC_BATCH, DEC_SEQ, D_MODEL))
    cache_k_a = nrm(ks[2], (DEPTH, n_pool, PAGE_SIZE, N_HEADS_A, HEAD_DIM))
    cache_v_a = nrm(ks[3], (DEPTH, n_pool, PAGE_SIZE, N_HEADS_A, HEAD_DIM))
    cache_k_idx = nrm(ks[4], (DEPTH, n_pool, PAGE_SIZE, IDX_DIM))
    cache_k_b = nrm(ks[5], (DEPTH, n_pool, PAGE_SIZE, N_HEADS_B, HEAD_DIM))
    cache_v_b = nrm(ks[6], (DEPTH, n_pool, PAGE_SIZE, N_HEADS_B, HEAD_DIM))
    cache_logf_b = jax.nn.log_sigmoid(FORGET_BIAS + jax.random.normal(ks[7], (DEPTH, n_pool, PAGE_SIZE, N_HEADS_B))).astype(f32)
    page_table = jax.random.permutation(ks[8], n_pool)[:DEC_BATCH * n_pages].reshape(DEC_BATCH, n_pages).astype(jnp.int32)
    c_prompt = nrm(ks[9], (BATCH, D_MODEL))
    c_sample = nrm(ks[10], (DEC_BATCH, D_MODEL))
    w_ada = nrm(ks[11], (DEPTH, D_MODEL, 6 * D_MODEL), 0.5 * D_MODEL ** -0.5)
    b_ada = nrm(ks[12], (DEPTH, 6 * D_MODEL), 0.02)
    g_norm1 = gain(ks[13], (DEPTH, D_MODEL))
    w_in = nrm(ks[14], (DEPTH, D_MODEL, D_IN), D_MODEL ** -0.5)
    b_forget = (FORGET_BIAS + 0.1 * jax.random.normal(ks[15], (DEPTH, N_HEADS_B))).astype(f32)
    g_qa = gain(ks[16], (DEPTH, HEAD_DIM))
    g_ka = gain(ks[17], (DEPTH, HEAD_DIM))
    g_qb = gain(ks[18], (DEPTH, HEAD_DIM))
    g_kb = gain(ks[19], (DEPTH, HEAD_DIM))
    w_o_a = nrm(ks[20], (DEPTH, WIDTH_A, D_MODEL), WIDTH_A ** -0.5)
    w_o_b = nrm(ks[21], (DEPTH, WIDTH_B, D_MODEL), WIDTH_B ** -0.5)
    w_out = nrm(ks[22], (DEPTH, D_MODEL, D_MODEL), D_MODEL ** -0.5)
    g_norm2 = gain(ks[23], (DEPTH, D_MODEL))
    w_up = nrm(ks[24], (DEPTH, D_MODEL, D_FF), D_MODEL ** -0.5)
    w_down = nrm(ks[25], (DEPTH, D_FF, D_MODEL), D_FF ** -0.5)
    return {'x_prompt': x_prompt, 'x_sample': x_sample,
            'cache_k_a': cache_k_a, 'cache_v_a': cache_v_a, 'cache_k_idx': cache_k_idx,
            'cache_k_b': cache_k_b, 'cache_v_b': cache_v_b, 'cache_logf_b': cache_logf_b,
            'page_table': page_table, 'c_prompt': c_prompt, 'c_sample': c_sample,
            'w_ada': w_ada, 'b_ada': b_ada, 'g_norm1': g_norm1, 'w_in': w_in, 'b_forget': b_forget,
            'g_qa': g_qa, 'g_ka': g_ka, 'g_qb': g_qb, 'g_kb': g_kb,
            'w_o_a': w_o_a, 'w_o_b': w_o_b, 'w_out': w_out,
            'g_norm2': g_norm2, 'w_up': w_up, 'w_down': w_down}


def reference(x_prompt, x_sample, cache_k_a, cache_v_a, cache_k_idx, cache_k_b, cache_v_b, cache_logf_b,
              page_table, c_prompt, c_sample, w_ada, b_ada, g_norm1, w_in, b_forget, g_qa, g_ka, g_qb, g_kb,
              w_o_a, w_o_b, w_out, g_norm2, w_up, w_down):
    y_p, y_s = x_prompt, x_sample
    states_p, states_s = [], []
    for l in range(DEPTH):
        lw = (w_ada[l], b_ada[l], g_norm1[l], w_in[l], b_forget[l], g_qa[l], g_ka[l], g_qb[l], g_kb[l],
              w_o_a[l], w_o_b[l], w_out[l], g_norm2[l], w_up[l], w_down[l])
        y_p, st_p = run_layer(y_p, c_prompt, prompt_mix, *lw)
        mix_s = functools.partial(sample_mix, l, cache_k_a, cache_v_a, cache_k_idx, cache_k_b, cache_v_b,
                                  cache_logf_b, page_table)
        y_s, st_s = run_layer(y_s, c_sample, mix_s, *lw)
        states_p.append(st_p)
        states_s.append(st_s)
    k_a_p, v_a_p, k_idx_p, k_b_p, v_b_p, logf_p = (jnp.stack(s) for s in zip(*states_p))
    k_a_s, v_a_s, k_idx_s, k_b_s, v_b_s, logf_s = (jnp.stack(s) for s in zip(*states_s))
    return (y_p, y_s, k_a_p, v_a_p, k_idx_p, k_b_p, v_b_p, logf_p, k_a_s, v_a_s, k_idx_s, k_b_s, v_b_s, logf_s)
```

```python
import functools

import numpy as np
import jax
import jax.numpy as jnp
from jax import lax
from jax.experimental import pallas as pl
from jax.experimental.pallas import tpu as pltpu

F32 = jnp.float32
BF16 = jnp.bfloat16
I32 = jnp.int32

D_MODEL = 1024
HEAD_DIM = 64
N_HEADS = 8
WIDTH = N_HEADS * HEAD_DIM
D_FF = 4 * D_MODEL
TOPK_MAX = 256
EPS = 1e-6
MASK_VALUE = -1e30
LANES = 128
QK_SCALE = HEAD_DIM ** -0.5
WI_SCALE = N_HEADS ** -0.5
ALIBI_SLOPES = tuple(2.0 ** (-8.0 * (h + 1) / N_HEADS) for h in range(N_HEADS))
INT_MIN = -(2 ** 31)
VMEM_LIMIT = 56 * 1024 * 1024

C_QA, C_KA, C_VA, C_QB, C_KB, C_VB, C_QI = (i * WIDTH for i in range(7))
C_SMALL = 7 * WIDTH
C_KI2 = C_SMALL + LANES
C_GA = C_KI2 + LANES
C_GB = C_GA + D_MODEL
C_TOTAL = C_GB + D_MODEL
SM_WI = HEAD_DIM
SM_LOGF = HEAD_DIM + N_HEADS


def _topk_count(n_keys):
    return max(1, min(TOPK_MAX, n_keys // 4))


def _sort_key(x):
    b = pltpu.bitcast(x, I32)
    key = b ^ ((b >> 31) & jnp.int32(0x7FFFFFFF))
    return jnp.where(key == -1, 0, key)


_KEY_MASK = int(np.float32(MASK_VALUE).view(np.int32)) ^ 0x7FFFFFFF
_KEY_MASK = _KEY_MASK - (1 << 32) if _KEY_MASK >= (1 << 31) else _KEY_MASK


def _const_spec(shape):
    zeros = (0,) * len(shape)
    return pl.BlockSpec(shape, lambda *_: zeros, pipeline_mode=pl.Buffered(1))


def _params(*sem):
    return pltpu.CompilerParams(dimension_semantics=sem, vmem_limit_bytes=VMEM_LIMIT)


def _nt_dot(a, b):
    return lax.dot_general(a, b, (((1,), (1,)), ((), ())), preferred_element_type=F32)


def _dot(a, b):
    return jnp.dot(a, b, preferred_element_type=F32)


def _split3(x):
    p1 = x.astype(BF16)
    r1 = x - p1.astype(F32)
    p2 = r1.astype(BF16)
    p3 = (r1 - p2.astype(F32)).astype(BF16)
    return p1, p2, p3


def _ada_kernel(c_ref, w_ref, b_ref, o_ref):
    c = c_ref[...]
    s = c * jax.nn.sigmoid(c)
    o_ref[...] = _dot(s.astype(BF16), w_ref[...].astype(BF16)) + b_ref[...]


def _ada_mod(c, w_ada, b_ada):
    rows = c.shape[0]
    n_out = w_ada.shape[1]
    tn = D_MODEL
    return pl.pallas_call(
        _ada_kernel,
        out_shape=jax.ShapeDtypeStruct((rows, n_out), F32),
        grid=(n_out // tn,),
        in_specs=[pl.BlockSpec((rows, D_MODEL), lambda j: (0, 0)),
                  pl.BlockSpec((D_MODEL, tn), lambda j: (0, j)),
                  pl.BlockSpec((1, tn), lambda j: (0, j))],
        out_specs=pl.BlockSpec((rows, tn), lambda j: (0, j)),
        compiler_params=_params("arbitrary"),
        name="ada_mod",
    )(c, w_ada, b_ada.reshape(1, n_out))


def _log_sigmoid(x):
    return jnp.minimum(x, 0.0) - jnp.log(1.0 + jnp.exp(-jnp.abs(x)))


def _in_proj_kernel(x_ref, shift_ref, scale_ref, g1_ref, w_ref, gains_ref, bsm_ref, gmat_ref,
                    ka_ref, va_ref, kb_ref, vb_ref, small_ref,
                    qa_bf, ka_bf, vat_bf, qb_bf, kb_bf, vb_bf, qi_bf, ki2_bf, smallt_ref, sga_ref, sgb_ref):
    x = x_ref[...]
    h = x * lax.rsqrt(jnp.mean(x * x, axis=-1, keepdims=True) + EPS) * g1_ref[...]
    h = h * (1.0 + scale_ref[...]) + shift_ref[...]
    hb = h.astype(BF16)

    def seg(col, width=WIDTH):
        return _dot(hb, w_ref[:, col:col + width])

    def head_norm(z, gi):
        zz = z * z
        hi = zz.astype(BF16)
        lo = (zz - hi.astype(F32)).astype(BF16)
        msq = _dot(hi, gmat_ref[...]) + _dot(lo, gmat_ref[...])
        return z * lax.rsqrt(msq + EPS) * gains_ref[gi:gi + 1, :]

    qa = head_norm(seg(C_QA), 0)
    qa_bf[...] = (qa * QK_SCALE).astype(BF16)
    ka = head_norm(seg(C_KA), 1)
    ka_ref[...] = ka
    ka_bf[...] = ka.astype(BF16)
    va = seg(C_VA)
    va_ref[...] = va
    vat_bf[...] = va.T.astype(BF16)
    qb = head_norm(seg(C_QB), 2)
    qb_bf[...] = (qb * QK_SCALE).astype(BF16)
    kb = head_norm(seg(C_KB), 3)
    kb_ref[...] = kb
    kb_bf[...] = kb.astype(BF16)
    vb = seg(C_VB)
    vb_ref[...] = vb
    vb_bf[...] = vb.astype(BF16)
    qi_bf[...] = seg(C_QI).astype(BF16)
    ki2_bf[...] = seg(C_KI2, LANES).astype(BF16)

    zs = seg(C_SMALL, LANES)
    lane = lax.broadcasted_iota(I32, zs.shape, 1)
    logf = _log_sigmoid(zs + bsm_ref[...])
    small = jnp.where(lane < SM_WI, zs,
                      jnp.where(lane < SM_LOGF, zs * WI_SCALE,
                                jnp.where(lane < SM_LOGF + N_HEADS, logf, 0.0)))
    small_ref[...] = small
    smallt_ref[...] = small.T

    sga_ref[...] = jax.nn.sigmoid(seg(C_GA, D_MODEL)).astype(BF16)
    sgb_ref[...] = jax.nn.sigmoid(seg(C_GB, D_MODEL)).astype(BF16)


def _in_proj(x, mod, g1, w_r, gains, bsm, gmat, tm):
    n_g, seq, _ = x.shape
    n_r = mod.shape[1]
    rb = 1 if n_r == 1 else tm
    nblk = seq // tm
    row = lambda width: pl.BlockSpec((None, tm, width), lambda g, i: (g, i, 0))
    modspec = lambda j: pl.BlockSpec((None, rb, D_MODEL), lambda g, i: (g, i if n_r > 1 else 0, j))
    sds = lambda shape, dt: jax.ShapeDtypeStruct(shape, dt)
    out_shape = (
        sds((n_g, seq, WIDTH), F32), sds((n_g, seq, WIDTH), F32), sds((n_g, seq, WIDTH), F32),
        sds((n_g, seq, WIDTH), F32), sds((n_g, seq, LANES), F32),
        sds((n_g, seq, WIDTH), BF16), sds((n_g, seq, WIDTH), BF16), sds((n_g, nblk, WIDTH, tm), BF16),
        sds((n_g, seq, WIDTH), BF16), sds((n_g, seq, WIDTH), BF16), sds((n_g, seq, WIDTH), BF16),
        sds((n_g, seq, WIDTH), BF16), sds((n_g, seq, LANES), BF16), sds((n_g, LANES, seq), F32),
        sds((n_g, seq, D_MODEL), BF16), sds((n_g, seq, D_MODEL), BF16),
    )
    out_specs = (
        row(WIDTH), row(WIDTH), row(WIDTH), row(WIDTH), row(LANES),
        row(WIDTH), row(WIDTH), pl.BlockSpec((None, None, WIDTH, tm), lambda g, i: (g, i, 0, 0)),
        row(WIDTH), row(WIDTH), row(WIDTH),
        row(WIDTH), row(LANES), pl.BlockSpec((None, LANES, tm), lambda g, i: (g, 0, i)),
        row(D_MODEL), row(D_MODEL),
    )
    return pl.pallas_call(
        _in_proj_kernel,
        out_shape=out_shape,
        grid=(n_g, nblk),
        in_specs=[row(D_MODEL), modspec(0), modspec(1), _const_spec((1, D_MODEL)),
                  _const_spec((D_MODEL, C_TOTAL)), _const_spec((4, WIDTH)), _const_spec((1, LANES)),
                  _const_spec((WIDTH, WIDTH))],
        out_specs=out_specs,
        compiler_params=_params("arbitrary", "arbitrary"),
        name="in_proj",
    )(x, mod, mod, g1, w_r, gains, bsm, gmat)


def _cumsum_kernel(x_ref, tri_ref, f_ref, ft_ref, carry_ref):
    @pl.when(pl.program_id(1) == 0)
    def _():
        carry_ref[...] = jnp.zeros_like(carry_ref)

    tri = tri_ref[...]
    p1, p2, p3 = _split3(x_ref[...])
    cs = _dot(tri, p1) + _dot(tri, p2) + _dot(tri, p3) + carry_ref[...]
    f_ref[...] = cs
    ft_ref[...] = cs.T
    n = cs.shape[0]
    carry_ref[...] = cs[n - 1:n, :]


def _cumsum_rows(small, tri, cb):
    n_b, seq, _ = small.shape
    nblk = seq // cb
    return pl.pallas_call(
        _cumsum_kernel,
        out_shape=(jax.ShapeDtypeStruct((n_b, seq, LANES), F32),
                   jax.ShapeDtypeStruct((n_b, nblk, LANES, cb), F32)),
        grid=(n_b, nblk),
        in_specs=[pl.BlockSpec((None, cb, LANES), lambda b, i: (b, i, 0)), _const_spec((cb, cb))],
        out_specs=(pl.BlockSpec((None, cb, LANES), lambda b, i: (b, i, 0)),
                   pl.BlockSpec((None, None, LANES, cb), lambda b, i: (b, i, 0, 0))),
        scratch_shapes=[pltpu.VMEM((1, LANES), F32)],
        compiler_params=_params("arbitrary", "arbitrary"),
        name="cumsum_logf",
    )(small, tri)


def _fox_kernel(q_ref, k_ref, v_ref, f_ref, ft_ref, o_ref, m_scr, l_scr, acc_scr, *, tq):
    qblk = pl.program_id(1)
    lane = lax.broadcasted_iota(I32, (tq, LANES), 1)
    upper = lane >= HEAD_DIM
    row = lax.broadcasted_iota(I32, (tq, tq), 0)
    col = lax.broadcasted_iota(I32, (tq, tq), 1)

    for hp in range(N_HEADS // 2):
        lanes = slice(LANES * hp, LANES * (hp + 1))
        q2 = q_ref[:, lanes]
        for j in range(2):
            h = 2 * hp + j
            qh = jnp.where(upper == (j == 1), q2, jnp.zeros_like(q2))
            fq = f_ref[:, SM_LOGF + h:SM_LOGF + h + 1]
            m_scr[...] = jnp.full(m_scr.shape, MASK_VALUE, F32)
            l_scr[...] = jnp.zeros_like(l_scr)
            acc_scr[j] = jnp.zeros((tq, LANES), F32)

            def step(kb, diagonal):
                k0 = pl.multiple_of(kb * tq, tq)
                s = _nt_dot(qh, k_ref[pl.ds(k0, tq), lanes])
                x = s + (fq - ft_ref[kb, h:h + 1, :])
                if diagonal:
                    x = jnp.where(col <= row, x, MASK_VALUE)
                m_old = m_scr[...]
                m_new = jnp.maximum(m_old, jnp.max(x, axis=1, keepdims=True))
                alpha = jnp.exp(m_old - m_new)
                p = jnp.exp(x - m_new)
                l_scr[...] = alpha * l_scr[...] + jnp.sum(p, axis=1, keepdims=True)
                acc_scr[j] = alpha * acc_scr[j] + _dot(p.astype(BF16), v_ref[pl.ds(k0, tq), lanes])
                m_scr[...] = m_new

            def full_step(kb, carry):
                step(kb, False)
                return carry

            lax.fori_loop(0, qblk, full_step, 0)
            step(qblk, True)
            acc_scr[j] = acc_scr[j] / l_scr[...]
        o_ref[:, lanes] = jnp.where(upper, acc_scr[1], acc_scr[0]).astype(BF16)


def _fox_prompt(qb, kb, vb, f_rows, f_t, tq):
    n_b, seq, _ = qb.shape
    nblk = seq // tq
    return pl.pallas_call(
        functools.partial(_fox_kernel, tq=tq),
        out_shape=jax.ShapeDtypeStruct((n_b, seq, WIDTH), BF16),
        grid=(n_b, nblk),
        in_specs=[pl.BlockSpec((None, tq, WIDTH), lambda b, i: (b, i, 0)),
                  pl.BlockSpec((None, seq, WIDTH), lambda b, i: (b, 0, 0)),
                  pl.BlockSpec((None, seq, WIDTH), lambda b, i: (b, 0, 0)),
                  pl.BlockSpec((None, tq, LANES), lambda b, i: (b, i, 0)),
                  pl.BlockSpec((None, nblk, N_HEADS, tq), lambda b, i: (b, 0, SM_LOGF // N_HEADS, 0))],
        out_specs=pl.BlockSpec((None, tq, WIDTH), lambda b, i: (b, i, 0)),
        scratch_shapes=[pltpu.VMEM((tq, 1), F32), pltpu.VMEM((tq, 1), F32), pltpu.VMEM((2, tq, LANES), F32)],
        compiler_params=_params("arbitrary", "arbitrary"),
        name="fox_prompt",
    )(qb, kb, vb, f_rows, f_t)


def _selection_bias(keys_scr, bias_scr, tri_ref, n_chunks, n_beyond, valid_fn, *, ck, tq, k_top):
    int_min = jnp.int32(INT_MIN)

    def count(cmp):
        def body(c, acc):
            c0 = pl.multiple_of(c * ck, ck)
            hit = jnp.where(cmp(keys_scr[pl.ds(c0, ck), :]), 1, 0)
            return acc + jnp.sum(hit.reshape(ck // 8, 8, tq), axis=0)
        acc = lax.fori_loop(0, n_chunks, body, jnp.zeros((8, tq), I32))
        return jnp.sum(acc, axis=0, keepdims=True)

    def bit_step(i, prefix):
        cand_u = prefix | lax.shift_left(jnp.int32(1), 31 - i)
        cand = cand_u ^ int_min
        cnt = count(lambda blk: blk >= cand) + jnp.where(cand <= _KEY_MASK, n_beyond, 0)
        return jnp.where(cnt >= k_top, cand_u, prefix)

    thr = lax.fori_loop(0, 32, bit_step, jnp.zeros((1, tq), I32)) ^ int_min
    n_greater = count(lambda blk: blk > thr) + jnp.where(thr < _KEY_MASK, n_beyond, 0)
    ties_wanted = (k_top - n_greater).astype(F32)

    def tie_step(c, ties_before):
        c0 = pl.multiple_of(c * ck, ck)
        blk = keys_scr[pl.ds(c0, ck), :]
        tie = blk == thr
        rank = _dot(tri_ref[...], jnp.where(tie, 1.0, 0.0).astype(BF16)) + ties_before
        keep = jnp.where(blk > thr, 0.0, jnp.where(tie, jnp.where(rank <= ties_wanted, 0.0, MASK_VALUE), MASK_VALUE))
        bias_scr[pl.ds(c0, ck), :] = jnp.where(valid_fn(c0), keep, MASK_VALUE)
        return rank[ck - 1:ck, :]

    lax.fori_loop(0, n_chunks, tie_step, jnp.zeros((1, tq), F32))


def _dsa_kernel(qi_ref, ki2_ref, w_ref, qa_ref, ka_ref, vat_ref, tri_ref, o_ref,
                qm_scr, qam_scr, keys_scr, bias_scr, m_scr, l_scr, acc_scr, *, tq, ck, k_top, seq):
    q0 = pl.program_id(1) * tq
    n_chunks = (q0 + tq + ck - 1) // ck
    n_beyond = seq - n_chunks * ck

    lane = lax.broadcasted_iota(I32, (tq, LANES), 1)
    upper = lane >= HEAD_DIM
    for h in range(N_HEADS):
        lanes = slice(LANES * (h // 2), LANES * (h // 2 + 1))
        pick = upper == (h % 2 == 1)
        qm_scr[h] = jnp.where(pick, qi_ref[:, lanes], jnp.zeros((tq, LANES), BF16))
        qam_scr[h] = jnp.where(pick, qa_ref[:, lanes], jnp.zeros((tq, LANES), BF16))

    w = w_ref[...] * QK_SCALE
    krow = lax.broadcasted_iota(I32, (ck, tq), 0)
    qcol = lax.broadcasted_iota(I32, (ck, tq), 1)

    def visible(c0):
        return (krow + c0) <= (qcol + q0)

    def score_chunk(c, carry):
        c0 = pl.multiple_of(c * ck, ck)
        kblk = ki2_ref[pl.ds(c0, ck), :]
        acc = jnp.zeros((ck, tq), F32)
        for h in range(N_HEADS):
            acc = acc + jnp.maximum(_nt_dot(kblk, qm_scr[h]), 0.0) * w[h:h + 1, :]
        keys_scr[pl.ds(c0, ck), :] = _sort_key(jnp.where(visible(c0), acc, MASK_VALUE))
        return carry

    lax.fori_loop(0, n_chunks, score_chunk, 0)

    _selection_bias(keys_scr, bias_scr, tri_ref, n_chunks, n_beyond, visible, ck=ck, tq=tq, k_top=k_top)

    m_scr[...] = jnp.full(m_scr.shape, MASK_VALUE, F32)
    l_scr[...] = jnp.zeros_like(l_scr)
    acc_scr[...] = jnp.zeros_like(acc_scr)
    rel0 = krow - qcol

    def attend_chunk(c, carry):
        c0 = pl.multiple_of(c * ck, ck)
        rel = (rel0 + (c0 - q0)).astype(F32)
        bias = bias_scr[pl.ds(c0, ck), :]
        for h in range(N_HEADS):
            lanes = slice(LANES * (h // 2), LANES * (h // 2 + 1))
            s = _nt_dot(ka_ref[pl.ds(c0, ck), lanes], qam_scr[h])
            x = s + ALIBI_SLOPES[h] * rel + bias
            m_old = m_scr[h]
            m_new = jnp.maximum(m_old, jnp.max(x, axis=0, keepdims=True))
            alpha = jnp.exp(m_old - m_new)
            p = jnp.exp(x - m_new)
            l_scr[h] = alpha * l_scr[h] + jnp.sum(p, axis=0, keepdims=True)
            vt = vat_ref[c, HEAD_DIM * h:HEAD_DIM * (h + 1), :]
            acc_scr[h] = alpha * acc_scr[h] + _dot(vt, p.astype(BF16))
            m_scr[h] = m_new
        return carry

    lax.fori_loop(0, n_chunks, attend_chunk, 0)

    out_t = jnp.concatenate([acc_scr[h] / l_scr[h] for h in range(N_HEADS)], axis=0)
    o_ref[...] = out_t.T.astype(BF16)


def _dsa_prompt(qi, ki2, small_t, qa, ka, va_t, tri, tq, ck):
    n_b, seq, _ = qi.shape
    nblk = seq // tq
    kern = functools.partial(_dsa_kernel, tq=tq, ck=ck, k_top=_topk_count(seq), seq=seq)
    return pl.pallas_call(
        kern,
        out_shape=jax.ShapeDtypeStruct((n_b, seq, WIDTH), BF16),
        grid=(n_b, nblk),
        in_specs=[pl.BlockSpec((None, tq, WIDTH), lambda b, i: (b, i, 0)),
                  pl.BlockSpec((None, seq, LANES), lambda b, i: (b, 0, 0)),
                  pl.BlockSpec((None, N_HEADS, tq), lambda b, i: (b, SM_WI // N_HEADS, i)),
                  pl.BlockSpec((None, tq, WIDTH), lambda b, i: (b, i, 0)),
                  pl.BlockSpec((None, seq, WIDTH), lambda b, i: (b, 0, 0)),
                  pl.BlockSpec((None, seq // ck, WIDTH, ck), lambda b, i: (b, 0, 0, 0)),
                  _const_spec((ck, ck))],
        out_specs=pl.BlockSpec((None, tq, WIDTH), lambda b, i: (b, i, 0)),
        scratch_shapes=[pltpu.VMEM((N_HEADS, tq, LANES), BF16), pltpu.VMEM((N_HEADS, tq, LANES), BF16),
                        pltpu.VMEM((seq, tq), I32), pltpu.VMEM((seq, tq), F32),
                        pltpu.VMEM((N_HEADS, 1, tq), F32), pltpu.VMEM((N_HEADS, 1, tq), F32),
                        pltpu.VMEM((N_HEADS, HEAD_DIM, tq), F32)],
        compiler_params=_params("arbitrary", "arbitrary"),
        name="dsa_prompt",
    )(qi, ki2, small_t, qa, ka, va_t, tri)


BF16_ROWS = 16


def _head_diag(n_rows):
    lane = lax.broadcasted_iota(I32, (n_rows, WIDTH), 1)
    sub = lax.broadcasted_iota(I32, (n_rows, WIDTH), 0)
    return (lane >= sub * HEAD_DIM) & (lane < (sub + 1) * HEAD_DIM)


def _sidx_kernel(pt_ref, q_ref, w_ref, knew_ref, *rest, n_pages, page):
    page_refs, o_ref = rest[:n_pages], rest[n_pages]
    q = q_ref[...]
    qp = jnp.concatenate([q, jnp.zeros((BF16_ROWS - N_HEADS, HEAD_DIM), F32)], axis=0).astype(BF16)
    w = w_ref[...] * QK_SCALE
    for p in range(n_pages):
        s = _dot(qp, page_refs[p][...].astype(BF16))[:N_HEADS]
        o_ref[:, page * p:page * (p + 1)] = jnp.sum(jnp.maximum(s, 0.0) * w, axis=0, keepdims=True)
    knew = knew_ref[...].astype(BF16).astype(F32)
    s_new = jnp.sum(q * knew, axis=1, keepdims=True)
    sc_new = jnp.sum(jnp.maximum(s_new, 0.0) * w, axis=0, keepdims=True)
    lane = lax.broadcasted_iota(I32, (1, LANES), 1)
    o_ref[:, page * n_pages:] = jnp.where(lane == 0, sc_new, 0.0)


def _sample_index_scores(layer, page_table, cache_k_idx, qi_s, wi_s, ki_s):
    n_db, n_pages = page_table.shape
    page = cache_k_idx.shape[2]
    l_pad = n_pages * page + LANES
    page_spec = lambda p: pl.BlockSpec((None, None, HEAD_DIM, page), lambda b, pt: (layer, pt[b, p], 0, 0))
    k_idx_t = jnp.swapaxes(cache_k_idx, 2, 3)
    grid_spec = pltpu.PrefetchScalarGridSpec(
        num_scalar_prefetch=1,
        grid=(n_db,),
        in_specs=[pl.BlockSpec((None, N_HEADS, HEAD_DIM), lambda b, pt: (b, 0, 0)),
                  pl.BlockSpec((None, N_HEADS, 1), lambda b, pt: (b, 0, 0)),
                  pl.BlockSpec((None, 1, HEAD_DIM), lambda b, pt: (b, 0, 0))]
                 + [page_spec(p) for p in range(n_pages)],
        out_specs=pl.BlockSpec((None, 1, l_pad), lambda b, pt: (b, 0, 0)),
    )
    out = pl.pallas_call(
        functools.partial(_sidx_kernel, n_pages=n_pages, page=page),
        out_shape=jax.ShapeDtypeStruct((n_db, 1, l_pad), F32),
        grid_spec=grid_spec,
        compiler_params=_params("arbitrary"),
        name="sample_index_scores",
    )(page_table, qi_s.astype(F32).reshape(n_db, N_HEADS, HEAD_DIM), wi_s.reshape(n_db, N_HEADS, 1),
      ki_s.reshape(n_db, 1, HEAD_DIM), *([k_idx_t] * n_pages))
    return out.reshape(n_db, l_pad)


def _ssel_kernel(s_ref, tri_ref, o_ref, keys_scr, *, n_keys, ck, k_top):
    l_pad, tq = s_ref.shape
    krow = lax.broadcasted_iota(I32, (ck, tq), 0)

    def valid(c0):
        return (krow + c0) < n_keys

    def to_keys(c, carry):
        c0 = pl.multiple_of(c * ck, ck)
        keys_scr[pl.ds(c0, ck), :] = jnp.where(valid(c0), _sort_key(s_ref[pl.ds(c0, ck), :]), jnp.int32(INT_MIN))
        return carry

    lax.fori_loop(0, l_pad // ck, to_keys, 0)
    _selection_bias(keys_scr, o_ref, tri_ref, l_pad // ck, 0, valid, ck=ck, tq=tq, k_top=k_top)


def _sample_selection(scores_t, tri, n_keys):
    l_pad, n_db = scores_t.shape
    return pl.pallas_call(
        functools.partial(_ssel_kernel, n_keys=n_keys, ck=LANES, k_top=_topk_count(n_keys)),
        out_shape=jax.ShapeDtypeStruct((l_pad, n_db), F32),
        grid=(1,),
        in_specs=[pl.BlockSpec((l_pad, n_db), lambda i: (0, 0)), pl.BlockSpec((LANES, LANES), lambda i: (0, 0))],
        out_specs=pl.BlockSpec((l_pad, n_db), lambda i: (0, 0)),
        scratch_shapes=[pltpu.VMEM((l_pad, n_db), I32)],
        compiler_params=_params("arbitrary"),
        name="sample_selection",
    )(scores_t, tri)


def _sattn_kernel(pt_ref, qa_ref, qb_ref, kan_ref, van_ref, kbn_ref, vbn_ref, lfn_ref, bias_ref, slope_ref, *rest,
                  n_pages, page):
    ka_p, va_p, kb_p, vb_p, lf_p = (rest[i * n_pages:(i + 1) * n_pages] for i in range(5))
    oa_ref, ob_ref = rest[5 * n_pages:]
    past = n_pages * page
    diag = _head_diag(N_HEADS)
    diag_padded = _head_diag(BF16_ROWS)

    def spread(q_ref):
        return jnp.where(diag_padded, jnp.broadcast_to(q_ref[...], (BF16_ROWS, WIDTH)), 0.0)

    def logits(qs, pages, new_ref):
        qb = qs.astype(BF16)
        s = jnp.concatenate([_dot(qb, pages[p][...].astype(BF16))[:N_HEADS] for p in range(n_pages)], axis=1)
        k_new = new_ref[...].astype(BF16).astype(F32)
        return s, jnp.sum(qs[:N_HEADS] * k_new, axis=1, keepdims=True)

    def attend(x, x_new, pages, new_ref):
        m = jnp.maximum(jnp.max(x, axis=1, keepdims=True), x_new)
        p = jnp.exp(x - m)
        p_new = jnp.exp(x_new - m)
        denom = jnp.sum(p, axis=1, keepdims=True) + p_new
        pb = jnp.concatenate([p, jnp.zeros_like(p)], axis=0).astype(BF16)
        acc = p_new.astype(BF16).astype(F32) * new_ref[...].astype(BF16).astype(F32)
        for q in range(n_pages):
            acc = acc + _nt_dot(pb[:, page * q:page * (q + 1)], pages[q][...].astype(BF16))[:N_HEADS]
        return jnp.sum(jnp.where(diag, acc / denom, 0.0), axis=0, keepdims=True)

    s, s_new = logits(spread(qa_ref), ka_p, kan_ref)
    rel = (lax.broadcasted_iota(I32, (1, past), 1) - past).astype(F32)
    bias = bias_ref[...]
    x = s + slope_ref[...] * rel + bias[:, :past]
    x_new = s_new + bias[:, past:past + 1]
    oa_ref[...] = attend(x, x_new, va_p, van_ref)

    lf = jnp.concatenate([lf_p[p][...] for p in range(n_pages)], axis=1)
    lane = lax.broadcasted_iota(I32, lf.shape, 1)
    d = 1
    while d < past:
        lf = lf + jnp.where(lane >= d, pltpu.roll(lf, d, 1), 0.0)
        d *= 2
    f_new = lf[:, past - 1:past] + lfn_ref[...]
    s, s_new = logits(spread(qb_ref), kb_p, kbn_ref)
    ob_ref[...] = attend(s + (f_new - lf), s_new, vb_p, vbn_ref)


def _sample_attention(layer, page_table, caches, logf_t, qa_s, qb_s, new_rows, logf_new, bias):
    cache_k_a, cache_v_a, cache_k_b, cache_v_b = caches
    n_db, n_pages = page_table.shape
    n_pool, page = cache_k_a.shape[1], cache_k_a.shape[2]
    l_pad = bias.shape[1]
    flat = lambda c: jnp.transpose(c, (0, 1, 3, 4, 2)).reshape(c.shape[0], n_pool, WIDTH, page)
    row = lambda width: pl.BlockSpec((None, 1, width), lambda b, pt: (b, 0, 0))
    kv_spec = lambda p: pl.BlockSpec((None, None, WIDTH, page), lambda b, pt: (layer, pt[b, p], 0, 0))
    lf_spec = lambda p: pl.BlockSpec((None, N_HEADS, page), lambda b, pt: (pt[b, p], 0, 0))
    pages = range(n_pages)
    grid_spec = pltpu.PrefetchScalarGridSpec(
        num_scalar_prefetch=1,
        grid=(n_db,),
        in_specs=[row(WIDTH), row(WIDTH), row(WIDTH), row(WIDTH), row(WIDTH), row(WIDTH),
                  pl.BlockSpec((None, N_HEADS, 1), lambda b, pt: (b, 0, 0)), row(l_pad),
                  pl.BlockSpec((N_HEADS, 1), lambda b, pt: (0, 0))]
                 + [kv_spec(p) for p in pages] * 4 + [lf_spec(p) for p in pages],
        out_specs=(row(WIDTH), row(WIDTH)),
    )
    r3 = lambda a: a.reshape(n_db, 1, a.shape[-1])
    slopes = jnp.asarray(ALIBI_SLOPES, F32).reshape(N_HEADS, 1)
    oa, ob = pl.pallas_call(
        functools.partial(_sattn_kernel, n_pages=n_pages, page=page),
        out_shape=(jax.ShapeDtypeStruct((n_db, 1, WIDTH), F32), jax.ShapeDtypeStruct((n_db, 1, WIDTH), F32)),
        grid_spec=grid_spec,
        compiler_params=_params("arbitrary"),
        name="sample_attention",
    )(page_table, r3(qa_s.astype(F32)), r3(qb_s.astype(F32)), *[r3(a) for a in new_rows], logf_new.reshape(n_db, N_HEADS, 1), r3(bias),
      slopes, *([flat(cache_k_a)] * n_pages), *([flat(cache_v_a)] * n_pages), *([flat(cache_k_b)] * n_pages),
      *([flat(cache_v_b)] * n_pages), *([logf_t] * n_pages))
    return oa, ob


def _post_kernel(x_ref, oa_ref, ob_ref, sga_ref, sgb_ref, gate1_ref, shift2_ref, scale2_ref, gate2_ref, g2_ref,
                 woa_ref, wob_ref, wout_ref, wup_ref, wdn_ref, y_ref, *, ff_chunk):
    merged = (sga_ref[...].astype(F32) * _dot(oa_ref[...], woa_ref[...])
              + sgb_ref[...].astype(F32) * _dot(ob_ref[...], wob_ref[...]))
    x1 = x_ref[...] + gate1_ref[...] * _dot(merged.astype(BF16), wout_ref[...])
    h2 = x1 * lax.rsqrt(jnp.mean(x1 * x1, axis=-1, keepdims=True) + EPS) * g2_ref[...]
    h2 = (h2 * (1.0 + scale2_ref[...]) + shift2_ref[...]).astype(BF16)
    mlp = jnp.zeros(x1.shape, F32)
    for c in range(0, D_FF, ff_chunk):
        u = jnp.maximum(_dot(h2, wup_ref[:, c:c + ff_chunk]), 0.0)
        mlp = mlp + _dot((u * u).astype(BF16), wdn_ref[c:c + ff_chunk, :])
    y_ref[...] = x1 + gate2_ref[...] * mlp


def _post(x, oa, ob, sga, sgb, mod, g2, woa, wob, wout, wup, wdn, tm):
    n_g, seq, _ = x.shape
    n_r = mod.shape[1]
    rb = 1 if n_r == 1 else tm
    row = lambda width: pl.BlockSpec((None, tm, width), lambda g, i: (g, i, 0))
    modspec = lambda j: pl.BlockSpec((None, rb, D_MODEL), lambda g, i: (g, i if n_r > 1 else 0, j))
    return pl.pallas_call(
        functools.partial(_post_kernel, ff_chunk=1024),
        out_shape=jax.ShapeDtypeStruct((n_g, seq, D_MODEL), F32),
        grid=(n_g, seq // tm),
        in_specs=[row(D_MODEL), row(WIDTH), row(WIDTH), row(D_MODEL), row(D_MODEL),
                  modspec(2), modspec(3), modspec(4), modspec(5), _const_spec((1, D_MODEL)),
                  _const_spec((WIDTH, D_MODEL)), _const_spec((WIDTH, D_MODEL)), _const_spec((D_MODEL, D_MODEL)),
                  _const_spec((D_MODEL, D_FF)), _const_spec((D_FF, D_MODEL))],
        out_specs=row(D_MODEL),
        compiler_params=_params("arbitrary", "arbitrary"),
        name="merge_out_mlp",
    )(x, oa, ob, sga, sgb, mod, mod, mod, mod, g2, woa, wob, wout, wup, wdn)


def _lower_tri(n):
    return jnp.asarray(np.tril(np.ones((n, n), np.float32)), BF16)


def _layer(l, x_p, x_s, caches, cache_k_idx, logf_t, page_table, mod_p, mod_s, lw, tiles):
    (g_norm1, w_r, gains, bsm, gmat, woa, wob, wout, g_norm2, wup, wdn) = lw
    tm, tq_fox, tq_dsa = tiles
    n_b, seq, _ = x_p.shape
    n_db = x_s.shape[0]

    (ka_p, va_p, kb_p, vb_p, small_p, qa_bf, ka_bf, vat_bf, qb_bf, kb_bf, vb_bf, qi_bf, ki2_bf, smallt_p,
     sga_p, sgb_p) = _in_proj(x_p, mod_p, g_norm1, w_r, gains, bsm, gmat, tm)
    f_rows, f_t = _cumsum_rows(small_p, _lower_tri(tq_fox), tq_fox)
    ob_p = _fox_prompt(qb_bf, kb_bf, vb_bf, f_rows, f_t, tq_fox)
    oa_p = _dsa_prompt(qi_bf, ki2_bf, smallt_p, qa_bf, ka_bf, vat_bf, _lower_tri(tm), tq_dsa, tm)
    y_p = _post(x_p, oa_p, ob_p, sga_p, sgb_p, mod_p, g_norm2, woa, wob, wout, wup, wdn, tm)

    xs = x_s.reshape(1, n_db, D_MODEL)
    (ka_s, va_s, kb_s, vb_s, small_s, qa_s, _, _, qb_s, _, _, qi_s, _, _, sga_s, sgb_s) = _in_proj(
        xs, mod_s, g_norm1, w_r, gains, bsm, gmat, n_db)
    small_s2 = small_s[0]
    ki_s, wi_s = small_s2[:, :SM_WI], small_s2[:, SM_WI:SM_LOGF]
    logf_s = small_s2[:, SM_LOGF:SM_LOGF + N_HEADS]
    n_keys = page_table.shape[1] * cache_k_idx.shape[2] + 1
    scores = _sample_index_scores(l, page_table, cache_k_idx, qi_s[0], wi_s, ki_s)
    bias_t = _sample_selection(scores.T, _lower_tri(LANES), n_keys)
    oa_s, ob_s = _sample_attention(l, page_table, caches, logf_t, qa_s[0], qb_s[0],
                                   (ka_s[0], va_s[0], kb_s[0], vb_s[0]), logf_s, bias_t.T)
    y_s = _post(xs, oa_s.reshape(1, n_db, WIDTH).astype(BF16), ob_s.reshape(1, n_db, WIDTH).astype(BF16),
                sga_s, sgb_s, mod_s, g_norm2, woa, wob, wout, wup, wdn, n_db)

    heads = lambda a, lead: a.reshape(*lead, N_HEADS, HEAD_DIM)
    lp, ls = (n_b, seq), (n_db, 1)
    state_p = (heads(ka_p, lp), heads(va_p, lp), small_p[..., :SM_WI], heads(kb_p, lp), heads(vb_p, lp),
               small_p[..., SM_LOGF:SM_LOGF + N_HEADS])
    state_s = (heads(ka_s[0], ls), heads(va_s[0], ls), ki_s.reshape(n_db, 1, HEAD_DIM), heads(kb_s[0], ls),
               heads(vb_s[0], ls), logf_s.reshape(n_db, 1, N_HEADS))
    return y_p, y_s.reshape(n_db, 1, D_MODEL), state_p, state_s


def kernel(x_prompt, x_sample, cache_k_a, cache_v_a, cache_k_idx, cache_k_b, cache_v_b, cache_logf_b, page_table,
           c_prompt, c_sample, w_ada, b_ada, g_norm1, w_in, b_forget, g_qa, g_ka, g_qb, g_kb, w_o_a, w_o_b, w_out,
           g_norm2, w_up, w_down):
    depth = w_in.shape[0]
    n_b, seq, _ = x_prompt.shape
    n_db, dec_seq, _ = x_sample.shape
    assert dec_seq == 1 and x_prompt.shape[-1] == D_MODEL
    assert cache_k_a.shape[3:] == (N_HEADS, HEAD_DIM) and cache_k_idx.shape[3] == HEAD_DIM
    tm = min(512, seq)
    tiles = (tm, min(512, seq), min(256, seq))

    c_all = jnp.concatenate([c_prompt, c_sample], axis=0)
    pad = (-c_all.shape[0]) % 8
    c_all = jnp.pad(c_all, ((0, pad), (0, 0)))
    gmat = jnp.asarray(np.kron(np.eye(N_HEADS, dtype=np.float32),
                               np.full((HEAD_DIM, HEAD_DIM), 1.0 / HEAD_DIM, np.float32)), BF16)
    split = 7 * WIDTH
    y_p, y_s = x_prompt, x_sample.reshape(n_db, D_MODEL)
    states_p, states_s = [], []
    for l in range(depth):
        mod = _ada_mod(c_all, w_ada[l], b_ada[l])
        mod_p = mod[:n_b].reshape(n_b, 1, 6 * D_MODEL)
        mod_s = mod[n_b:n_b + n_db].reshape(1, n_db, 6 * D_MODEL)
        w = w_in[l]
        k_idx_cols = w[:, split:split + HEAD_DIM]
        w_r = jnp.concatenate(
            [w[:, :split], w[:, split:split + HEAD_DIM + 2 * N_HEADS],
             jnp.zeros((D_MODEL, LANES - HEAD_DIM - 2 * N_HEADS), w.dtype),
             k_idx_cols, k_idx_cols, w[:, split + HEAD_DIM + 2 * N_HEADS:]], axis=1).astype(BF16)
        gains = jnp.stack([jnp.tile(g[l], N_HEADS) for g in (g_qa, g_ka, g_qb, g_kb)])
        bsm = jnp.zeros((1, LANES), F32).at[0, SM_LOGF:SM_LOGF + N_HEADS].set(b_forget[l])
        lw = (g_norm1[l].reshape(1, D_MODEL), w_r, gains, bsm, gmat, w_o_a[l].astype(BF16), w_o_b[l].astype(BF16),
              w_out[l].astype(BF16), g_norm2[l].reshape(1, D_MODEL), w_up[l].astype(BF16), w_down[l].astype(BF16))
        logf_t = jnp.swapaxes(cache_logf_b[l], 1, 2)
        y_p, y_s, st_p, st_s = _layer(l, y_p, y_s, (cache_k_a, cache_v_a, cache_k_b, cache_v_b), cache_k_idx,
                                      logf_t, page_table, mod_p, mod_s, lw, tiles)
        y_s = y_s.reshape(n_db, D_MODEL)
        states_p.append(st_p)
        states_s.append(st_s)
    k_a_p, v_a_p, k_idx_p, k_b_p, v_b_p, logf_p = (jnp.stack(s) for s in zip(*states_p))
    k_a_s, v_a_s, k_idx_s, k_b_s, v_b_s, logf_s = (jnp.stack(s) for s in zip(*states_s))
    return (y_p, y_s.reshape(n_db, 1, D_MODEL), k_a_p, v_a_p, k_idx_p, k_b_p, v_b_p, logf_p,
            k_a_s, v_a_s, k_idx_s, k_b_s, v_b_s, logf_s)
```

```python
import functools

import numpy as np
import jax
import jax.numpy as jnp
from jax import lax
from jax.experimental import pallas as pl
from jax.experimental.pallas import tpu as pltpu

F32 = jnp.float32
BF16 = jnp.bfloat16
I32 = jnp.int32

D_MODEL = 1024
HEAD_DIM = 64
N_HEADS = 8
WIDTH = N_HEADS * HEAD_DIM
D_FF = 4 * D_MODEL
TOPK_MAX = 256
EPS = 1e-6
MASK_VALUE = -1e30
LANES = 128
BF16_ROWS = 16
QK_SCALE = HEAD_DIM ** -0.5
WI_SCALE = N_HEADS ** -0.5
ALIBI_SLOPES = tuple(2.0 ** (-8.0 * (h + 1) / N_HEADS) for h in range(N_HEADS))
INT_MIN = -(2 ** 31)
VMEM_LIMIT = 56 * 1024 * 1024

C_QA, C_KA, C_VA, C_QB, C_KB, C_VB, C_QI = (i * WIDTH for i in range(7))
C_SMALL = 7 * WIDTH
C_KI2 = C_SMALL + LANES
C_GA = C_KI2 + LANES
C_GB = C_GA + D_MODEL
C_TOTAL = C_GB + D_MODEL
SM_WI = HEAD_DIM
SM_LOGF = HEAD_DIM + N_HEADS


def _topk_count(n_keys):
    return max(1, min(TOPK_MAX, n_keys // 4))


def _const_spec(shape):
    zeros = (0,) * len(shape)
    return pl.BlockSpec(shape, lambda *_: zeros, pipeline_mode=pl.Buffered(1))


def _params(*sem):
    return pltpu.CompilerParams(dimension_semantics=sem, vmem_limit_bytes=VMEM_LIMIT)


def _nt_dot(a, b):
    return lax.dot_general(a, b, (((1,), (1,)), ((), ())), preferred_element_type=F32)


def _dot(a, b):
    return jnp.dot(a, b, preferred_element_type=F32)


def _split3(x):
    p1 = x.astype(BF16)
    r1 = x - p1.astype(F32)
    p2 = r1.astype(BF16)
    p3 = (r1 - p2.astype(F32)).astype(BF16)
    return p1, p2, p3


def _ada_kernel(c_ref, w_ref, b_ref, o_ref):
    c = c_ref[...]
    s = c * jax.nn.sigmoid(c)
    o_ref[...] = _dot(s.astype(BF16), w_ref[...].astype(BF16)) + b_ref[...]


def _ada_mod(c, w_ada, b_ada):
    rows = c.shape[0]
    n_out = w_ada.shape[1]
    tn = D_MODEL
    return pl.pallas_call(
        _ada_kernel,
        out_shape=jax.ShapeDtypeStruct((rows, n_out), F32),
        grid=(n_out // tn,),
        in_specs=[pl.BlockSpec((rows, D_MODEL), lambda j: (0, 0)),
                  pl.BlockSpec((D_MODEL, tn), lambda j: (0, j)),
                  pl.BlockSpec((1, tn), lambda j: (0, j))],
        out_specs=pl.BlockSpec((rows, tn), lambda j: (0, j)),
        compiler_params=_params("arbitrary"),
        name="ada_mod",
    )(c, w_ada, b_ada.reshape(1, n_out))


def _log_sigmoid(x):
    return jnp.minimum(x, 0.0) - jnp.log(1.0 + jnp.exp(-jnp.abs(x)))


def _in_proj_kernel(x_ref, shift_ref, scale_ref, g1_ref, w_ref, gains_ref, bsm_ref, gmat_ref,
                    ka_ref, va_ref, kb_ref, vb_ref, small_ref,
                    qa_bf, ka_bf, vat_bf, qb_bf, kb_bf, vbt_bf, qi_bf, ki2_bf, smallt_ref, sga_ref, sgb_ref):
    x = x_ref[...]
    h = x * lax.rsqrt(jnp.mean(x * x, axis=-1, keepdims=True) + EPS) * g1_ref[...]
    h = h * (1.0 + scale_ref[...]) + shift_ref[...]
    hb = h.astype(BF16)

    def seg(col, width=WIDTH):
        return _dot(hb, w_ref[:, col:col + width])

    def head_norm(z, gi):
        zz = z * z
        hi = zz.astype(BF16)
        lo = (zz - hi.astype(F32)).astype(BF16)
        msq = _dot(hi, gmat_ref[...]) + _dot(lo, gmat_ref[...])
        return z * lax.rsqrt(msq + EPS) * gains_ref[gi:gi + 1, :]

    qa = head_norm(seg(C_QA), 0)
    qa_bf[...] = (qa * QK_SCALE).astype(BF16)
    ka = head_norm(seg(C_KA), 1)
    ka_ref[...] = ka
    ka_bf[...] = ka.astype(BF16)
    va = seg(C_VA)
    va_ref[...] = va
    vat_bf[...] = va.T.astype(BF16)
    qb = head_norm(seg(C_QB), 2)
    qb_bf[...] = (qb * QK_SCALE).astype(BF16)
    kb = head_norm(seg(C_KB), 3)
    kb_ref[...] = kb
    kb_bf[...] = kb.astype(BF16)
    vb = seg(C_VB)
    vb_ref[...] = vb
    vbt_bf[...] = vb.T.astype(BF16)
    qi_bf[...] = seg(C_QI).astype(BF16)
    ki2_bf[...] = seg(C_KI2, LANES).astype(BF16)

    zs = seg(C_SMALL, LANES)
    lane = lax.broadcasted_iota(I32, zs.shape, 1)
    logf = _log_sigmoid(zs + bsm_ref[...])
    small = jnp.where(lane < SM_WI, zs,
                      jnp.where(lane < SM_LOGF, zs * WI_SCALE,
                                jnp.where(lane < SM_LOGF + N_HEADS, logf, 0.0)))
    small_ref[...] = small
    smallt_ref[...] = small.T

    sga_ref[...] = jax.nn.sigmoid(seg(C_GA, D_MODEL)).astype(BF16)
    sgb_ref[...] = jax.nn.sigmoid(seg(C_GB, D_MODEL)).astype(BF16)


def _in_proj(x, mod, g1, w_r, gains, bsm, gmat, tm):
    n_g, seq, _ = x.shape
    n_r = mod.shape[1]
    rb = 1 if n_r == 1 else tm
    nblk = seq // tm
    row = lambda width: pl.BlockSpec((None, tm, width), lambda g, i: (g, i, 0))
    modspec = lambda j: pl.BlockSpec((None, rb, D_MODEL), lambda g, i: (g, i if n_r > 1 else 0, j))
    sds = lambda shape, dt: jax.ShapeDtypeStruct(shape, dt)
    out_shape = (
        sds((n_g, seq, WIDTH), F32), sds((n_g, seq, WIDTH), F32), sds((n_g, seq, WIDTH), F32),
        sds((n_g, seq, WIDTH), F32), sds((n_g, seq, LANES), F32),
        sds((n_g, seq, WIDTH), BF16), sds((n_g, seq, WIDTH), BF16), sds((n_g, nblk, WIDTH, tm), BF16),
        sds((n_g, seq, WIDTH), BF16), sds((n_g, seq, WIDTH), BF16), sds((n_g, nblk, WIDTH, tm), BF16),
        sds((n_g, seq, WIDTH), BF16), sds((n_g, seq, LANES), BF16), sds((n_g, LANES, seq), F32),
        sds((n_g, seq, D_MODEL), BF16), sds((n_g, seq, D_MODEL), BF16),
    )
    chunk_t = pl.BlockSpec((None, None, WIDTH, tm), lambda g, i: (g, i, 0, 0))
    out_specs = (
        row(WIDTH), row(WIDTH), row(WIDTH), row(WIDTH), row(LANES),
        row(WIDTH), row(WIDTH), chunk_t,
        row(WIDTH), row(WIDTH), chunk_t,
        row(WIDTH), row(LANES), pl.BlockSpec((None, LANES, tm), lambda g, i: (g, 0, i)),
        row(D_MODEL), row(D_MODEL),
    )
    return pl.pallas_call(
        _in_proj_kernel,
        out_shape=out_shape,
        grid=(n_g, nblk),
        in_specs=[row(D_MODEL), modspec(0), modspec(1), _const_spec((1, D_MODEL)),
                  _const_spec((D_MODEL, C_TOTAL)), _const_spec((4, WIDTH)), _const_spec((1, LANES)),
                  _const_spec((WIDTH, WIDTH))],
        out_specs=out_specs,
        compiler_params=_params("arbitrary", "arbitrary"),
        name="in_proj",
    )(x, mod, mod, g1, w_r, gains, bsm, gmat)


FX = 6


def _forget_placement():
    place = np.zeros((3, LANES, 2 * LANES), np.float32)
    const = np.zeros((1, 2 * LANES), np.float32)
    for h in range(N_HEADS):
        for i in range(3):
            place[i, SM_LOGF + h, FX * h + i] = 1.0
            place[i, SM_LOGF + h, LANES + FX * h + 3 + i] = 1.0
            const[0, FX * h + 3 + i] = 1.0
            const[0, LANES + FX * h + i] = -1.0
    return jnp.asarray(place, BF16), jnp.asarray(const, F32)


def _cumsum_kernel(x_ref, tri_ref, place_ref, const_ref, kx_ref, qx_ref, carry_ref):
    @pl.when(pl.program_id(1) == 0)
    def _():
        carry_ref[...] = jnp.zeros_like(carry_ref)

    tri = tri_ref[...]
    p1, p2, p3 = _split3(x_ref[...])
    cs = _dot(tri, p1) + _dot(tri, p2) + _dot(tri, p3) + carry_ref[...]
    n = cs.shape[0]
    carry_ref[...] = cs[n - 1:n, :]
    f1, f2, f3 = _split3(cs)
    ext = _dot(f1, place_ref[0]) + _dot(f2, place_ref[1]) + _dot(f3, place_ref[2]) + const_ref[...]
    kx_ref[...] = ext[:, :LANES].astype(BF16)
    qx_ref[...] = ext[:, LANES:].astype(BF16)


def _forget_operands(small, tri, cb):
    n_b, seq, _ = small.shape
    place, const = _forget_placement()
    blk = pl.BlockSpec((None, cb, LANES), lambda b, i: (b, i, 0))
    return pl.pallas_call(
        _cumsum_kernel,
        out_shape=(jax.ShapeDtypeStruct((n_b, seq, LANES), BF16), jax.ShapeDtypeStruct((n_b, seq, LANES), BF16)),
        grid=(n_b, seq // cb),
        in_specs=[blk, _const_spec((cb, cb)), _const_spec((3, LANES, 2 * LANES)), _const_spec((1, 2 * LANES))],
        out_specs=(blk, blk),
        scratch_shapes=[pltpu.VMEM((1, LANES), F32)],
        compiler_params=_params("arbitrary", "arbitrary"),
        name="cumsum_logf",
    )(small, tri, place, const)


def _flash_chunk(k_aug, qaug_scr, vt_of, x_scr, p_scr, m_scr, acc_scr, bias=None, visible=None):
    ck = x_scr.shape[1]
    ones = jnp.ones((BF16_ROWS, ck), BF16)
    alphas = []
    for h in range(N_HEADS):
        x = _nt_dot(k_aug(h // 2), qaug_scr[h])
        if bias is not None:
            x = x + bias
        if visible is not None:
            x = jnp.where(visible, x, MASK_VALUE)
        x_scr[h] = x
        m_old = m_scr[h]
        m_new = jnp.maximum(m_old, jnp.max(x, axis=0, keepdims=True))
        alphas.append(jnp.exp(m_old - m_new))
        m_scr[h] = m_new
    for h in range(N_HEADS):
        p_scr[h] = jnp.exp(x_scr[h] - m_scr[h]).astype(BF16)
    for h in range(N_HEADS):
        acc_scr[h] = alphas[h] * acc_scr[h] + _dot(jnp.concatenate([vt_of(h), ones], axis=0), p_scr[h])


def _flash_scratch(tq, ck):
    return [pltpu.VMEM((N_HEADS, tq, 2 * LANES), BF16), pltpu.VMEM((N_HEADS, ck, tq), F32),
            pltpu.VMEM((N_HEADS, ck, tq), BF16), pltpu.VMEM((N_HEADS, 1, tq), F32),
            pltpu.VMEM((N_HEADS, HEAD_DIM + BF16_ROWS, tq), F32)]


def _flash_init(m_scr, acc_scr):
    m_scr[...] = jnp.full(m_scr.shape, MASK_VALUE, F32)
    acc_scr[...] = jnp.zeros_like(acc_scr)


def _flash_output(o_ref, acc_scr):
    out_t = jnp.concatenate([acc_scr[h, :HEAD_DIM] / acc_scr[h, HEAD_DIM:HEAD_DIM + 1]
                             for h in range(N_HEADS)], axis=0)
    o_ref[...] = out_t.T.astype(BF16)


def _head_pick(tq, h):
    lane = lax.broadcasted_iota(I32, (tq, LANES), 1)
    return (lane >= HEAD_DIM) == (h % 2 == 1)


def _pair_lanes(h):
    return slice(LANES * (h // 2), LANES * (h // 2 + 1))


def _fox_kernel(q_ref, qx_ref, k_ref, kx_ref, vt_ref, o_ref, qaug_scr, x_scr, p_scr, m_scr, acc_scr, *, tq, ck):
    q0 = pl.program_id(1) * tq
    n_chunks = (q0 + tq + ck - 1) // ck
    lane = lax.broadcasted_iota(I32, (tq, LANES), 1)
    zeros = jnp.zeros((tq, LANES), BF16)
    qx = qx_ref[...]
    for h in range(N_HEADS):
        own = (lane >= FX * h) & (lane < FX * (h + 1))
        qaug_scr[h] = jnp.concatenate([jnp.where(_head_pick(tq, h), q_ref[:, _pair_lanes(h)], zeros),
                                       jnp.where(own, qx, zeros)], axis=1)
    _flash_init(m_scr, acc_scr)
    krow = lax.broadcasted_iota(I32, (ck, tq), 0)
    qcol = lax.broadcasted_iota(I32, (ck, tq), 1)

    def chunk(c, masked):
        c0 = pl.multiple_of(c * ck, ck)
        k_aug = lambda hp: jnp.concatenate([k_ref[pl.ds(c0, ck), LANES * hp:LANES * (hp + 1)],
                                            kx_ref[pl.ds(c0, ck), :]], axis=1)
        visible = ((krow + c0) <= (qcol + q0)) if masked else None
        _flash_chunk(k_aug, qaug_scr, lambda h: vt_ref[c, HEAD_DIM * h:HEAD_DIM * (h + 1), :],
                     x_scr, p_scr, m_scr, acc_scr, visible=visible)

    def full_chunk(c, carry):
        chunk(c, False)
        return carry

    lax.fori_loop(0, n_chunks - 1, full_chunk, 0)
    chunk(n_chunks - 1, True)
    _flash_output(o_ref, acc_scr)


def _fox_prompt(qb, qx, kb, kx, vb_t, tq, ck):
    n_b, seq, _ = qb.shape
    per_q = lambda width: pl.BlockSpec((None, tq, width), lambda b, i: (b, i, 0))
    whole = lambda width: pl.BlockSpec((None, seq, width), lambda b, i: (b, 0, 0))
    return pl.pallas_call(
        functools.partial(_fox_kernel, tq=tq, ck=ck),
        out_shape=jax.ShapeDtypeStruct((n_b, seq, WIDTH), BF16),
        grid=(n_b, seq // tq),
        in_specs=[per_q(WIDTH), per_q(LANES), whole(WIDTH), whole(LANES),
                  pl.BlockSpec((None, seq // ck, WIDTH, ck), lambda b, i: (b, 0, 0, 0))],
        out_specs=per_q(WIDTH),
        scratch_shapes=_flash_scratch(tq, ck),
        compiler_params=_params("arbitrary", "arbitrary"),
        name="fox_prompt",
    )(qb, qx, kb, kx, vb_t)


def _ordered_float(u):
    key = u ^ jnp.int32(INT_MIN)
    return pltpu.bitcast(key ^ ((key >> 31) & jnp.int32(0x7FFFFFFF)), F32)


def _selection_bias(score_scr, bias_scr, tri_ref, n_chunks, n_beyond, valid_fn, *, ck, tq, k_top):
    def count(cmp):
        def body(c, acc):
            c0 = pl.multiple_of(c * ck, ck)
            hit = jnp.where(cmp(score_scr[pl.ds(c0, ck), :]), 1, 0)
            return acc + jnp.sum(hit.reshape(ck // 8, 8, tq), axis=0)
        acc = lax.fori_loop(0, n_chunks, body, jnp.zeros((8, tq), I32))
        return jnp.sum(acc, axis=0, keepdims=True)

    def bit_step(i, code):
        cand_code = code | lax.shift_left(jnp.int32(1), 31 - i)
        cand = _ordered_float(cand_code)
        cnt = count(lambda blk: blk >= cand) + jnp.where(cand <= MASK_VALUE, n_beyond, 0)
        return jnp.where(cnt >= k_top, cand_code, code)

    thr = _ordered_float(lax.fori_loop(0, 32, bit_step, jnp.zeros((1, tq), I32)))
    n_greater = count(lambda blk: blk > thr) + jnp.where(thr < MASK_VALUE, n_beyond, 0)
    ties_wanted = (k_top - n_greater).astype(F32)

    def tie_step(c, ties_before):
        c0 = pl.multiple_of(c * ck, ck)
        blk = score_scr[pl.ds(c0, ck), :]
        tie = blk == thr
        rank = _dot(tri_ref[...], jnp.where(tie, 1.0, 0.0).astype(BF16)) + ties_before
        keep = jnp.where(blk > thr, 0.0, jnp.where(tie, jnp.where(rank <= ties_wanted, 0.0, MASK_VALUE), MASK_VALUE))
        bias_scr[pl.ds(c0, ck), :] = jnp.where(valid_fn(c0), keep, MASK_VALUE)
        return rank[ck - 1:ck, :]

    lax.fori_loop(0, n_chunks, tie_step, jnp.zeros((1, tq), F32))


POS_SPLIT = 256


def _alibi_lanes(pos, slope=None):
    lane = lax.broadcasted_iota(I32, pos.shape, 1)
    lo = pos & (POS_SPLIT - 1)
    hi_f, lo_f = (pos - lo).astype(F32), lo.astype(F32)
    if slope is None:
        val = jnp.where(lane == 0, hi_f, jnp.where(lane == 1, lo_f, jnp.where(lane < 4, 1.0, 0.0)))
    else:
        val = jnp.where(lane < 2, slope, jnp.where(lane == 2, hi_f * -slope, jnp.where(lane == 3, lo_f * -slope, 0.0)))
    return val.astype(BF16)


def _dsa_kernel(qi_ref, ki2_ref, w_ref, qa_ref, ka_ref, vat_ref, tri_ref, o_ref,
                qm_scr, score_scr, bias_scr, kx_scr, qaug_scr, x_scr, p_scr, m_scr, acc_scr, *, tq, ck, k_top, seq):
    q0 = pl.program_id(1) * tq
    n_chunks = (q0 + tq + ck - 1) // ck
    n_beyond = seq - n_chunks * ck

    zeros = jnp.zeros((tq, LANES), BF16)
    qpos = lax.broadcasted_iota(I32, (tq, LANES), 0) + q0
    for h in range(N_HEADS):
        pick = _head_pick(tq, h)
        qm_scr[h] = jnp.where(pick, qi_ref[:, _pair_lanes(h)], zeros)
        qaug_scr[h] = jnp.concatenate([jnp.where(pick, qa_ref[:, _pair_lanes(h)], zeros),
                                       _alibi_lanes(qpos, ALIBI_SLOPES[h])], axis=1)

    w = w_ref[...] * QK_SCALE
    krow = lax.broadcasted_iota(I32, (ck, tq), 0)
    qcol = lax.broadcasted_iota(I32, (ck, tq), 1)

    def visible(c0):
        return (krow + c0) <= (qcol + q0)

    def score_chunk(c, carry):
        c0 = pl.multiple_of(c * ck, ck)
        kblk = ki2_ref[pl.ds(c0, ck), :]
        acc = jnp.zeros((ck, tq), F32)
        for h in range(N_HEADS):
            acc = acc + jnp.maximum(_nt_dot(kblk, qm_scr[h]), 0.0) * w[h:h + 1, :]
        score_scr[pl.ds(c0, ck), :] = jnp.where(visible(c0), acc, MASK_VALUE)
        return carry

    lax.fori_loop(0, n_chunks, score_chunk, 0)

    _selection_bias(score_scr, bias_scr, tri_ref, n_chunks, n_beyond, visible, ck=ck, tq=tq, k_top=k_top)

    _flash_init(m_scr, acc_scr)
    kpos0 = lax.broadcasted_iota(I32, (ck, LANES), 0)

    def attend_chunk(c, carry):
        c0 = pl.multiple_of(c * ck, ck)
        kx_scr[...] = _alibi_lanes(kpos0 + c0)
        k_aug = lambda hp: jnp.concatenate([ka_ref[pl.ds(c0, ck), LANES * hp:LANES * (hp + 1)], kx_scr[...]], axis=1)
        _flash_chunk(k_aug, qaug_scr, lambda h: vat_ref[c, HEAD_DIM * h:HEAD_DIM * (h + 1), :],
                     x_scr, p_scr, m_scr, acc_scr, bias=bias_scr[pl.ds(c0, ck), :])
        return carry

    lax.fori_loop(0, n_chunks, attend_chunk, 0)
    _flash_output(o_ref, acc_scr)


def _dsa_prompt(qi, ki2, small_t, qa, ka, va_t, tri, tq, ck):
    n_b, seq, _ = qi.shape
    nblk = seq // tq
    kern = functools.partial(_dsa_kernel, tq=tq, ck=ck, k_top=_topk_count(seq), seq=seq)
    return pl.pallas_call(
        kern,
        out_shape=jax.ShapeDtypeStruct((n_b, seq, WIDTH), BF16),
        grid=(n_b, nblk),
        in_specs=[pl.BlockSpec((None, tq, WIDTH), lambda b, i: (b, i, 0)),
                  pl.BlockSpec((None, seq, LANES), lambda b, i: (b, 0, 0)),
                  pl.BlockSpec((None, N_HEADS, tq), lambda b, i: (b, SM_WI // N_HEADS, i)),
                  pl.BlockSpec((None, tq, WIDTH), lambda b, i: (b, i, 0)),
                  pl.BlockSpec((None, seq, WIDTH), lambda b, i: (b, 0, 0)),
                  pl.BlockSpec((None, seq // ck, WIDTH, ck), lambda b, i: (b, 0, 0, 0)),
                  _const_spec((ck, ck))],
        out_specs=pl.BlockSpec((None, tq, WIDTH), lambda b, i: (b, i, 0)),
        scratch_shapes=[pltpu.VMEM((N_HEADS, tq, LANES), BF16), pltpu.VMEM((seq, tq), F32),
                        pltpu.VMEM((seq, tq), F32), pltpu.VMEM((ck, LANES), BF16)] + _flash_scratch(tq, ck),
        compiler_params=_params("arbitrary", "arbitrary"),
        name="dsa_prompt",
    )(qi, ki2, small_t, qa, ka, va_t, tri)


def _head_diag(n_rows):
    lane = lax.broadcasted_iota(I32, (n_rows, WIDTH), 1)
    sub = lax.broadcasted_iota(I32, (n_rows, WIDTH), 0)
    return (lane >= sub * HEAD_DIM) & (lane < (sub + 1) * HEAD_DIM)


def _sidx_kernel(pt_ref, q_ref, w_ref, knew_ref, *rest, n_pages, page):
    page_refs, o_ref = rest[:n_pages], rest[n_pages]
    q = q_ref[...]
    qp = jnp.concatenate([q, jnp.zeros((BF16_ROWS - N_HEADS, HEAD_DIM), F32)], axis=0).astype(BF16)
    w = w_ref[...] * QK_SCALE
    for p in range(n_pages):
        s = _dot(qp, page_refs[p][...].astype(BF16))[:N_HEADS]
        o_ref[:, page * p:page * (p + 1)] = jnp.sum(jnp.maximum(s, 0.0) * w, axis=0, keepdims=True)
    knew = knew_ref[...].astype(BF16).astype(F32)
    s_new = jnp.sum(q * knew, axis=1, keepdims=True)
    sc_new = jnp.sum(jnp.maximum(s_new, 0.0) * w, axis=0, keepdims=True)
    lane = lax.broadcasted_iota(I32, (1, LANES), 1)
    o_ref[:, page * n_pages:] = jnp.where(lane == 0, sc_new, 0.0)


def _sample_index_scores(layer, page_table, cache_k_idx, qi_s, wi_s, ki_s):
    n_db, n_pages = page_table.shape
    page = cache_k_idx.shape[2]
    l_pad = n_pages * page + LANES
    page_spec = lambda p: pl.BlockSpec((None, None, HEAD_DIM, page), lambda b, pt: (layer, pt[b, p], 0, 0))
    k_idx_t = jnp.swapaxes(cache_k_idx, 2, 3)
    grid_spec = pltpu.PrefetchScalarGridSpec(
        num_scalar_prefetch=1,
        grid=(n_db,),
        in_specs=[pl.BlockSpec((None, N_HEADS, HEAD_DIM), lambda b, pt: (b, 0, 0)),
                  pl.BlockSpec((None, N_HEADS, 1), lambda b, pt: (b, 0, 0)),
                  pl.BlockSpec((None, 1, HEAD_DIM), lambda b, pt: (b, 0, 0))]
                 + [page_spec(p) for p in range(n_pages)],
        out_specs=pl.BlockSpec((None, 1, l_pad), lambda b, pt: (b, 0, 0)),
    )
    out = pl.pallas_call(
        functools.partial(_sidx_kernel, n_pages=n_pages, page=page),
        out_shape=jax.ShapeDtypeStruct((n_db, 1, l_pad), F32),
        grid_spec=grid_spec,
        compiler_params=_params("arbitrary"),
        name="sample_index_scores",
    )(page_table, qi_s.astype(F32).reshape(n_db, N_HEADS, HEAD_DIM), wi_s.reshape(n_db, N_HEADS, 1),
      ki_s.reshape(n_db, 1, HEAD_DIM), *([k_idx_t] * n_pages))
    return out.reshape(n_db, l_pad)


def _ssel_kernel(s_ref, tri_ref, o_ref, score_scr, *, n_keys, ck, k_top):
    l_pad, tq = s_ref.shape
    krow = lax.broadcasted_iota(I32, (ck, tq), 0)

    def valid(c0):
        return (krow + c0) < n_keys

    def drop_padding(c, carry):
        c0 = pl.multiple_of(c * ck, ck)
        score_scr[pl.ds(c0, ck), :] = jnp.where(valid(c0), s_ref[pl.ds(c0, ck), :], -jnp.inf)
        return carry

    lax.fori_loop(0, l_pad // ck, drop_padding, 0)
    _selection_bias(score_scr, o_ref, tri_ref, l_pad // ck, 0, valid, ck=ck, tq=tq, k_top=k_top)


def _sample_selection(scores_t, tri, n_keys):
    l_pad, n_db = scores_t.shape
    return pl.pallas_call(
        functools.partial(_ssel_kernel, n_keys=n_keys, ck=LANES, k_top=_topk_count(n_keys)),
        out_shape=jax.ShapeDtypeStruct((l_pad, n_db), F32),
        grid=(1,),
        in_specs=[pl.BlockSpec((l_pad, n_db), lambda i: (0, 0)), pl.BlockSpec((LANES, LANES), lambda i: (0, 0))],
        out_specs=pl.BlockSpec((l_pad, n_db), lambda i: (0, 0)),
        scratch_shapes=[pltpu.VMEM((l_pad, n_db), F32)],
        compiler_params=_params("arbitrary"),
        name="sample_selection",
    )(scores_t, tri)


def _sattn_kernel(pt_ref, qa_ref, qb_ref, kan_ref, van_ref, kbn_ref, vbn_ref, lfn_ref, bias_ref, slope_ref, *rest,
                  n_pages, page):
    ka_p, va_p, kb_p, vb_p, lf_p = (rest[i * n_pages:(i + 1) * n_pages] for i in range(5))
    oa_ref, ob_ref = rest[5 * n_pages:]
    past = n_pages * page
    diag = _head_diag(N_HEADS)
    diag_padded = _head_diag(BF16_ROWS)

    def spread(q_ref):
        return jnp.where(diag_padded, jnp.broadcast_to(q_ref[...], (BF16_ROWS, WIDTH)), 0.0)

    def logits(qs, pages, new_ref):
        qb = qs.astype(BF16)
        s = jnp.concatenate([_dot(qb, pages[p][...].astype(BF16))[:N_HEADS] for p in range(n_pages)], axis=1)
        k_new = new_ref[...].astype(BF16).astype(F32)
        return s, jnp.sum(qs[:N_HEADS] * k_new, axis=1, keepdims=True)

    def attend(x, x_new, pages, new_ref):
        m = jnp.maximum(jnp.max(x, axis=1, keepdims=True), x_new)
        p = jnp.exp(x - m)
        p_new = jnp.exp(x_new - m)
        denom = jnp.sum(p, axis=1, keepdims=True) + p_new
        pb = jnp.concatenate([p, jnp.zeros_like(p)], axis=0).astype(BF16)
        acc = p_new.astype(BF16).astype(F32) * new_ref[...].astype(BF16).astype(F32)
        for q in range(n_pages):
            acc = acc + _nt_dot(pb[:, page * q:page * (q + 1)], pages[q][...].astype(BF16))[:N_HEADS]
        return jnp.sum(jnp.where(diag, acc / denom, 0.0), axis=0, keepdims=True)

    s, s_new = logits(spread(qa_ref), ka_p, kan_ref)
    rel = (lax.broadcasted_iota(I32, (1, past), 1) - past).astype(F32)
    bias = bias_ref[...]
    x = s + slope_ref[...] * rel + bias[:, :past]
    x_new = s_new + bias[:, past:past + 1]
    oa_ref[...] = attend(x, x_new, va_p, van_ref)

    lf = jnp.concatenate([lf_p[p][...] for p in range(n_pages)], axis=1)
    lane = lax.broadcasted_iota(I32, lf.shape, 1)
    d = 1
    while d < past:
        lf = lf + jnp.where(lane >= d, pltpu.roll(lf, d, 1), 0.0)
        d *= 2
    f_new = lf[:, past - 1:past] + lfn_ref[...]
    s, s_new = logits(spread(qb_ref), kb_p, kbn_ref)
    ob_ref[...] = attend(s + (f_new - lf), s_new, vb_p, vbn_ref)


def _sample_attention(layer, page_table, caches, logf_t, qa_s, qb_s, new_rows, logf_new, bias):
    cache_k_a, cache_v_a, cache_k_b, cache_v_b = caches
    n_db, n_pages = page_table.shape
    n_pool, page = cache_k_a.shape[1], cache_k_a.shape[2]
    l_pad = bias.shape[1]
    flat = lambda c: jnp.transpose(c, (0, 1, 3, 4, 2)).reshape(c.shape[0], n_pool, WIDTH, page)
    row = lambda width: pl.BlockSpec((None, 1, width), lambda b, pt: (b, 0, 0))
    kv_spec = lambda p: pl.BlockSpec((None, None, WIDTH, page), lambda b, pt: (layer, pt[b, p], 0, 0))
    lf_spec = lambda p: pl.BlockSpec((None, N_HEADS, page), lambda b, pt: (pt[b, p], 0, 0))
    pages = range(n_pages)
    grid_spec = pltpu.PrefetchScalarGridSpec(
        num_scalar_prefetch=1,
        grid=(n_db,),
        in_specs=[row(WIDTH), row(WIDTH), row(WIDTH), row(WIDTH), row(WIDTH), row(WIDTH),
                  pl.BlockSpec((None, N_HEADS, 1), lambda b, pt: (b, 0, 0)), row(l_pad),
                  pl.BlockSpec((N_HEADS, 1), lambda b, pt: (0, 0))]
                 + [kv_spec(p) for p in pages] * 4 + [lf_spec(p) for p in pages],
        out_specs=(row(WIDTH), row(WIDTH)),
    )
    r3 = lambda a: a.reshape(n_db, 1, a.shape[-1])
    slopes = jnp.asarray(ALIBI_SLOPES, F32).reshape(N_HEADS, 1)
    oa, ob = pl.pallas_call(
        functools.partial(_sattn_kernel, n_pages=n_pages, page=page),
        out_shape=(jax.ShapeDtypeStruct((n_db, 1, WIDTH), F32), jax.ShapeDtypeStruct((n_db, 1, WIDTH), F32)),
        grid_spec=grid_spec,
        compiler_params=_params("arbitrary"),
        name="sample_attention",
    )(page_table, r3(qa_s.astype(F32)), r3(qb_s.astype(F32)), *[r3(a) for a in new_rows], logf_new.reshape(n_db, N_HEADS, 1), r3(bias),
      slopes, *([flat(cache_k_a)] * n_pages), *([flat(cache_v_a)] * n_pages), *([flat(cache_k_b)] * n_pages),
      *([flat(cache_v_b)] * n_pages), *([logf_t] * n_pages))
    return oa, ob


def _post_kernel(x_ref, oa_ref, ob_ref, sga_ref, sgb_ref, gate1_ref, shift2_ref, scale2_ref, gate2_ref, g2_ref,
                 woa_ref, wob_ref, wout_ref, wup_ref, wdn_ref, y_ref, *, ff_chunk):
    merged = (sga_ref[...].astype(F32) * _dot(oa_ref[...], woa_ref[...])
              + sgb_ref[...].astype(F32) * _dot(ob_ref[...], wob_ref[...]))
    x1 = x_ref[...] + gate1_ref[...] * _dot(merged.astype(BF16), wout_ref[...])
    h2 = x1 * lax.rsqrt(jnp.mean(x1 * x1, axis=-1, keepdims=True) + EPS) * g2_ref[...]
    h2 = (h2 * (1.0 + scale2_ref[...]) + shift2_ref[...]).astype(BF16)
    mlp = jnp.zeros(x1.shape, F32)
    for c in range(0, D_FF, ff_chunk):
        u = jnp.maximum(_dot(h2, wup_ref[:, c:c + ff_chunk]), 0.0)
        mlp = mlp + _dot((u * u).astype(BF16), wdn_ref[c:c + ff_chunk, :])
    y_ref[...] = x1 + gate2_ref[...] * mlp


def _post(x, oa, ob, sga, sgb, mod, g2, woa, wob, wout, wup, wdn, tm):
    n_g, seq, _ = x.shape
    n_r = mod.shape[1]
    rb = 1 if n_r == 1 else tm
    row = lambda width: pl.BlockSpec((None, tm, width), lambda g, i: (g, i, 0))
    modspec = lambda j: pl.BlockSpec((None, rb, D_MODEL), lambda g, i: (g, i if n_r > 1 else 0, j))
    return pl.pallas_call(
        functools.partial(_post_kernel, ff_chunk=1024),
        out_shape=jax.ShapeDtypeStruct((n_g, seq, D_MODEL), F32),
        grid=(n_g, seq // tm),
        in_specs=[row(D_MODEL), row(WIDTH), row(WIDTH), row(D_MODEL), row(D_MODEL),
                  modspec(2), modspec(3), modspec(4), modspec(5), _const_spec((1, D_MODEL)),
                  _const_spec((WIDTH, D_MODEL)), _const_spec((WIDTH, D_MODEL)), _const_spec((D_MODEL, D_MODEL)),
                  _const_spec((D_MODEL, D_FF)), _const_spec((D_FF, D_MODEL))],
        out_specs=row(D_MODEL),
        compiler_params=_params("arbitrary", "arbitrary"),
        name="merge_out_mlp",
    )(x, oa, ob, sga, sgb, mod, mod, mod, mod, g2, woa, wob, wout, wup, wdn)


def _lower_tri(n):
    return jnp.asarray(np.tril(np.ones((n, n), np.float32)), BF16)


def _layer(l, x_p, x_s, caches, cache_k_idx, logf_t, page_table, mod_p, mod_s, lw, tiles):
    (g_norm1, w_r, gains, bsm, gmat, woa, wob, wout, g_norm2, wup, wdn) = lw
    tm, tq = tiles
    n_b, seq, _ = x_p.shape
    n_db = x_s.shape[0]

    (ka_p, va_p, kb_p, vb_p, small_p, qa_bf, ka_bf, vat_bf, qb_bf, kb_bf, vbt_bf, qi_bf, ki2_bf, smallt_p,
     sga_p, sgb_p) = _in_proj(x_p, mod_p, g_norm1, w_r, gains, bsm, gmat, tm)
    kx_b, qx_b = _forget_operands(small_p, _lower_tri(tm), tm)
    ob_p = _fox_prompt(qb_bf, qx_b, kb_bf, kx_b, vbt_bf, tq, tm)
    oa_p = _dsa_prompt(qi_bf, ki2_bf, smallt_p, qa_bf, ka_bf, vat_bf, _lower_tri(tm), tq, tm)
    y_p = _post(x_p, oa_p, ob_p, sga_p, sgb_p, mod_p, g_norm2, woa, wob, wout, wup, wdn, tm)

    xs = x_s.reshape(1, n_db, D_MODEL)
    (ka_s, va_s, kb_s, vb_s, small_s, qa_s, _, _, qb_s, _, _, qi_s, _, _, sga_s, sgb_s) = _in_proj(
        xs, mod_s, g_norm1, w_r, gains, bsm, gmat, n_db)
    small_s2 = small_s[0]
    ki_s, wi_s = small_s2[:, :SM_WI], small_s2[:, SM_WI:SM_LOGF]
    logf_s = small_s2[:, SM_LOGF:SM_LOGF + N_HEADS]
    n_keys = page_table.shape[1] * cache_k_idx.shape[2] + 1
    scores = _sample_index_scores(l, page_table, cache_k_idx, qi_s[0], wi_s, ki_s)
    bias_t = _sample_selection(scores.T, _lower_tri(LANES), n_keys)
    oa_s, ob_s = _sample_attention(l, page_table, caches, logf_t, qa_s[0], qb_s[0],
                                   (ka_s[0], va_s[0], kb_s[0], vb_s[0]), logf_s, bias_t.T)
    y_s = _post(xs, oa_s.reshape(1, n_db, WIDTH).astype(BF16), ob_s.reshape(1, n_db, WIDTH).astype(BF16),
                sga_s, sgb_s, mod_s, g_norm2, woa, wob, wout, wup, wdn, n_db)

    heads = lambda a, lead: a.reshape(*lead, N_HEADS, HEAD_DIM)
    lp, ls = (n_b, seq), (n_db, 1)
    state_p = (heads(ka_p, lp), heads(va_p, lp), small_p[..., :SM_WI], heads(kb_p, lp), heads(vb_p, lp),
               small_p[..., SM_LOGF:SM_LOGF + N_HEADS])
    state_s = (heads(ka_s[0], ls), heads(va_s[0], ls), ki_s.reshape(n_db, 1, HEAD_DIM), heads(kb_s[0], ls),
               heads(vb_s[0], ls), logf_s.reshape(n_db, 1, N_HEADS))
    return y_p, y_s.reshape(n_db, 1, D_MODEL), state_p, state_s


def kernel(x_prompt, x_sample, cache_k_a, cache_v_a, cache_k_idx, cache_k_b, cache_v_b, cache_logf_b, page_table,
           c_prompt, c_sample, w_ada, b_ada, g_norm1, w_in, b_forget, g_qa, g_ka, g_qb, g_kb, w_o_a, w_o_b, w_out,
           g_norm2, w_up, w_down):
    depth = w_in.shape[0]
    n_b, seq, _ = x_prompt.shape
    n_db, dec_seq, _ = x_sample.shape
    assert dec_seq == 1 and x_prompt.shape[-1] == D_MODEL
    assert cache_k_a.shape[3:] == (N_HEADS, HEAD_DIM) and cache_k_idx.shape[3] == HEAD_DIM
    tiles = (min(512, seq), min(256, seq))

    c_all = jnp.concatenate([c_prompt, c_sample], axis=0)
    pad = (-c_all.shape[0]) % 8
    c_all = jnp.pad(c_all, ((0, pad), (0, 0)))
    gmat = jnp.asarray(np.kron(np.eye(N_HEADS, dtype=np.float32),
                               np.full((HEAD_DIM, HEAD_DIM), 1.0 / HEAD_DIM, np.float32)), BF16)
    split = 7 * WIDTH
    y_p, y_s = x_prompt, x_sample.reshape(n_db, D_MODEL)
    states_p, states_s = [], []
    for l in range(depth):
        mod = _ada_mod(c_all, w_ada[l], b_ada[l])
        mod_p = mod[:n_b].reshape(n_b, 1, 6 * D_MODEL)
        mod_s = mod[n_b:n_b + n_db].reshape(1, n_db, 6 * D_MODEL)
        w = w_in[l]
        k_idx_cols = w[:, split:split + HEAD_DIM]
        w_r = jnp.concatenate(
            [w[:, :split], w[:, split:split + HEAD_DIM + 2 * N_HEADS],
             jnp.zeros((D_MODEL, LANES - HEAD_DIM - 2 * N_HEADS), w.dtype),
             k_idx_cols, k_idx_cols, w[:, split + HEAD_DIM + 2 * N_HEADS:]], axis=1).astype(BF16)
        gains = jnp.stack([jnp.tile(g[l], N_HEADS) for g in (g_qa, g_ka, g_qb, g_kb)])
        bsm = jnp.zeros((1, LANES), F32).at[0, SM_LOGF:SM_LOGF + N_HEADS].set(b_forget[l])
        lw = (g_norm1[l].reshape(1, D_MODEL), w_r, gains, bsm, gmat, w_o_a[l].astype(BF16), w_o_b[l].astype(BF16),
              w_out[l].astype(BF16), g_norm2[l].reshape(1, D_MODEL), w_up[l].astype(BF16), w_down[l].astype(BF16))
        logf_t = jnp.swapaxes(cache_logf_b[l], 1, 2)
        y_p, y_s, st_p, st_s = _layer(l, y_p, y_s, (cache_k_a, cache_v_a, cache_k_b, cache_v_b), cache_k_idx,
                                      logf_t, page_table, mod_p, mod_s, lw, tiles)
        y_s = y_s.reshape(n_db, D_MODEL)
        states_p.append(st_p)
        states_s.append(st_s)
    k_a_p, v_a_p, k_idx_p, k_b_p, v_b_p, logf_p = (jnp.stack(s) for s in zip(*states_p))
    k_a_s, v_a_s, k_idx_s, k_b_s, v_b_s, logf_s = (jnp.stack(s) for s in zip(*states_s))
    return (y_p, y_s.reshape(n_db, 1, D_MODEL), k_a_p, v_a_p, k_idx_p, k_b_p, v_b_p, logf_p,
            k_a_s, v_a_s, k_idx_s, k_b_s, v_b_s, logf_s)
```

```python
import functools

import numpy as np
import jax
import jax.numpy as jnp
from jax import lax
from jax.experimental import pallas as pl
from jax.experimental.pallas import tpu as pltpu

F32 = jnp.float32
BF16 = jnp.bfloat16
I32 = jnp.int32

D_MODEL = 1024
HEAD_DIM = 64
N_HEADS = 8
WIDTH = N_HEADS * HEAD_DIM
D_FF = 4 * D_MODEL
TOPK_MAX = 256
EPS = 1e-6
MASK_VALUE = -1e30
LANES = 128
BF16_ROWS = 16
QK_SCALE = HEAD_DIM ** -0.5
WI_SCALE = N_HEADS ** -0.5
ALIBI_SLOPES = tuple(2.0 ** (-8.0 * (h + 1) / N_HEADS) for h in range(N_HEADS))
INT_MIN = -(2 ** 31)
VMEM_LIMIT = 56 * 1024 * 1024

C_QA, C_KA, C_VA, C_QB, C_KB, C_VB, C_QI = (i * WIDTH for i in range(7))
C_SMALL = 7 * WIDTH
C_KI2 = C_SMALL + LANES
C_GA = C_KI2 + LANES
C_GB = C_GA + D_MODEL
C_TOTAL = C_GB + D_MODEL
SM_WI = HEAD_DIM
SM_LOGF = HEAD_DIM + N_HEADS


def _topk_count(n_keys):
    return max(1, min(TOPK_MAX, n_keys // 4))


def _const_spec(shape):
    zeros = (0,) * len(shape)
    return pl.BlockSpec(shape, lambda *_: zeros, pipeline_mode=pl.Buffered(1))


def _params(*sem):
    return pltpu.CompilerParams(dimension_semantics=sem, vmem_limit_bytes=VMEM_LIMIT)


def _nt_dot(a, b):
    return lax.dot_general(a, b, (((1,), (1,)), ((), ())), preferred_element_type=F32)


def _dot(a, b):
    return jnp.dot(a, b, preferred_element_type=F32)


def _split3(x):
    p1 = x.astype(BF16)
    r1 = x - p1.astype(F32)
    p2 = r1.astype(BF16)
    p3 = (r1 - p2.astype(F32)).astype(BF16)
    return p1, p2, p3


def _ada_kernel(c_ref, w_ref, b_ref, o_ref):
    c = c_ref[...]
    s = c * jax.nn.sigmoid(c)
    o_ref[...] = _dot(s.astype(BF16), w_ref[...].astype(BF16)) + b_ref[...]


def _ada_mod(c, w_ada, b_ada):
    rows = c.shape[0]
    n_out = w_ada.shape[1]
    tn = D_MODEL
    return pl.pallas_call(
        _ada_kernel,
        out_shape=jax.ShapeDtypeStruct((rows, n_out), F32),
        grid=(n_out // tn,),
        in_specs=[pl.BlockSpec((rows, D_MODEL), lambda j: (0, 0)),
                  pl.BlockSpec((D_MODEL, tn), lambda j: (0, j)),
                  pl.BlockSpec((1, tn), lambda j: (0, j))],
        out_specs=pl.BlockSpec((rows, tn), lambda j: (0, j)),
        compiler_params=_params("arbitrary"),
        name="ada_mod",
    )(c, w_ada, b_ada.reshape(1, n_out))


def _log_sigmoid(x):
    return jnp.minimum(x, 0.0) - jnp.log(1.0 + jnp.exp(-jnp.abs(x)))


def _in_proj_kernel(x_ref, shift_ref, scale_ref, g1_ref, w_ref, gains_ref, bsm_ref, gmat_ref,
                    ka_ref, va_ref, kb_ref, vb_ref, small_ref,
                    qa_bf, ka_bf, vat_bf, qb_bf, kb_bf, vbt_bf, qi_bf, ki2_bf, smallt_ref, sga_ref, sgb_ref):
    x = x_ref[...]
    h = x * lax.rsqrt(jnp.mean(x * x, axis=-1, keepdims=True) + EPS) * g1_ref[...]
    h = h * (1.0 + scale_ref[...]) + shift_ref[...]
    hb = h.astype(BF16)

    def seg(col, width=WIDTH):
        return _dot(hb, w_ref[:, col:col + width])

    def head_norm(z, gi):
        msq = _dot((z * z).astype(BF16), gmat_ref[...])
        return z * lax.rsqrt(msq + EPS) * gains_ref[gi:gi + 1, :]

    qa = head_norm(seg(C_QA), 0)
    qa_bf[...] = (qa * QK_SCALE).astype(BF16)
    ka = head_norm(seg(C_KA), 1)
    ka_ref[...] = ka
    ka_bf[...] = ka.astype(BF16)
    va = seg(C_VA)
    va_ref[...] = va
    vat_bf[...] = va.T.astype(BF16)
    qb = head_norm(seg(C_QB), 2)
    qb_bf[...] = (qb * QK_SCALE).astype(BF16)
    kb = head_norm(seg(C_KB), 3)
    kb_ref[...] = kb
    for hp in range(N_HEADS // 2):
        kb_bf[hp] = kb[:, LANES * hp:LANES * (hp + 1)].astype(BF16)
    vb = seg(C_VB)
    vb_ref[...] = vb
    vbt_bf[...] = vb.T.astype(BF16)
    qi_bf[...] = seg(C_QI).astype(BF16)
    ki2_bf[...] = seg(C_KI2, LANES).astype(BF16)

    zs = seg(C_SMALL, LANES)
    lane = lax.broadcasted_iota(I32, zs.shape, 1)
    logf = _log_sigmoid(zs + bsm_ref[...])
    small = jnp.where(lane < SM_WI, zs,
                      jnp.where(lane < SM_LOGF, zs * WI_SCALE,
                                jnp.where(lane < SM_LOGF + N_HEADS, logf, 0.0)))
    small_ref[...] = small
    smallt_ref[...] = small.T

    sga_ref[...] = jax.nn.sigmoid(seg(C_GA, D_MODEL)).astype(BF16)
    sgb_ref[...] = jax.nn.sigmoid(seg(C_GB, D_MODEL)).astype(BF16)


def _in_proj(x, mod, g1, w_r, gains, bsm, gmat, tm):
    n_g, seq, _ = x.shape
    n_r = mod.shape[1]
    rb = 1 if n_r == 1 else tm
    nblk = seq // tm
    row = lambda width: pl.BlockSpec((None, tm, width), lambda g, i: (g, i, 0))
    modspec = lambda j: pl.BlockSpec((None, rb, D_MODEL), lambda g, i: (g, i if n_r > 1 else 0, j))
    sds = lambda shape, dt: jax.ShapeDtypeStruct(shape, dt)
    out_shape = (
        sds((n_g, seq, WIDTH), F32), sds((n_g, seq, WIDTH), F32), sds((n_g, seq, WIDTH), F32),
        sds((n_g, seq, WIDTH), F32), sds((n_g, seq, LANES), F32),
        sds((n_g, seq, WIDTH), BF16), sds((n_g, seq, WIDTH), BF16), sds((n_g, nblk, WIDTH, tm), BF16),
        sds((n_g, seq, WIDTH), BF16), sds((n_g, N_HEADS // 2, seq, LANES), BF16), sds((n_g, nblk, WIDTH, tm), BF16),
        sds((n_g, seq, WIDTH), BF16), sds((n_g, seq, LANES), BF16), sds((n_g, LANES, seq), F32),
        sds((n_g, seq, D_MODEL), BF16), sds((n_g, seq, D_MODEL), BF16),
    )
    chunk_t = pl.BlockSpec((None, None, WIDTH, tm), lambda g, i: (g, i, 0, 0))
    out_specs = (
        row(WIDTH), row(WIDTH), row(WIDTH), row(WIDTH), row(LANES),
        row(WIDTH), row(WIDTH), chunk_t,
        row(WIDTH), pl.BlockSpec((None, N_HEADS // 2, tm, LANES), lambda g, i: (g, 0, i, 0)), chunk_t,
        row(WIDTH), row(LANES), pl.BlockSpec((None, LANES, tm), lambda g, i: (g, 0, i)),
        row(D_MODEL), row(D_MODEL),
    )
    return pl.pallas_call(
        _in_proj_kernel,
        out_shape=out_shape,
        grid=(n_g, nblk),
        in_specs=[row(D_MODEL), modspec(0), modspec(1), _const_spec((1, D_MODEL)),
                  _const_spec((D_MODEL, C_TOTAL)), _const_spec((4, WIDTH)), _const_spec((1, LANES)),
                  _const_spec((WIDTH, WIDTH))],
        out_specs=out_specs,
        compiler_params=_params("arbitrary", "arbitrary"),
        name="in_proj",
    )(x, mod, mod, g1, w_r, gains, bsm, gmat)


FX = 6


def _forget_placement():
    place = np.zeros((3, LANES, 2 * LANES), np.float32)
    const = np.zeros((1, 2 * LANES), np.float32)
    for h in range(N_HEADS):
        for i in range(3):
            place[i, SM_LOGF + h, FX * h + i] = 1.0
            place[i, SM_LOGF + h, LANES + FX * h + 3 + i] = 1.0
            const[0, FX * h + 3 + i] = 1.0
            const[0, LANES + FX * h + i] = -1.0
    return jnp.asarray(place, BF16), jnp.asarray(const, F32)


def _cumsum_kernel(x_ref, tri_ref, place_ref, const_ref, kx_ref, qx_ref, carry_ref):
    @pl.when(pl.program_id(1) == 0)
    def _():
        carry_ref[...] = jnp.zeros_like(carry_ref)

    tri = tri_ref[...]
    p1, p2, p3 = _split3(x_ref[...])
    cs = _dot(tri, p1) + _dot(tri, p2) + _dot(tri, p3) + carry_ref[...]
    n = cs.shape[0]
    carry_ref[...] = cs[n - 1:n, :]
    f1, f2, f3 = _split3(cs)
    ext = _dot(f1, place_ref[0]) + _dot(f2, place_ref[1]) + _dot(f3, place_ref[2]) + const_ref[...]
    kx_ref[...] = ext[:, :LANES].astype(BF16)
    qx_ref[...] = ext[:, LANES:].astype(BF16)


def _forget_operands(small, tri, cb):
    n_b, seq, _ = small.shape
    place, const = _forget_placement()
    blk = pl.BlockSpec((None, cb, LANES), lambda b, i: (b, i, 0))
    return pl.pallas_call(
        _cumsum_kernel,
        out_shape=(jax.ShapeDtypeStruct((n_b, seq, LANES), BF16), jax.ShapeDtypeStruct((n_b, seq, LANES), BF16)),
        grid=(n_b, seq // cb),
        in_specs=[blk, _const_spec((cb, cb)), _const_spec((3, LANES, 2 * LANES)), _const_spec((1, 2 * LANES))],
        out_specs=(blk, blk),
        scratch_shapes=[pltpu.VMEM((1, LANES), F32)],
        compiler_params=_params("arbitrary", "arbitrary"),
        name="cumsum_logf",
    )(small, tri, place, const)


def _flash_chunk(k_aug, qaug_scr, vt_of, x_scr, p_scr, m_scr, acc_scr, bias=None, visible=None):
    n_slots, ck = x_scr.shape[0], x_scr.shape[1]
    ones = jnp.ones((BF16_ROWS, ck), BF16)
    alphas = []
    for s in range(n_slots):
        x = _nt_dot(k_aug(s), qaug_scr[s])
        if bias is not None:
            x = x + bias
        if visible is not None:
            x = jnp.where(visible, x, MASK_VALUE)
        x_scr[s] = x
        m_old = m_scr[s]
        m_new = jnp.maximum(m_old, jnp.max(x, axis=0, keepdims=True))
        alphas.append(jnp.exp(m_old - m_new))
        m_scr[s] = m_new
    for s in range(n_slots):
        p_scr[s] = jnp.exp(x_scr[s] - m_scr[s]).astype(BF16)
    for s in range(n_slots):
        acc_scr[s] = alphas[s] * acc_scr[s] + _dot(jnp.concatenate([vt_of(s), ones], axis=0), p_scr[s])


def _flash_scratch(n_slots, tq, ck):
    return [pltpu.VMEM((n_slots, tq, 2 * LANES), BF16), pltpu.VMEM((n_slots, ck, tq), F32),
            pltpu.VMEM((n_slots, ck, tq), BF16), pltpu.VMEM((n_slots, 1, tq), F32),
            pltpu.VMEM((n_slots, HEAD_DIM + BF16_ROWS, tq), F32)]


def _flash_init(m_scr, acc_scr):
    m_scr[...] = jnp.full(m_scr.shape, MASK_VALUE, F32)
    acc_scr[...] = jnp.zeros_like(acc_scr)


def _flash_output(o_ref, acc_scr):
    out_t = jnp.concatenate([acc_scr[s, :HEAD_DIM] / acc_scr[s, HEAD_DIM:HEAD_DIM + 1]
                             for s in range(acc_scr.shape[0])], axis=0)
    o_ref[...] = out_t.T.astype(BF16)


def _head_pick(tq, h):
    lane = lax.broadcasted_iota(I32, (tq, LANES), 1)
    return (lane >= HEAD_DIM) == (h % 2 == 1)


def _pair_lanes(h):
    return slice(LANES * (h // 2), LANES * (h // 2 + 1))


def _ordered_float(u):
    key = u ^ jnp.int32(INT_MIN)
    return pltpu.bitcast(key ^ ((key >> 31) & jnp.int32(0x7FFFFFFF)), F32)


def _selection_bias(score_scr, bias_scr, tri_ref, n_chunks, n_beyond, valid_fn, *, ck, tq, k_top):
    n_acc = 4

    def count(cmp):
        def body(c, acc):
            c0 = pl.multiple_of(c * ck, ck)
            hit = jnp.where(cmp(score_scr[pl.ds(c0, ck), :]), 1, 0)
            return acc + jnp.sum(hit.reshape(ck // (8 * n_acc), n_acc, 8, tq), axis=0)
        acc = lax.fori_loop(0, n_chunks, body, jnp.zeros((n_acc, 8, tq), I32))
        return jnp.sum(jnp.sum(acc, axis=0), axis=0, keepdims=True)

    def bit_step(i, carry):
        code, reached = carry
        cand_code = code | lax.shift_left(jnp.int32(1), 31 - i)
        cand = _ordered_float(cand_code)
        cnt = count(lambda blk: blk >= cand) + jnp.where(cand <= MASK_VALUE, n_beyond, 0)
        take = cnt >= k_top
        return jnp.where(take, cand_code, code), jnp.where(take, cnt, reached)

    code, reached = lax.fori_loop(0, 32, bit_step, (jnp.zeros((1, tq), I32), jnp.full((1, tq), k_top, I32)))
    thr = _ordered_float(code)

    def exact_cut():
        def body(c, carry):
            c0 = pl.multiple_of(c * ck, ck)
            keep = jnp.where(score_scr[pl.ds(c0, ck), :] >= thr, 0.0, MASK_VALUE)
            bias_scr[pl.ds(c0, ck), :] = jnp.where(valid_fn(c0), keep, MASK_VALUE)
            return carry
        lax.fori_loop(0, n_chunks, body, 0)

    def ranked_cut():
        n_greater = count(lambda blk: blk > thr) + jnp.where(thr < MASK_VALUE, n_beyond, 0)
        ties_wanted = (k_top - n_greater).astype(F32)

        def body(c, ties_before):
            c0 = pl.multiple_of(c * ck, ck)
            blk = score_scr[pl.ds(c0, ck), :]
            tie = blk == thr
            rank = _dot(tri_ref[...], jnp.where(tie, 1.0, 0.0).astype(BF16)) + ties_before
            keep = jnp.where(blk > thr, 0.0,
                             jnp.where(tie, jnp.where(rank <= ties_wanted, 0.0, MASK_VALUE), MASK_VALUE))
            bias_scr[pl.ds(c0, ck), :] = jnp.where(valid_fn(c0), keep, MASK_VALUE)
            return rank[ck - 1:ck, :]
        lax.fori_loop(0, n_chunks, body, jnp.zeros((1, tq), F32))

    lax.cond(jnp.max(reached) > k_top, ranked_cut, exact_cut)


POS_SPLIT = 256


def _alibi_lanes(pos, slope=None):
    lane = lax.broadcasted_iota(I32, pos.shape, 1)
    lo = pos & (POS_SPLIT - 1)
    hi_f, lo_f = (pos - lo).astype(F32), lo.astype(F32)
    if slope is None:
        val = jnp.where(lane == 0, hi_f, jnp.where(lane == 1, lo_f, jnp.where(lane < 4, 1.0, 0.0)))
    else:
        val = jnp.where(lane < 2, slope, jnp.where(lane == 2, hi_f * -slope, jnp.where(lane == 3, lo_f * -slope, 0.0)))
    return val.astype(BF16)


def _dsa_kernel(qi_ref, ki2_ref, w_ref, qa_ref, ka_ref, vat_ref, tri_ref, o_ref,
                qm_scr, score_scr, bias_scr, kx_scr, qaug_scr, x_scr, p_scr, m_scr, acc_scr, *, tq, ck, k_top, seq):
    q0 = pl.program_id(1) * tq
    n_chunks = (q0 + tq + ck - 1) // ck
    n_beyond = seq - n_chunks * ck

    zeros = jnp.zeros((tq, LANES), BF16)
    qpos = lax.broadcasted_iota(I32, (tq, LANES), 0) + q0
    for h in range(N_HEADS):
        pick = _head_pick(tq, h)
        qm_scr[h] = jnp.where(pick, qi_ref[:, _pair_lanes(h)], zeros)
        qaug_scr[h] = jnp.concatenate([jnp.where(pick, qa_ref[:, _pair_lanes(h)], zeros),
                                       _alibi_lanes(qpos, ALIBI_SLOPES[h])], axis=1)

    w = w_ref[...] * QK_SCALE
    krow = lax.broadcasted_iota(I32, (ck, tq), 0)
    qcol = lax.broadcasted_iota(I32, (ck, tq), 1)

    def visible(c0):
        return (krow + c0) <= (qcol + q0)

    def score_chunk(c, carry):
        c0 = pl.multiple_of(c * ck, ck)
        kblk = ki2_ref[pl.ds(c0, ck), :]
        acc = jnp.zeros((ck, tq), F32)
        for h in range(N_HEADS):
            acc = acc + jnp.maximum(_nt_dot(kblk, qm_scr[h]), 0.0) * w[h:h + 1, :]
        score_scr[pl.ds(c0, ck), :] = jnp.where(visible(c0), acc, MASK_VALUE)
        return carry

    lax.fori_loop(0, n_chunks, score_chunk, 0)

    _selection_bias(score_scr, bias_scr, tri_ref, n_chunks, n_beyond, visible, ck=ck, tq=tq, k_top=k_top)

    _flash_init(m_scr, acc_scr)
    kpos0 = lax.broadcasted_iota(I32, (ck, LANES), 0)

    def attend_chunk(c, carry):
        c0 = pl.multiple_of(c * ck, ck)
        kx_scr[...] = _alibi_lanes(kpos0 + c0)
        k_aug = lambda h: jnp.concatenate([ka_ref[pl.ds(c0, ck), _pair_lanes(h)], kx_scr[...]], axis=1)
        _flash_chunk(k_aug, qaug_scr, lambda h: vat_ref[c, HEAD_DIM * h:HEAD_DIM * (h + 1), :],
                     x_scr, p_scr, m_scr, acc_scr, bias=bias_scr[pl.ds(c0, ck), :])
        return carry

    lax.fori_loop(0, n_chunks, attend_chunk, 0)
    _flash_output(o_ref, acc_scr)


def _dsa_prompt(qi, ki2, small_t, qa, ka, va_t, tri, tq, ck):
    n_b, seq, _ = qi.shape
    nblk = seq // tq
    kern = functools.partial(_dsa_kernel, tq=tq, ck=ck, k_top=_topk_count(seq), seq=seq)
    return pl.pallas_call(
        kern,
        out_shape=jax.ShapeDtypeStruct((n_b, seq, WIDTH), BF16),
        grid=(n_b, nblk),
        in_specs=[pl.BlockSpec((None, tq, WIDTH), lambda b, i: (b, i, 0)),
                  pl.BlockSpec((None, seq, LANES), lambda b, i: (b, 0, 0)),
                  pl.BlockSpec((None, N_HEADS, tq), lambda b, i: (b, SM_WI // N_HEADS, i)),
                  pl.BlockSpec((None, tq, WIDTH), lambda b, i: (b, i, 0)),
                  pl.BlockSpec((None, seq, WIDTH), lambda b, i: (b, 0, 0)),
                  pl.BlockSpec((None, seq // ck, WIDTH, ck), lambda b, i: (b, 0, 0, 0)),
                  _const_spec((ck, ck))],
        out_specs=pl.BlockSpec((None, tq, WIDTH), lambda b, i: (b, i, 0)),
        scratch_shapes=[pltpu.VMEM((N_HEADS, tq, LANES), BF16), pltpu.VMEM((seq, tq), F32),
                        pltpu.VMEM((seq, tq), F32), pltpu.VMEM((ck, LANES), BF16)]
                       + _flash_scratch(N_HEADS, tq, ck),
        compiler_params=_params("arbitrary", "arbitrary"),
        name="dsa_prompt",
    )(qi, ki2, small_t, qa, ka, va_t, tri)


def _head_diag(n_rows):
    lane = lax.broadcasted_iota(I32, (n_rows, WIDTH), 1)
    sub = lax.broadcasted_iota(I32, (n_rows, WIDTH), 0)
    return (lane >= sub * HEAD_DIM) & (lane < (sub + 1) * HEAD_DIM)


def _sidx_kernel(pt_ref, q_ref, w_ref, knew_ref, *rest, n_pages, page):
    page_refs, o_ref = rest[:n_pages], rest[n_pages]
    q = q_ref[...]
    qp = jnp.concatenate([q, jnp.zeros((BF16_ROWS - N_HEADS, HEAD_DIM), F32)], axis=0).astype(BF16)
    w = w_ref[...] * QK_SCALE
    for p in range(n_pages):
        s = _dot(qp, page_refs[p][...].astype(BF16))[:N_HEADS]
        o_ref[:, page * p:page * (p + 1)] = jnp.sum(jnp.maximum(s, 0.0) * w, axis=0, keepdims=True)
    knew = knew_ref[...].astype(BF16).astype(F32)
    s_new = jnp.sum(q * knew, axis=1, keepdims=True)
    sc_new = jnp.sum(jnp.maximum(s_new, 0.0) * w, axis=0, keepdims=True)
    lane = lax.broadcasted_iota(I32, (1, LANES), 1)
    o_ref[:, page * n_pages:] = jnp.where(lane == 0, sc_new, 0.0)


def _sample_index_scores(layer, page_table, cache_k_idx, qi_s, wi_s, ki_s):
    n_db, n_pages = page_table.shape
    page = cache_k_idx.shape[2]
    l_pad = n_pages * page + LANES
    page_spec = lambda p: pl.BlockSpec((None, None, HEAD_DIM, page), lambda b, pt: (layer, pt[b, p], 0, 0))
    k_idx_t = jnp.swapaxes(cache_k_idx, 2, 3)
    grid_spec = pltpu.PrefetchScalarGridSpec(
        num_scalar_prefetch=1,
        grid=(n_db,),
        in_specs=[pl.BlockSpec((None, N_HEADS, HEAD_DIM), lambda b, pt: (b, 0, 0)),
                  pl.BlockSpec((None, N_HEADS, 1), lambda b, pt: (b, 0, 0)),
                  pl.BlockSpec((None, 1, HEAD_DIM), lambda b, pt: (b, 0, 0))]
                 + [page_spec(p) for p in range(n_pages)],
        out_specs=pl.BlockSpec((None, 1, l_pad), lambda b, pt: (b, 0, 0)),
    )
    out = pl.pallas_call(
        functools.partial(_sidx_kernel, n_pages=n_pages, page=page),
        out_shape=jax.ShapeDtypeStruct((n_db, 1, l_pad), F32),
        grid_spec=grid_spec,
        compiler_params=_params("arbitrary"),
        name="sample_index_scores",
    )(page_table, qi_s.astype(F32).reshape(n_db, N_HEADS, HEAD_DIM), wi_s.reshape(n_db, N_HEADS, 1),
      ki_s.reshape(n_db, 1, HEAD_DIM), *([k_idx_t] * n_pages))
    return out.reshape(n_db, l_pad)


def _ssel_kernel(s_ref, tri_ref, o_ref, score_scr, *, n_keys, ck, k_top):
    l_pad, tq = s_ref.shape
    krow = lax.broadcasted_iota(I32, (ck, tq), 0)

    def valid(c0):
        return (krow + c0) < n_keys

    def drop_padding(c, carry):
        c0 = pl.multiple_of(c * ck, ck)
        score_scr[pl.ds(c0, ck), :] = jnp.where(valid(c0), s_ref[pl.ds(c0, ck), :], -jnp.inf)
        return carry

    lax.fori_loop(0, l_pad // ck, drop_padding, 0)
    _selection_bias(score_scr, o_ref, tri_ref, l_pad // ck, 0, valid, ck=ck, tq=tq, k_top=k_top)


def _sample_selection(scores_t, tri, n_keys):
    l_pad, n_db = scores_t.shape
    return pl.pallas_call(
        functools.partial(_ssel_kernel, n_keys=n_keys, ck=LANES, k_top=_topk_count(n_keys)),
        out_shape=jax.ShapeDtypeStruct((l_pad, n_db), F32),
        grid=(1,),
        in_specs=[pl.BlockSpec((l_pad, n_db), lambda i: (0, 0)), pl.BlockSpec((LANES, LANES), lambda i: (0, 0))],
        out_specs=pl.BlockSpec((l_pad, n_db), lambda i: (0, 0)),
        scratch_shapes=[pltpu.VMEM((l_pad, n_db), F32)],
        compiler_params=_params("arbitrary"),
        name="sample_selection",
    )(scores_t, tri)


N_SUB = N_HEADS // 2


def _paged_attend(q_ref, knew_ref, vnew_ref, k_pages, v_pages, logit_terms, o_ref):
    n_pages, page = len(k_pages), k_pages[0].shape[-1]
    diag = _head_diag(N_HEADS)
    qs = jnp.where(_head_diag(BF16_ROWS), jnp.broadcast_to(q_ref[...], (BF16_ROWS, WIDTH)), 0.0)
    qb = qs.astype(BF16)
    s = jnp.concatenate([_dot(qb, k_pages[p][...].astype(BF16))[:N_HEADS] for p in range(n_pages)], axis=1)
    s_new = jnp.sum(qs[:N_HEADS] * knew_ref[...].astype(BF16).astype(F32), axis=1, keepdims=True)
    x, x_new = logit_terms(s, s_new)
    m = jnp.maximum(jnp.max(x, axis=1, keepdims=True), x_new)
    p = jnp.exp(x - m)
    p_new = jnp.exp(x_new - m)
    denom = jnp.sum(p, axis=1, keepdims=True) + p_new
    pb = jnp.concatenate([p, jnp.zeros_like(p)], axis=0).astype(BF16)
    acc = p_new.astype(BF16).astype(F32) * vnew_ref[...].astype(BF16).astype(F32)
    for q in range(n_pages):
        acc = acc + _nt_dot(pb[:, page * q:page * (q + 1)], v_pages[q][...].astype(BF16))[:N_HEADS]
    o_ref[...] = jnp.sum(jnp.where(diag, acc / denom, 0.0), axis=0, keepdims=True)


def _fox_sample_kernel(pt_ref, q_ref, qx_ref, k_ref, kx_ref, vt_ref,
                       qa_ref, qbs_ref, kan_ref, van_ref, kbn_ref, vbn_ref, lfn_ref, bias_ref, slope_ref, *rest,
                       tq, ck, n_pages, n_db):
    ka_p, va_p, kb_p, vb_p, lf_p = (rest[g * n_pages:(g + 1) * n_pages] for g in range(5))
    o_ref, oa_ref, ob_ref, qaug_scr, x_scr, p_scr, m_scr, acc_scr = rest[5 * n_pages:]
    blk, pair = pl.program_id(1), pl.program_id(2)

    q0 = blk * tq
    n_chunks = (q0 + tq + ck - 1) // ck
    lane = lax.broadcasted_iota(I32, (tq, LANES), 1)
    zeros = jnp.zeros((tq, LANES), BF16)
    for s in range(2):
        first = FX * (2 * pair + s)
        own = (lane >= first) & (lane < first + FX)
        qaug_scr[s] = jnp.concatenate([jnp.where(_head_pick(tq, s), q_ref[...], zeros),
                                       jnp.where(own, qx_ref[...], zeros)], axis=1)
    _flash_init(m_scr, acc_scr)
    krow = lax.broadcasted_iota(I32, (ck, tq), 0)
    qcol = lax.broadcasted_iota(I32, (ck, tq), 1)

    def chunk(c, masked):
        c0 = pl.multiple_of(c * ck, ck)
        k_aug = lambda s: jnp.concatenate([k_ref[pair, pl.ds(c0, ck), :], kx_ref[pl.ds(c0, ck), :]], axis=1)
        vt_of = lambda s: vt_ref[c, pl.ds(pl.multiple_of((2 * pair + s) * HEAD_DIM, HEAD_DIM), HEAD_DIM), :]
        visible = ((krow + c0) <= (qcol + q0)) if masked else None
        _flash_chunk(k_aug, qaug_scr, vt_of, x_scr, p_scr, m_scr, acc_scr, visible=visible)

    def full_chunk(c, carry):
        chunk(c, False)
        return carry

    lax.fori_loop(0, n_chunks - 1, full_chunk, 0)
    chunk(n_chunks - 1, True)
    _flash_output(o_ref, acc_scr)

    step = (pl.program_id(0) * pl.num_programs(1) + blk) * N_SUB + pair
    past = n_pages * ka_p[0].shape[-1]

    @pl.when((pair % 2 == 0) & (step < 2 * n_db))
    def _():
        def terms(s, s_new):
            rel = (lax.broadcasted_iota(I32, (1, past), 1) - past).astype(F32)
            bias = bias_ref[...]
            return s + slope_ref[...] * rel + bias[:, :past], s_new + bias[:, past:past + 1]
        _paged_attend(qa_ref, kan_ref, van_ref, ka_p, va_p, terms, oa_ref)

    @pl.when((pair % 2 == 1) & (step < 2 * n_db))
    def _():
        lf = jnp.concatenate([lf_p[p][...] for p in range(n_pages)], axis=1)
        idx = lax.broadcasted_iota(I32, lf.shape, 1)
        d = 1
        while d < past:
            lf = lf + jnp.where(idx >= d, pltpu.roll(lf, d, 1), 0.0)
            d *= 2
        f_new = lf[:, past - 1:past] + lfn_ref[...]
        _paged_attend(qbs_ref, kbn_ref, vbn_ref, kb_p, vb_p, lambda s, s_new: (s + (f_new - lf), s_new), ob_ref)


def _fox_and_sample(layer, qb, qx, kb_pairs, kx, vb_t, page_table, caches, logf_t, qa_s, qb_s, new_rows, logf_new,
                    bias, tq, ck):
    cache_k_a, cache_v_a, cache_k_b, cache_v_b = caches
    n_b, seq, _ = qb.shape
    nblk = seq // tq
    n_db, n_pages = page_table.shape
    n_pool, page = cache_k_a.shape[1], cache_k_a.shape[2]
    l_pad = bias.shape[1]
    assert 2 * n_db <= n_b * nblk * N_SUB, "the sample group needs two grid steps per sequence"
    flat = lambda c: jnp.transpose(c, (0, 1, 3, 4, 2)).reshape(c.shape[0], n_pool, WIDTH, page)

    step = lambda b, i, j: (b * nblk + i) * N_SUB + j
    seq_a = lambda b, i, j: jnp.minimum(step(b, i, j) // 2, n_db - 1)
    seq_b = lambda b, i, j: jnp.minimum(jnp.maximum(step(b, i, j) - 1, 0) // 2, n_db - 1)
    once = pl.Buffered(1)
    row = lambda width, sq: pl.BlockSpec((None, 1, width), lambda b, i, j, pt: (sq(b, i, j), 0, 0))
    kv_spec = lambda p, sq: pl.BlockSpec((None, None, WIDTH, page),
                                         lambda b, i, j, pt: (layer, pt[sq(b, i, j), p], 0, 0))
    lf_spec = lambda p: pl.BlockSpec((None, N_HEADS, page), lambda b, i, j, pt: (pt[seq_b(b, i, j), p], 0, 0))
    pages = range(n_pages)
    grid_spec = pltpu.PrefetchScalarGridSpec(
        num_scalar_prefetch=1,
        grid=(n_b, nblk, N_SUB),
        in_specs=[pl.BlockSpec((None, tq, LANES), lambda b, i, j, pt: (b, i, j)),
                  pl.BlockSpec((None, tq, LANES), lambda b, i, j, pt: (b, i, 0)),
                  pl.BlockSpec((None, N_SUB, seq, LANES), lambda b, i, j, pt: (b, 0, 0, 0), pipeline_mode=once),
                  pl.BlockSpec((None, seq, LANES), lambda b, i, j, pt: (b, 0, 0), pipeline_mode=once),
                  pl.BlockSpec((None, seq // ck, WIDTH, ck), lambda b, i, j, pt: (b, 0, 0, 0), pipeline_mode=once),
                  row(WIDTH, seq_a), row(WIDTH, seq_b), row(WIDTH, seq_a), row(WIDTH, seq_a),
                  row(WIDTH, seq_b), row(WIDTH, seq_b),
                  pl.BlockSpec((None, N_HEADS, 1), lambda b, i, j, pt: (seq_b(b, i, j), 0, 0)),
                  row(l_pad, seq_a), pl.BlockSpec((N_HEADS, 1), lambda b, i, j, pt: (0, 0))]
                 + [kv_spec(p, seq_a) for p in pages] * 2 + [kv_spec(p, seq_b) for p in pages] * 2
                 + [lf_spec(p) for p in pages],
        out_specs=(pl.BlockSpec((None, tq, LANES), lambda b, i, j, pt: (b, i, j)),
                   row(WIDTH, seq_a), row(WIDTH, seq_b)),
        scratch_shapes=_flash_scratch(2, tq, ck),
    )
    r3 = lambda a: a.reshape(n_db, 1, a.shape[-1])
    ka_new, va_new, kb_new, vb_new = new_rows
    slopes = jnp.asarray(ALIBI_SLOPES, F32).reshape(N_HEADS, 1)
    return pl.pallas_call(
        functools.partial(_fox_sample_kernel, tq=tq, ck=ck, n_pages=n_pages, n_db=n_db),
        out_shape=(jax.ShapeDtypeStruct((n_b, seq, WIDTH), BF16),
                   jax.ShapeDtypeStruct((n_db, 1, WIDTH), F32), jax.ShapeDtypeStruct((n_db, 1, WIDTH), F32)),
        grid_spec=grid_spec,
        compiler_params=_params("arbitrary", "arbitrary", "arbitrary"),
        name="fox_prompt_and_sample_attention",
    )(page_table, qb, qx, kb_pairs, kx, vb_t,
      r3(qa_s.astype(F32)), r3(qb_s.astype(F32)), r3(ka_new), r3(va_new), r3(kb_new), r3(vb_new),
      logf_new.reshape(n_db, N_HEADS, 1), r3(bias), slopes,
      *([flat(cache_k_a)] * n_pages), *([flat(cache_v_a)] * n_pages), *([flat(cache_k_b)] * n_pages),
      *([flat(cache_v_b)] * n_pages), *([logf_t] * n_pages))


def _post_kernel(x_ref, oa_ref, ob_ref, sga_ref, sgb_ref, gate1_ref, shift2_ref, scale2_ref, gate2_ref, g2_ref,
                 woa_ref, wob_ref, wout_ref, wup_ref, wdn_ref, y_ref, *, ff_chunk):
    merged = (sga_ref[...].astype(F32) * _dot(oa_ref[...], woa_ref[...])
              + sgb_ref[...].astype(F32) * _dot(ob_ref[...], wob_ref[...]))
    x1 = x_ref[...] + gate1_ref[...] * _dot(merged.astype(BF16), wout_ref[...])
    h2 = x1 * lax.rsqrt(jnp.mean(x1 * x1, axis=-1, keepdims=True) + EPS) * g2_ref[...]
    h2 = (h2 * (1.0 + scale2_ref[...]) + shift2_ref[...]).astype(BF16)
    mlp = jnp.zeros(x1.shape, F32)
    for c in range(0, D_FF, ff_chunk):
        u = jnp.maximum(_dot(h2, wup_ref[:, c:c + ff_chunk]), 0.0)
        mlp = mlp + _dot((u * u).astype(BF16), wdn_ref[c:c + ff_chunk, :])
    y_ref[...] = x1 + gate2_ref[...] * mlp


def _post(x, oa, ob, sga, sgb, mod, g2, woa, wob, wout, wup, wdn, tm):
    n_g, seq, _ = x.shape
    n_r = mod.shape[1]
    rb = 1 if n_r == 1 else tm
    row = lambda width: pl.BlockSpec((None, tm, width), lambda g, i: (g, i, 0))
    modspec = lambda j: pl.BlockSpec((None, rb, D_MODEL), lambda g, i: (g, i if n_r > 1 else 0, j))
    return pl.pallas_call(
        functools.partial(_post_kernel, ff_chunk=1024),
        out_shape=jax.ShapeDtypeStruct((n_g, seq, D_MODEL), F32),
        grid=(n_g, seq // tm),
        in_specs=[row(D_MODEL), row(WIDTH), row(WIDTH), row(D_MODEL), row(D_MODEL),
                  modspec(2), modspec(3), modspec(4), modspec(5), _const_spec((1, D_MODEL)),
                  _const_spec((WIDTH, D_MODEL)), _const_spec((WIDTH, D_MODEL)), _const_spec((D_MODEL, D_MODEL)),
                  _const_spec((D_MODEL, D_FF)), _const_spec((D_FF, D_MODEL))],
        out_specs=row(D_MODEL),
        compiler_params=_params("arbitrary", "arbitrary"),
        name="merge_out_mlp",
    )(x, oa, ob, sga, sgb, mod, mod, mod, mod, g2, woa, wob, wout, wup, wdn)


def _lower_tri(n):
    return jnp.asarray(np.tril(np.ones((n, n), np.float32)), BF16)


def _layer(l, x_p, x_s, caches, cache_k_idx, logf_t, page_table, mod_p, mod_s, lw, tiles):
    (g_norm1, w_r, gains, bsm, gmat, woa, wob, wout, g_norm2, wup, wdn) = lw
    tm, tq = tiles
    n_b, seq, _ = x_p.shape
    n_db = x_s.shape[0]

    (ka_p, va_p, kb_p, vb_p, small_p, qa_bf, ka_bf, vat_bf, qb_bf, kb_pairs, vbt_bf, qi_bf, ki2_bf, smallt_p,
     sga_p, sgb_p) = _in_proj(x_p, mod_p, g_norm1, w_r, gains, bsm, gmat, tm)
    xs = x_s.reshape(1, n_db, D_MODEL)
    (ka_s, va_s, kb_s, vb_s, small_s, qa_s, _, _, qb_s, _, _, qi_s, _, _, sga_s, sgb_s) = _in_proj(
        xs, mod_s, g_norm1, w_r, gains, bsm, gmat, n_db)
    small_s2 = small_s[0]
    ki_s, wi_s = small_s2[:, :SM_WI], small_s2[:, SM_WI:SM_LOGF]
    logf_s = small_s2[:, SM_LOGF:SM_LOGF + N_HEADS]

    n_keys = page_table.shape[1] * cache_k_idx.shape[2] + 1
    scores = _sample_index_scores(l, page_table, cache_k_idx, qi_s[0], wi_s, ki_s)
    bias_t = _sample_selection(scores.T, _lower_tri(LANES), n_keys)

    kx_b, qx_b = _forget_operands(small_p, _lower_tri(tm), tm)
    ob_p, oa_s, ob_s = _fox_and_sample(l, qb_bf, qx_b, kb_pairs, kx_b, vbt_bf, page_table, caches, logf_t,
                                       qa_s[0], qb_s[0], (ka_s[0], va_s[0], kb_s[0], vb_s[0]), logf_s, bias_t.T,
                                       tq, tm)
    oa_p = _dsa_prompt(qi_bf, ki2_bf, smallt_p, qa_bf, ka_bf, vat_bf, _lower_tri(tm), tq, tm)

    y_p = _post(x_p, oa_p, ob_p, sga_p, sgb_p, mod_p, g_norm2, woa, wob, wout, wup, wdn, tm)
    y_s = _post(xs, oa_s.reshape(1, n_db, WIDTH).astype(BF16), ob_s.reshape(1, n_db, WIDTH).astype(BF16),
                sga_s, sgb_s, mod_s, g_norm2, woa, wob, wout, wup, wdn, n_db)

    heads = lambda a, lead: a.reshape(*lead, N_HEADS, HEAD_DIM)
    lp, ls = (n_b, seq), (n_db, 1)
    state_p = (heads(ka_p, lp), heads(va_p, lp), small_p[..., :SM_WI], heads(kb_p, lp), heads(vb_p, lp),
               small_p[..., SM_LOGF:SM_LOGF + N_HEADS])
    state_s = (heads(ka_s[0], ls), heads(va_s[0], ls), ki_s.reshape(n_db, 1, HEAD_DIM), heads(kb_s[0], ls),
               heads(vb_s[0], ls), logf_s.reshape(n_db, 1, N_HEADS))
    return y_p, y_s.reshape(n_db, 1, D_MODEL), state_p, state_s


def kernel(x_prompt, x_sample, cache_k_a, cache_v_a, cache_k_idx, cache_k_b, cache_v_b, cache_logf_b, page_table,
           c_prompt, c_sample, w_ada, b_ada, g_norm1, w_in, b_forget, g_qa, g_ka, g_qb, g_kb, w_o_a, w_o_b, w_out,
           g_norm2, w_up, w_down):
    depth = w_in.shape[0]
    n_b, seq, _ = x_prompt.shape
    n_db, dec_seq, _ = x_sample.shape
    assert dec_seq == 1 and x_prompt.shape[-1] == D_MODEL
    assert cache_k_a.shape[3:] == (N_HEADS, HEAD_DIM) and cache_k_idx.shape[3] == HEAD_DIM
    tiles = (min(512, seq), min(256, seq))

    c_all = jnp.concatenate([c_prompt, c_sample], axis=0)
    pad = (-c_all.shape[0]) % 8
    c_all = jnp.pad(c_all, ((0, pad), (0, 0)))
    gmat = jnp.asarray(np.kron(np.eye(N_HEADS, dtype=np.float32),
                               np.full((HEAD_DIM, HEAD_DIM), 1.0 / HEAD_DIM, np.float32)), BF16)
    split = 7 * WIDTH
    y_p, y_s = x_prompt, x_sample.reshape(n_db, D_MODEL)
    states_p, states_s = [], []
    for l in range(depth):
        mod = _ada_mod(c_all, w_ada[l], b_ada[l])
        mod_p = mod[:n_b].reshape(n_b, 1, 6 * D_MODEL)
        mod_s = mod[n_b:n_b + n_db].reshape(1, n_db, 6 * D_MODEL)
        w = w_in[l]
        k_idx_cols = w[:, split:split + HEAD_DIM]
        w_r = jnp.concatenate(
            [w[:, :split], w[:, split:split + HEAD_DIM + 2 * N_HEADS],
             jnp.zeros((D_MODEL, LANES - HEAD_DIM - 2 * N_HEADS), w.dtype),
             k_idx_cols, k_idx_cols, w[:, split + HEAD_DIM + 2 * N_HEADS:]], axis=1).astype(BF16)
        gains = jnp.stack([jnp.tile(g[l], N_HEADS) for g in (g_qa, g_ka, g_qb, g_kb)])
        bsm = jnp.zeros((1, LANES), F32).at[0, SM_LOGF:SM_LOGF + N_HEADS].set(b_forget[l])
        lw = (g_norm1[l].reshape(1, D_MODEL), w_r, gains, bsm, gmat, w_o_a[l].astype(BF16), w_o_b[l].astype(BF16),
              w_out[l].astype(BF16), g_norm2[l].reshape(1, D_MODEL), w_up[l].astype(BF16), w_down[l].astype(BF16))
        logf_t = jnp.swapaxes(cache_logf_b[l], 1, 2)
        y_p, y_s, st_p, st_s = _layer(l, y_p, y_s, (cache_k_a, cache_v_a, cache_k_b, cache_v_b), cache_k_idx,
                                      logf_t, page_table, mod_p, mod_s, lw, tiles)
        y_s = y_s.reshape(n_db, D_MODEL)
        states_p.append(st_p)
        states_s.append(st_s)
    k_a_p, v_a_p, k_idx_p, k_b_p, v_b_p, logf_p = (jnp.stack(s) for s in zip(*states_p))
    k_a_s, v_a_s, k_idx_s, k_b_s, v_b_s, logf_s = (jnp.stack(s) for s in zip(*states_s))
    return (y_p, y_s.reshape(n_db, 1, D_MODEL), k_a_p, v_a_p, k_idx_p, k_b_p, v_b_p, logf_p,
            k_a_s, v_a_s, k_idx_s, k_b_s, v_b_s, logf_s)
```

```python
import functools

import numpy as np
import jax
import jax.numpy as jnp
from jax import lax
from jax.experimental import pallas as pl
from jax.experimental.pallas import tpu as pltpu

F32 = jnp.float32
BF16 = jnp.bfloat16
I32 = jnp.int32

D_MODEL = 1024
HEAD_DIM = 64
N_HEADS = 8
WIDTH = N_HEADS * HEAD_DIM
D_FF = 4 * D_MODEL
TOPK_MAX = 256
EPS = 1e-6
MASK_VALUE = -1e30
LANES = 128
BF16_ROWS = 16
QK_SCALE = HEAD_DIM ** -0.5
WI_SCALE = N_HEADS ** -0.5
ALIBI_SLOPES = tuple(2.0 ** (-8.0 * (h + 1) / N_HEADS) for h in range(N_HEADS))
INT_MIN = -(2 ** 31)
VMEM_LIMIT = 56 * 1024 * 1024

C_QA, C_KA, C_VA, C_QB, C_KB, C_VB, C_QI = (i * WIDTH for i in range(7))
C_SMALL = 7 * WIDTH
C_KI2 = C_SMALL + LANES
C_GA = C_KI2 + LANES
C_GB = C_GA + D_MODEL
C_TOTAL = C_GB + D_MODEL
SM_WI = HEAD_DIM
SM_LOGF = HEAD_DIM + N_HEADS


def _topk_count(n_keys):
    return max(1, min(TOPK_MAX, n_keys // 4))


def _const_spec(shape):
    zeros = (0,) * len(shape)
    return pl.BlockSpec(shape, lambda *_: zeros, pipeline_mode=pl.Buffered(1))


def _params(*sem):
    return pltpu.CompilerParams(dimension_semantics=sem, vmem_limit_bytes=VMEM_LIMIT)


def _nt_dot(a, b):
    return lax.dot_general(a, b, (((1,), (1,)), ((), ())), preferred_element_type=F32)


def _dot(a, b):
    return jnp.dot(a, b, preferred_element_type=F32)


def _split3(x):
    p1 = x.astype(BF16)
    r1 = x - p1.astype(F32)
    p2 = r1.astype(BF16)
    p3 = (r1 - p2.astype(F32)).astype(BF16)
    return p1, p2, p3


def _ada_kernel(c_ref, w_ref, b_ref, o_ref):
    c = c_ref[...]
    s = c * jax.nn.sigmoid(c)
    o_ref[...] = _dot(s.astype(BF16), w_ref[...].astype(BF16)) + b_ref[...]


def _ada_mod(c, w_ada, b_ada):
    rows = c.shape[0]
    n_out = w_ada.shape[1]
    tn = D_MODEL
    return pl.pallas_call(
        _ada_kernel,
        out_shape=jax.ShapeDtypeStruct((rows, n_out), F32),
        grid=(n_out // tn,),
        in_specs=[pl.BlockSpec((rows, D_MODEL), lambda j: (0, 0)),
                  pl.BlockSpec((D_MODEL, tn), lambda j: (0, j)),
                  pl.BlockSpec((1, tn), lambda j: (0, j))],
        out_specs=pl.BlockSpec((rows, tn), lambda j: (0, j)),
        compiler_params=_params("arbitrary"),
        name="ada_mod",
    )(c, w_ada, b_ada.reshape(1, n_out))


def _log_sigmoid(x):
    return jnp.minimum(x, 0.0) - jnp.log(1.0 + jnp.exp(-jnp.abs(x)))


def _in_proj_kernel(x_ref, shift_ref, scale_ref, g1_ref, w_ref, gains_ref, bsm_ref, gmat_ref,
                    ka_ref, va_ref, kb_ref, vb_ref, small_ref,
                    qa_bf, ka_bf, vat_bf, qb_bf, kb_bf, vbt_bf, qi_bf, ki2_bf, smallt_ref, sga_ref, sgb_ref):
    x = x_ref[...]
    h = x * lax.rsqrt(jnp.mean(x * x, axis=-1, keepdims=True) + EPS) * g1_ref[...]
    h = h * (1.0 + scale_ref[...]) + shift_ref[...]
    hb = h.astype(BF16)

    def seg(col, width=WIDTH):
        return _dot(hb, w_ref[:, col:col + width])

    def head_norm(z, gi):
        msq = _dot((z * z).astype(BF16), gmat_ref[...])
        return z * lax.rsqrt(msq + EPS) * gains_ref[gi:gi + 1, :]

    qa = head_norm(seg(C_QA), 0)
    qa_bf[...] = (qa * QK_SCALE).astype(BF16)
    ka = head_norm(seg(C_KA), 1)
    ka_ref[...] = ka
    ka_bf[...] = ka.astype(BF16)
    va = seg(C_VA)
    va_ref[...] = va
    vat_bf[...] = va.T.astype(BF16)
    qb = head_norm(seg(C_QB), 2)
    qb_bf[...] = (qb * QK_SCALE).astype(BF16)
    kb = head_norm(seg(C_KB), 3)
    kb_ref[...] = kb
    kb_bf[...] = kb.astype(BF16)
    vb = seg(C_VB)
    vb_ref[...] = vb
    vbt_bf[...] = vb.T.astype(BF16)
    qi_bf[...] = seg(C_QI).astype(BF16)
    ki2_bf[...] = seg(C_KI2, LANES).astype(BF16)

    zs = seg(C_SMALL, LANES)
    lane = lax.broadcasted_iota(I32, zs.shape, 1)
    logf = _log_sigmoid(zs + bsm_ref[...])
    small = jnp.where(lane < SM_WI, zs,
                      jnp.where(lane < SM_LOGF, zs * WI_SCALE,
                                jnp.where(lane < SM_LOGF + N_HEADS, logf, 0.0)))
    small_ref[...] = small
    smallt_ref[...] = small.T

    sga_ref[...] = jax.nn.sigmoid(seg(C_GA, D_MODEL)).astype(BF16)
    sgb_ref[...] = jax.nn.sigmoid(seg(C_GB, D_MODEL)).astype(BF16)


def _in_proj(x, mod, g1, w_r, gains, bsm, gmat, tm):
    n_g, seq, _ = x.shape
    n_r = mod.shape[1]
    rb = 1 if n_r == 1 else tm
    nblk = seq // tm
    row = lambda width: pl.BlockSpec((None, tm, width), lambda g, i: (g, i, 0))
    modspec = lambda j: pl.BlockSpec((None, rb, D_MODEL), lambda g, i: (g, i if n_r > 1 else 0, j))
    sds = lambda shape, dt: jax.ShapeDtypeStruct(shape, dt)
    out_shape = (
        sds((n_g, seq, WIDTH), F32), sds((n_g, seq, WIDTH), F32), sds((n_g, seq, WIDTH), F32),
        sds((n_g, seq, WIDTH), F32), sds((n_g, seq, LANES), F32),
        sds((n_g, seq, WIDTH), BF16), sds((n_g, seq, WIDTH), BF16), sds((n_g, nblk, WIDTH, tm), BF16),
        sds((n_g, seq, WIDTH), BF16), sds((n_g, seq, WIDTH), BF16), sds((n_g, nblk, WIDTH, tm), BF16),
        sds((n_g, seq, WIDTH), BF16), sds((n_g, seq, LANES), BF16), sds((n_g, LANES, seq), F32),
        sds((n_g, seq, D_MODEL), BF16), sds((n_g, seq, D_MODEL), BF16),
    )
    chunk_t = pl.BlockSpec((None, None, WIDTH, tm), lambda g, i: (g, i, 0, 0))
    out_specs = (
        row(WIDTH), row(WIDTH), row(WIDTH), row(WIDTH), row(LANES),
        row(WIDTH), row(WIDTH), chunk_t,
        row(WIDTH), row(WIDTH), chunk_t,
        row(WIDTH), row(LANES), pl.BlockSpec((None, LANES, tm), lambda g, i: (g, 0, i)),
        row(D_MODEL), row(D_MODEL),
    )
    return pl.pallas_call(
        _in_proj_kernel,
        out_shape=out_shape,
        grid=(n_g, nblk),
        in_specs=[row(D_MODEL), modspec(0), modspec(1), _const_spec((1, D_MODEL)),
                  _const_spec((D_MODEL, C_TOTAL)), _const_spec((4, WIDTH)), _const_spec((1, LANES)),
                  _const_spec((WIDTH, WIDTH))],
        out_specs=out_specs,
        compiler_params=_params("arbitrary", "arbitrary"),
        name="in_proj",
    )(x, mod, mod, g1, w_r, gains, bsm, gmat)


FX = 6


def _forget_placement():
    place = np.zeros((3, LANES, 2 * LANES), np.float32)
    const = np.zeros((1, 2 * LANES), np.float32)
    for h in range(N_HEADS):
        for i in range(3):
            place[i, SM_LOGF + h, FX * h + i] = 1.0
            place[i, SM_LOGF + h, LANES + FX * h + 3 + i] = 1.0
            const[0, FX * h + 3 + i] = 1.0
            const[0, LANES + FX * h + i] = -1.0
    return jnp.asarray(place, BF16), jnp.asarray(const, F32)


def _cumsum_kernel(x_ref, tri_ref, place_ref, const_ref, kx_ref, qx_ref, carry_ref):
    @pl.when(pl.program_id(1) == 0)
    def _():
        carry_ref[...] = jnp.zeros_like(carry_ref)

    tri = tri_ref[...]
    p1, p2, p3 = _split3(x_ref[...])
    cs = _dot(tri, p1) + _dot(tri, p2) + _dot(tri, p3) + carry_ref[...]
    n = cs.shape[0]
    carry_ref[...] = cs[n - 1:n, :]
    f1, f2, f3 = _split3(cs)
    ext = _dot(f1, place_ref[0]) + _dot(f2, place_ref[1]) + _dot(f3, place_ref[2]) + const_ref[...]
    kx_ref[...] = ext[:, :LANES].astype(BF16)
    qx_ref[...] = ext[:, LANES:].astype(BF16)


def _forget_operands(small, tri, cb):
    n_b, seq, _ = small.shape
    place, const = _forget_placement()
    blk = pl.BlockSpec((None, cb, LANES), lambda b, i: (b, i, 0))
    return pl.pallas_call(
        _cumsum_kernel,
        out_shape=(jax.ShapeDtypeStruct((n_b, seq, LANES), BF16), jax.ShapeDtypeStruct((n_b, seq, LANES), BF16)),
        grid=(n_b, seq // cb),
        in_specs=[blk, _const_spec((cb, cb)), _const_spec((3, LANES, 2 * LANES)), _const_spec((1, 2 * LANES))],
        out_specs=(blk, blk),
        scratch_shapes=[pltpu.VMEM((1, LANES), F32)],
        compiler_params=_params("arbitrary", "arbitrary"),
        name="cumsum_logf",
    )(small, tri, place, const)


def _flash_chunk(k_aug, qaug_scr, vt_of, x_scr, p_scr, m_scr, acc_scr, bias=None, visible=None):
    n_slots, ck = x_scr.shape[0], x_scr.shape[1]
    ones = jnp.ones((BF16_ROWS, ck), BF16)
    alphas = {}

    def logits(s):
        x = _nt_dot(k_aug(s), qaug_scr[s])
        if bias is not None:
            x = x + bias
        if visible is not None:
            x = jnp.where(visible, x, MASK_VALUE)
        x_scr[s] = x
        m_old = m_scr[s]
        m_new = jnp.maximum(m_old, jnp.max(x, axis=0, keepdims=True))
        alphas[s] = jnp.exp(m_old - m_new)
        m_scr[s] = m_new

    def probabilities(s):
        p_scr[s] = jnp.exp(x_scr[s] - m_scr[s]).astype(BF16)

    def values(s):
        acc_scr[s] = alphas[s] * acc_scr[s] + _dot(jnp.concatenate([vt_of(s), ones], axis=0), p_scr[s])

    for phase in (logits, probabilities, values):
        for s in range(n_slots):
            phase(s)


def _flash_scratch(n_slots, tq, ck):
    return [pltpu.VMEM((n_slots, tq, 2 * LANES), BF16), pltpu.VMEM((n_slots, ck, tq), F32),
            pltpu.VMEM((n_slots, ck, tq), BF16), pltpu.VMEM((n_slots, 1, tq), F32),
            pltpu.VMEM((n_slots, HEAD_DIM + BF16_ROWS, tq), F32)]


def _flash_init(m_scr, acc_scr):
    m_scr[...] = jnp.full(m_scr.shape, MASK_VALUE, F32)
    acc_scr[...] = jnp.zeros_like(acc_scr)


def _flash_output(o_ref, acc_scr):
    out_t = jnp.concatenate([acc_scr[s, :HEAD_DIM] / acc_scr[s, HEAD_DIM:HEAD_DIM + 1]
                             for s in range(acc_scr.shape[0])], axis=0)
    o_ref[...] = out_t.T.astype(BF16)


def _head_pick(tq, h):
    lane = lax.broadcasted_iota(I32, (tq, LANES), 1)
    return (lane >= HEAD_DIM) == (h % 2 == 1)


def _pair_lanes(h):
    return slice(LANES * (h // 2), LANES * (h // 2 + 1))


def _ordered_float(u):
    key = u ^ jnp.int32(INT_MIN)
    return pltpu.bitcast(key ^ ((key >> 31) & jnp.int32(0x7FFFFFFF)), F32)


BISECT_FIRST = 20
BISECT_MORE = 3
BISECT_STEPS = 29


def _selection_bias(score_scr, bias_scr, tri_ref, n_chunks, n_beyond, valid_fn, *, ck, tq, k_top, try_bisect=None):
    n_acc = 4

    def partials(blk, reduce):
        return reduce(blk.reshape(ck // (8 * n_acc), n_acc, 8, tq), axis=0)

    def finish(acc, reduce):
        return reduce(reduce(acc, axis=0), axis=0, keepdims=True)

    def count(cmp):
        def body(c, acc):
            c0 = pl.multiple_of(c * ck, ck)
            return acc + partials(jnp.where(cmp(score_scr[pl.ds(c0, ck), :]), 1, 0), jnp.sum)
        return finish(lax.fori_loop(0, n_chunks, body, jnp.zeros((n_acc, 8, tq), I32)), jnp.sum)

    def reaching(x):
        return count(lambda blk: blk >= x) + jnp.where(x <= MASK_VALUE, n_beyond, 0)

    def exact_cut(thr):
        def body(c, carry):
            c0 = pl.multiple_of(c * ck, ck)
            keep = jnp.where(score_scr[pl.ds(c0, ck), :] >= thr, 0.0, MASK_VALUE)
            bias_scr[pl.ds(c0, ck), :] = jnp.where(valid_fn(c0), keep, MASK_VALUE)
            return carry
        lax.fori_loop(0, n_chunks, body, 0)

    def ranked_cut(thr):
        n_greater = count(lambda blk: blk > thr) + jnp.where(thr < MASK_VALUE, n_beyond, 0)
        ties_wanted = (k_top - n_greater).astype(F32)

        def body(c, ties_before):
            c0 = pl.multiple_of(c * ck, ck)
            blk = score_scr[pl.ds(c0, ck), :]
            tie = blk == thr
            rank = _dot(tri_ref[...], jnp.where(tie, 1.0, 0.0).astype(BF16)) + ties_before
            keep = jnp.where(blk > thr, 0.0,
                             jnp.where(tie, jnp.where(rank <= ties_wanted, 0.0, MASK_VALUE), MASK_VALUE))
            bias_scr[pl.ds(c0, ck), :] = jnp.where(valid_fn(c0), keep, MASK_VALUE)
            return rank[ck - 1:ck, :]
        lax.fori_loop(0, n_chunks, body, jnp.zeros((1, tq), F32))

    def bitwise():
        def bit_step(i, carry):
            code, reached = carry
            cand_code = code | lax.shift_left(jnp.int32(1), 31 - i)
            cnt = reaching(_ordered_float(cand_code))
            take = cnt >= k_top
            return jnp.where(take, cand_code, code), jnp.where(take, cnt, reached)

        code, reached = lax.fori_loop(0, 32, bit_step, (jnp.zeros((1, tq), I32), jnp.full((1, tq), k_top, I32)))
        thr = _ordered_float(code)
        lax.cond(jnp.max(reached) > k_top, lambda: ranked_cut(thr), lambda: exact_cut(thr))

    if try_bisect is None:
        bitwise()
        return

    def unsettled(reached):
        return jnp.max(jnp.abs(reached - k_top)) > 0

    def bisect():
        def bounds(c, carry):
            lo, hi = carry
            c0 = pl.multiple_of(c * ck, ck)
            blk = score_scr[pl.ds(c0, ck), :]
            lo = jnp.minimum(lo, partials(jnp.where(valid_fn(c0), blk, jnp.inf), jnp.min))
            return lo, jnp.maximum(hi, partials(blk, jnp.max))
        lo, hi = lax.fori_loop(0, n_chunks, bounds, (jnp.full((n_acc, 8, tq), jnp.inf, F32),
                                                      jnp.full((n_acc, 8, tq), -jnp.inf, F32)))

        def halve(_, state):
            lo, hi, reached = state
            mid = lo + 0.5 * (hi - lo)
            cnt = reaching(mid)
            up = cnt >= k_top
            return jnp.where(up, mid, lo), jnp.where(up, hi, mid), jnp.where(up, cnt, reached)

        state = (finish(lo, jnp.min), finish(hi, jnp.max), jnp.full((1, tq), k_top + 1, I32))
        state = lax.fori_loop(0, BISECT_FIRST, halve, state)
        _, (lo, _, reached) = lax.while_loop(
            lambda st: (st[0] < BISECT_STEPS) & unsettled(st[1][2]),
            lambda st: (st[0] + BISECT_MORE, lax.fori_loop(0, BISECT_MORE, halve, st[1])),
            (jnp.int32(BISECT_FIRST), state))
        return lo, jnp.logical_not(unsettled(reached))

    thr, found = lax.cond(try_bisect, bisect, lambda: (jnp.zeros((1, tq), F32), jnp.bool_(False)))
    lax.cond(found, lambda: exact_cut(thr), bitwise)


POS_SPLIT = 256


def _alibi_lanes(pos, slope=None):
    lane = lax.broadcasted_iota(I32, pos.shape, 1)
    lo = pos & (POS_SPLIT - 1)
    hi_f, lo_f = (pos - lo).astype(F32), lo.astype(F32)
    if slope is None:
        val = jnp.where(lane == 0, hi_f, jnp.where(lane == 1, lo_f, jnp.where(lane < 4, 1.0, 0.0)))
    else:
        val = jnp.where(lane < 2, slope, jnp.where(lane == 2, hi_f * -slope, jnp.where(lane == 3, lo_f * -slope, 0.0)))
    return val.astype(BF16)


def _dsa_kernel(qi_ref, ki2_ref, w_ref, qa_ref, ka_ref, vat_ref, tri_ref, o_ref,
                qm_scr, score_scr, bias_scr, kx_scr, qaug_scr, x_scr, p_scr, m_scr, acc_scr, *, tq, ck, k_top, seq):
    q0 = pl.program_id(1) * tq
    n_chunks = (q0 + tq + ck - 1) // ck
    n_beyond = seq - n_chunks * ck

    zeros = jnp.zeros((tq, LANES), BF16)
    qpos = lax.broadcasted_iota(I32, (tq, LANES), 0) + q0
    for h in range(N_HEADS):
        pick = _head_pick(tq, h)
        qm_scr[h] = jnp.where(pick, qi_ref[:, _pair_lanes(h)], zeros)
        qaug_scr[h] = jnp.concatenate([jnp.where(pick, qa_ref[:, _pair_lanes(h)], zeros),
                                       _alibi_lanes(qpos, ALIBI_SLOPES[h])], axis=1)

    w = w_ref[...] * QK_SCALE
    krow = lax.broadcasted_iota(I32, (ck, tq), 0)
    qcol = lax.broadcasted_iota(I32, (ck, tq), 1)

    def visible(c0):
        return (krow + c0) <= (qcol + q0)

    def score_chunk(c, carry):
        c0 = pl.multiple_of(c * ck, ck)
        kblk = ki2_ref[pl.ds(c0, ck), :]
        acc = jnp.zeros((ck, tq), F32)
        for h in range(N_HEADS):
            acc = acc + jnp.maximum(_nt_dot(kblk, qm_scr[h]), 0.0) * w[h:h + 1, :]
        score_scr[pl.ds(c0, ck), :] = jnp.where(visible(c0), acc, MASK_VALUE)
        return carry

    lax.fori_loop(0, n_chunks, score_chunk, 0)

    _selection_bias(score_scr, bias_scr, tri_ref, n_chunks, n_beyond, visible, ck=ck, tq=tq, k_top=k_top)

    _flash_init(m_scr, acc_scr)
    kpos0 = lax.broadcasted_iota(I32, (ck, LANES), 0)

    def attend_chunk(c, carry):
        c0 = pl.multiple_of(c * ck, ck)
        kx_scr[...] = _alibi_lanes(kpos0 + c0)
        k_aug = lambda h: jnp.concatenate([ka_ref[pl.ds(c0, ck), _pair_lanes(h)], kx_scr[...]], axis=1)
        _flash_chunk(k_aug, qaug_scr, lambda h: vat_ref[c, HEAD_DIM * h:HEAD_DIM * (h + 1), :],
                     x_scr, p_scr, m_scr, acc_scr, bias=bias_scr[pl.ds(c0, ck), :])
        return carry

    lax.fori_loop(0, n_chunks, attend_chunk, 0)
    _flash_output(o_ref, acc_scr)


def _dsa_prompt(qi, ki2, small_t, qa, ka, va_t, tri, tq, ck):
    n_b, seq, _ = qi.shape
    nblk = seq // tq
    kern = functools.partial(_dsa_kernel, tq=tq, ck=ck, k_top=_topk_count(seq), seq=seq)
    return pl.pallas_call(
        kern,
        out_shape=jax.ShapeDtypeStruct((n_b, seq, WIDTH), BF16),
        grid=(n_b, nblk),
        in_specs=[pl.BlockSpec((None, tq, WIDTH), lambda b, i: (b, i, 0)),
                  pl.BlockSpec((None, seq, LANES), lambda b, i: (b, 0, 0)),
                  pl.BlockSpec((None, N_HEADS, tq), lambda b, i: (b, SM_WI // N_HEADS, i)),
                  pl.BlockSpec((None, tq, WIDTH), lambda b, i: (b, i, 0)),
                  pl.BlockSpec((None, seq, WIDTH), lambda b, i: (b, 0, 0)),
                  pl.BlockSpec((None, seq // ck, WIDTH, ck), lambda b, i: (b, 0, 0, 0)),
                  _const_spec((ck, ck))],
        out_specs=pl.BlockSpec((None, tq, WIDTH), lambda b, i: (b, i, 0)),
        scratch_shapes=[pltpu.VMEM((N_HEADS, tq, LANES), BF16), pltpu.VMEM((seq, tq), F32),
                        pltpu.VMEM((seq, tq), F32), pltpu.VMEM((ck, LANES), BF16)]
                       + _flash_scratch(N_HEADS, tq, ck),
        compiler_params=_params("arbitrary", "arbitrary"),
        name="dsa_prompt",
    )(qi, ki2, small_t, qa, ka, va_t, tri)


def _head_diag(n_rows):
    lane = lax.broadcasted_iota(I32, (n_rows, WIDTH), 1)
    sub = lax.broadcasted_iota(I32, (n_rows, WIDTH), 0)
    return (lane >= sub * HEAD_DIM) & (lane < (sub + 1) * HEAD_DIM)


def _sidx_kernel(pt_ref, q_ref, w_ref, knew_ref, *rest, n_pages, page):
    page_refs, o_ref = rest[:n_pages], rest[n_pages]
    q = q_ref[...]
    qp = jnp.concatenate([q, jnp.zeros((BF16_ROWS - N_HEADS, HEAD_DIM), F32)], axis=0).astype(BF16)
    w = w_ref[...] * QK_SCALE
    for p in range(n_pages):
        s = _dot(qp, page_refs[p][...].astype(BF16))[:N_HEADS]
        o_ref[:, page * p:page * (p + 1)] = jnp.sum(jnp.maximum(s, 0.0) * w, axis=0, keepdims=True)
    knew = knew_ref[...].astype(BF16).astype(F32)
    s_new = jnp.sum(q * knew, axis=1, keepdims=True)
    sc_new = jnp.sum(jnp.maximum(s_new, 0.0) * w, axis=0, keepdims=True)
    lane = lax.broadcasted_iota(I32, (1, LANES), 1)
    o_ref[:, page * n_pages:] = jnp.where(lane == 0, sc_new, 0.0)


def _sample_index_scores(layer, page_table, cache_k_idx, qi_s, wi_s, ki_s):
    n_db, n_pages = page_table.shape
    page = cache_k_idx.shape[2]
    l_pad = n_pages * page + LANES
    page_spec = lambda p: pl.BlockSpec((None, None, HEAD_DIM, page), lambda b, pt: (layer, pt[b, p], 0, 0))
    k_idx_t = jnp.swapaxes(cache_k_idx, 2, 3)
    grid_spec = pltpu.PrefetchScalarGridSpec(
        num_scalar_prefetch=1,
        grid=(n_db,),
        in_specs=[pl.BlockSpec((None, N_HEADS, HEAD_DIM), lambda b, pt: (b, 0, 0)),
                  pl.BlockSpec((None, N_HEADS, 1), lambda b, pt: (b, 0, 0)),
                  pl.BlockSpec((None, 1, HEAD_DIM), lambda b, pt: (b, 0, 0))]
                 + [page_spec(p) for p in range(n_pages)],
        out_specs=pl.BlockSpec((None, 1, l_pad), lambda b, pt: (b, 0, 0)),
    )
    out = pl.pallas_call(
        functools.partial(_sidx_kernel, n_pages=n_pages, page=page),
        out_shape=jax.ShapeDtypeStruct((n_db, 1, l_pad), F32),
        grid_spec=grid_spec,
        compiler_params=_params("arbitrary"),
        name="sample_index_scores",
    )(page_table, qi_s.astype(F32).reshape(n_db, N_HEADS, HEAD_DIM), wi_s.reshape(n_db, N_HEADS, 1),
      ki_s.reshape(n_db, 1, HEAD_DIM), *([k_idx_t] * n_pages))
    return out.reshape(n_db, l_pad)


def _ssel_kernel(s_ref, tri_ref, o_ref, score_scr, *, n_keys, ck, k_top):
    l_pad, tq = s_ref.shape
    krow = lax.broadcasted_iota(I32, (ck, tq), 0)

    def valid(c0):
        return (krow + c0) < n_keys

    def drop_padding(c, carry):
        c0 = pl.multiple_of(c * ck, ck)
        score_scr[pl.ds(c0, ck), :] = jnp.where(valid(c0), s_ref[pl.ds(c0, ck), :], -jnp.inf)
        return carry

    lax.fori_loop(0, l_pad // ck, drop_padding, 0)
    _selection_bias(score_scr, o_ref, tri_ref, l_pad // ck, 0, valid, ck=ck, tq=tq, k_top=k_top)


def _sample_selection(scores_t, tri, n_keys):
    l_pad, n_db = scores_t.shape
    return pl.pallas_call(
        functools.partial(_ssel_kernel, n_keys=n_keys, ck=LANES, k_top=_topk_count(n_keys)),
        out_shape=jax.ShapeDtypeStruct((l_pad, n_db), F32),
        grid=(1,),
        in_specs=[pl.BlockSpec((l_pad, n_db), lambda i: (0, 0)), pl.BlockSpec((LANES, LANES), lambda i: (0, 0))],
        out_specs=pl.BlockSpec((l_pad, n_db), lambda i: (0, 0)),
        scratch_shapes=[pltpu.VMEM((l_pad, n_db), F32)],
        compiler_params=_params("arbitrary"),
        name="sample_selection",
    )(scores_t, tri)


def _paged_attend(q_ref, knew_ref, vnew_ref, k_pages, v_pages, logit_terms, o_ref):
    n_pages, page = len(k_pages), k_pages[0].shape[-1]
    diag = _head_diag(N_HEADS)
    qs = jnp.where(_head_diag(BF16_ROWS), jnp.broadcast_to(q_ref[...], (BF16_ROWS, WIDTH)), 0.0)
    qb = qs.astype(BF16)
    s = jnp.concatenate([_dot(qb, k_pages[p][...].astype(BF16))[:N_HEADS] for p in range(n_pages)], axis=1)
    s_new = jnp.sum(qs[:N_HEADS] * knew_ref[...].astype(BF16).astype(F32), axis=1, keepdims=True)
    x, x_new = logit_terms(s, s_new)
    m = jnp.maximum(jnp.max(x, axis=1, keepdims=True), x_new)
    p = jnp.exp(x - m)
    p_new = jnp.exp(x_new - m)
    denom = jnp.sum(p, axis=1, keepdims=True) + p_new
    pb = jnp.concatenate([p, jnp.zeros_like(p)], axis=0).astype(BF16)
    acc = p_new.astype(BF16).astype(F32) * vnew_ref[...].astype(BF16).astype(F32)
    for q in range(n_pages):
        acc = acc + _nt_dot(pb[:, page * q:page * (q + 1)], v_pages[q][...].astype(BF16))[:N_HEADS]
    o_ref[...] = jnp.sum(jnp.where(diag, acc / denom, 0.0), axis=0, keepdims=True)


def _sattn_kernel(pt_ref, qa_ref, qb_ref, kan_ref, van_ref, kbn_ref, vbn_ref, lfn_ref, bias_ref, slope_ref, *rest,
                  n_pages):
    ka_p, va_p, kb_p, vb_p, lf_p = (rest[g * n_pages:(g + 1) * n_pages] for g in range(5))
    oa_ref, ob_ref = rest[5 * n_pages:]
    past = n_pages * ka_p[0].shape[-1]

    def alibi_and_mask(s, s_new):
        rel = (lax.broadcasted_iota(I32, (1, past), 1) - past).astype(F32)
        bias = bias_ref[...]
        return s + slope_ref[...] * rel + bias[:, :past], s_new + bias[:, past:past + 1]

    _paged_attend(qa_ref, kan_ref, van_ref, ka_p, va_p, alibi_and_mask, oa_ref)

    lf = jnp.concatenate([lf_p[p][...] for p in range(n_pages)], axis=1)
    idx = lax.broadcasted_iota(I32, lf.shape, 1)
    d = 1
    while d < past:
        lf = lf + jnp.where(idx >= d, pltpu.roll(lf, d, 1), 0.0)
        d *= 2
    f_new = lf[:, past - 1:past] + lfn_ref[...]
    _paged_attend(qb_ref, kbn_ref, vbn_ref, kb_p, vb_p, lambda s, s_new: (s + (f_new - lf), s_new), ob_ref)


def _sample_attention(layer, page_table, caches, logf_t, qa_s, qb_s, new_rows, logf_new, bias):
    cache_k_a, cache_v_a, cache_k_b, cache_v_b = caches
    n_db, n_pages = page_table.shape
    n_pool, page = cache_k_a.shape[1], cache_k_a.shape[2]
    l_pad = bias.shape[1]
    flat = lambda c: jnp.transpose(c, (0, 1, 3, 4, 2)).reshape(c.shape[0], n_pool, WIDTH, page)
    row = lambda width: pl.BlockSpec((None, 1, width), lambda b, pt: (b, 0, 0))
    kv_spec = lambda p: pl.BlockSpec((None, None, WIDTH, page), lambda b, pt: (layer, pt[b, p], 0, 0))
    lf_spec = lambda p: pl.BlockSpec((None, N_HEADS, page), lambda b, pt: (pt[b, p], 0, 0))
    pages = range(n_pages)
    grid_spec = pltpu.PrefetchScalarGridSpec(
        num_scalar_prefetch=1,
        grid=(n_db,),
        in_specs=[row(WIDTH), row(WIDTH), row(WIDTH), row(WIDTH), row(WIDTH), row(WIDTH),
                  pl.BlockSpec((None, N_HEADS, 1), lambda b, pt: (b, 0, 0)), row(l_pad),
                  pl.BlockSpec((N_HEADS, 1), lambda b, pt: (0, 0))]
                 + [kv_spec(p) for p in pages] * 4 + [lf_spec(p) for p in pages],
        out_specs=(row(WIDTH), row(WIDTH)),
    )
    r3 = lambda a: a.reshape(n_db, 1, a.shape[-1])
    slopes = jnp.asarray(ALIBI_SLOPES, F32).reshape(N_HEADS, 1)
    return pl.pallas_call(
        functools.partial(_sattn_kernel, n_pages=n_pages),
        out_shape=(jax.ShapeDtypeStruct((n_db, 1, WIDTH), F32), jax.ShapeDtypeStruct((n_db, 1, WIDTH), F32)),
        grid_spec=grid_spec,
        compiler_params=_params("arbitrary"),
        name="sample_attention",
    )(page_table, r3(qa_s.astype(F32)), r3(qb_s.astype(F32)), *[r3(a) for a in new_rows],
      logf_new.reshape(n_db, N_HEADS, 1), r3(bias), slopes,
      *([flat(cache_k_a)] * n_pages), *([flat(cache_v_a)] * n_pages), *([flat(cache_k_b)] * n_pages),
      *([flat(cache_v_b)] * n_pages), *([logf_t] * n_pages))


def _fox_kernel(q_ref, qx_ref, k_ref, kx_ref, vt_ref, o_ref, qaug_scr, x_scr, p_scr, m_scr, acc_scr, *, tq, ck):
    q0 = pl.program_id(1) * tq
    n_chunks = (q0 + tq + ck - 1) // ck
    lane = lax.broadcasted_iota(I32, (tq, LANES), 1)
    zeros = jnp.zeros((tq, LANES), BF16)
    qx = qx_ref[...]
    for h in range(N_HEADS):
        own = (lane >= FX * h) & (lane < FX * (h + 1))
        qaug_scr[h] = jnp.concatenate([jnp.where(_head_pick(tq, h), q_ref[:, _pair_lanes(h)], zeros),
                                       jnp.where(own, qx, zeros)], axis=1)
    _flash_init(m_scr, acc_scr)
    krow = lax.broadcasted_iota(I32, (ck, tq), 0)
    qcol = lax.broadcasted_iota(I32, (ck, tq), 1)

    def chunk(c, masked):
        c0 = pl.multiple_of(c * ck, ck)
        k_aug = lambda h: jnp.concatenate([k_ref[pl.ds(c0, ck), _pair_lanes(h)], kx_ref[pl.ds(c0, ck), :]], axis=1)
        visible = ((krow + c0) <= (qcol + q0)) if masked else None
        _flash_chunk(k_aug, qaug_scr, lambda h: vt_ref[c, HEAD_DIM * h:HEAD_DIM * (h + 1), :],
                     x_scr, p_scr, m_scr, acc_scr, visible=visible)

    def full_chunk(c, carry):
        chunk(c, False)
        return carry

    lax.fori_loop(0, n_chunks - 1, full_chunk, 0)
    chunk(n_chunks - 1, True)
    _flash_output(o_ref, acc_scr)


def _fox_prompt(qb, qx, kb, kx, vb_t, tq, ck):
    n_b, seq, _ = qb.shape
    per_q = lambda width: pl.BlockSpec((None, tq, width), lambda b, i: (b, i, 0))
    whole = lambda width: pl.BlockSpec((None, seq, width), lambda b, i: (b, 0, 0))
    return pl.pallas_call(
        functools.partial(_fox_kernel, tq=tq, ck=ck),
        out_shape=jax.ShapeDtypeStruct((n_b, seq, WIDTH), BF16),
        grid=(n_b, seq // tq),
        in_specs=[per_q(WIDTH), per_q(LANES), whole(WIDTH), whole(LANES),
                  pl.BlockSpec((None, seq // ck, WIDTH, ck), lambda b, i: (b, 0, 0, 0))],
        out_specs=per_q(WIDTH),
        scratch_shapes=_flash_scratch(N_HEADS, tq, ck),
        compiler_params=_params("arbitrary", "arbitrary"),
        name="fox_prompt",
    )(qb, qx, kb, kx, vb_t)


def _post_kernel(x_ref, oa_ref, ob_ref, sga_ref, sgb_ref, gate1_ref, shift2_ref, scale2_ref, gate2_ref, g2_ref,
                 woa_ref, wob_ref, wout_ref, wup_ref, wdn_ref, y_ref, *, ff_chunk):
    merged = (sga_ref[...].astype(F32) * _dot(oa_ref[...], woa_ref[...])
              + sgb_ref[...].astype(F32) * _dot(ob_ref[...], wob_ref[...]))
    x1 = x_ref[...] + gate1_ref[...] * _dot(merged.astype(BF16), wout_ref[...])
    h2 = x1 * lax.rsqrt(jnp.mean(x1 * x1, axis=-1, keepdims=True) + EPS) * g2_ref[...]
    h2 = (h2 * (1.0 + scale2_ref[...]) + shift2_ref[...]).astype(BF16)
    mlp = jnp.zeros(x1.shape, F32)
    for c in range(0, D_FF, ff_chunk):
        u = jnp.maximum(_dot(h2, wup_ref[:, c:c + ff_chunk]), 0.0)
        mlp = mlp + _dot((u * u).astype(BF16), wdn_ref[c:c + ff_chunk, :])
    y_ref[...] = x1 + gate2_ref[...] * mlp


def _post(x, oa, ob, sga, sgb, mod, g2, woa, wob, wout, wup, wdn, tm):
    n_g, seq, _ = x.shape
    n_r = mod.shape[1]
    rb = 1 if n_r == 1 else tm
    row = lambda width: pl.BlockSpec((None, tm, width), lambda g, i: (g, i, 0))
    modspec = lambda j: pl.BlockSpec((None, rb, D_MODEL), lambda g, i: (g, i if n_r > 1 else 0, j))
    return pl.pallas_call(
        functools.partial(_post_kernel, ff_chunk=1024),
        out_shape=jax.ShapeDtypeStruct((n_g, seq, D_MODEL), F32),
        grid=(n_g, seq // tm),
        in_specs=[row(D_MODEL), row(WIDTH), row(WIDTH), row(D_MODEL), row(D_MODEL),
                  modspec(2), modspec(3), modspec(4), modspec(5), _const_spec((1, D_MODEL)),
                  _const_spec((WIDTH, D_MODEL)), _const_spec((WIDTH, D_MODEL)), _const_spec((D_MODEL, D_MODEL)),
                  _const_spec((D_MODEL, D_FF)), _const_spec((D_FF, D_MODEL))],
        out_specs=row(D_MODEL),
        compiler_params=_params("arbitrary", "arbitrary"),
        name="merge_out_mlp",
    )(x, oa, ob, sga, sgb, mod, mod, mod, mod, g2, woa, wob, wout, wup, wdn)


def _lower_tri(n):
    return jnp.asarray(np.tril(np.ones((n, n), np.float32)), BF16)


def _layer(l, x_p, x_s, caches, cache_k_idx, logf_t, page_table, mod_p, mod_s, lw, tiles):
    (g_norm1, w_r, gains, bsm, gmat, woa, wob, wout, g_norm2, wup, wdn) = lw
    tm, tq = tiles
    n_b, seq, _ = x_p.shape
    n_db = x_s.shape[0]

    (ka_p, va_p, kb_p, vb_p, small_p, qa_bf, ka_bf, vat_bf, qb_bf, kb_bf, vbt_bf, qi_bf, ki2_bf, smallt_p,
     sga_p, sgb_p) = _in_proj(x_p, mod_p, g_norm1, w_r, gains, bsm, gmat, tm)
    xs = x_s.reshape(1, n_db, D_MODEL)
    (ka_s, va_s, kb_s, vb_s, small_s, qa_s, _, _, qb_s, _, _, qi_s, _, _, sga_s, sgb_s) = _in_proj(
        xs, mod_s, g_norm1, w_r, gains, bsm, gmat, n_db)
    small_s2 = small_s[0]
    ki_s, wi_s = small_s2[:, :SM_WI], small_s2[:, SM_WI:SM_LOGF]
    logf_s = small_s2[:, SM_LOGF:SM_LOGF + N_HEADS]

    n_keys = page_table.shape[1] * cache_k_idx.shape[2] + 1
    scores = _sample_index_scores(l, page_table, cache_k_idx, qi_s[0], wi_s, ki_s)
    bias_t = _sample_selection(scores.T, _lower_tri(LANES), n_keys)

    kx_b, qx_b = _forget_operands(small_p, _lower_tri(tm), tm)
    oa_s, ob_s = _sample_attention(l, page_table, caches, logf_t, qa_s[0], qb_s[0],
                                   (ka_s[0], va_s[0], kb_s[0], vb_s[0]), logf_s, bias_t.T)
    ob_p = _fox_prompt(qb_bf, qx_b, kb_bf, kx_b, vbt_bf, tq, tm)
    oa_p = _dsa_prompt(qi_bf, ki2_bf, smallt_p, qa_bf, ka_bf, vat_bf, _lower_tri(tm), tq, tm)

    y_p = _post(x_p, oa_p, ob_p, sga_p, sgb_p, mod_p, g_norm2, woa, wob, wout, wup, wdn, tm)
    y_s = _post(xs, oa_s.reshape(1, n_db, WIDTH).astype(BF16), ob_s.reshape(1, n_db, WIDTH).astype(BF16),
                sga_s, sgb_s, mod_s, g_norm2, woa, wob, wout, wup, wdn, n_db)

    heads = lambda a, lead: a.reshape(*lead, N_HEADS, HEAD_DIM)
    lp, ls = (n_b, seq), (n_db, 1)
    state_p = (heads(ka_p, lp), heads(va_p, lp), small_p[..., :SM_WI], heads(kb_p, lp), heads(vb_p, lp),
               small_p[..., SM_LOGF:SM_LOGF + N_HEADS])
    state_s = (heads(ka_s[0], ls), heads(va_s[0], ls), ki_s.reshape(n_db, 1, HEAD_DIM), heads(kb_s[0], ls),
               heads(vb_s[0], ls), logf_s.reshape(n_db, 1, N_HEADS))
    return y_p, y_s.reshape(n_db, 1, D_MODEL), state_p, state_s


def kernel(x_prompt, x_sample, cache_k_a, cache_v_a, cache_k_idx, cache_k_b, cache_v_b, cache_logf_b, page_table,
           c_prompt, c_sample, w_ada, b_ada, g_norm1, w_in, b_forget, g_qa, g_ka, g_qb, g_kb, w_o_a, w_o_b, w_out,
           g_norm2, w_up, w_down):
    depth = w_in.shape[0]
    n_b, seq, _ = x_prompt.shape
    n_db, dec_seq, _ = x_sample.shape
    assert dec_seq == 1 and x_prompt.shape[-1] == D_MODEL
    assert cache_k_a.shape[3:] == (N_HEADS, HEAD_DIM) and cache_k_idx.shape[3] == HEAD_DIM
    tiles = (min(512, seq), min(256, seq))

    c_all = jnp.concatenate([c_prompt, c_sample], axis=0)
    pad = (-c_all.shape[0]) % 8
    c_all = jnp.pad(c_all, ((0, pad), (0, 0)))
    gmat = jnp.asarray(np.kron(np.eye(N_HEADS, dtype=np.float32),
                               np.full((HEAD_DIM, HEAD_DIM), 1.0 / HEAD_DIM, np.float32)), BF16)
    split = 7 * WIDTH
    y_p, y_s = x_prompt, x_sample.reshape(n_db, D_MODEL)
    states_p, states_s = [], []
    for l in range(depth):
        mod = _ada_mod(c_all, w_ada[l], b_ada[l])
        mod_p = mod[:n_b].reshape(n_b, 1, 6 * D_MODEL)
        mod_s = mod[n_b:n_b + n_db].reshape(1, n_db, 6 * D_MODEL)
        w = w_in[l]
        k_idx_cols = w[:, split:split + HEAD_DIM]
        w_r = jnp.concatenate(
            [w[:, :split], w[:, split:split + HEAD_DIM + 2 * N_HEADS],
             jnp.zeros((D_MODEL, LANES - HEAD_DIM - 2 * N_HEADS), w.dtype),
             k_idx_cols, k_idx_cols, w[:, split + HEAD_DIM + 2 * N_HEADS:]], axis=1).astype(BF16)
        gains = jnp.stack([jnp.tile(g[l], N_HEADS) for g in (g_qa, g_ka, g_qb, g_kb)])
        bsm = jnp.zeros((1, LANES), F32).at[0, SM_LOGF:SM_LOGF + N_HEADS].set(b_forget[l])
        lw = (g_norm1[l].reshape(1, D_MODEL), w_r, gains, bsm, gmat, w_o_a[l].astype(BF16), w_o_b[l].astype(BF16),
              w_out[l].astype(BF16), g_norm2[l].reshape(1, D_MODEL), w_up[l].astype(BF16), w_down[l].astype(BF16))
        logf_t = jnp.swapaxes(cache_logf_b[l], 1, 2)
        y_p, y_s, st_p, st_s = _layer(l, y_p, y_s, (cache_k_a, cache_v_a, cache_k_b, cache_v_b), cache_k_idx,
                                      logf_t, page_table, mod_p, mod_s, lw, tiles)
        y_s = y_s.reshape(n_db, D_MODEL)
        states_p.append(st_p)
        states_s.append(st_s)
    k_a_p, v_a_p, k_idx_p, k_b_p, v_b_p, logf_p = (jnp.stack(s) for s in zip(*states_p))
    k_a_s, v_a_s, k_idx_s, k_b_s, v_b_s, logf_s = (jnp.stack(s) for s in zip(*states_s))
    return (y_p, y_s.reshape(n_db, 1, D_MODEL), k_a_p, v_a_p, k_idx_p, k_b_p, v_b_p, logf_p,
            k_a_s, v_a_s, k_idx_s, k_b_s, v_b_s, logf_s)
```

```python
import functools

import numpy as np
import jax
import jax.numpy as jnp
from jax import lax
from jax.experimental import pallas as pl
from jax.experimental.pallas import tpu as pltpu

F32 = jnp.float32
BF16 = jnp.bfloat16
I32 = jnp.int32

D_MODEL = 1024
HEAD_DIM = 64
N_HEADS = 8
WIDTH = N_HEADS * HEAD_DIM
D_FF = 4 * D_MODEL
TOPK_MAX = 256
EPS = 1e-6
MASK_VALUE = -1e30
LANES = 128
BF16_ROWS = 16
QK_SCALE = HEAD_DIM ** -0.5
WI_SCALE = N_HEADS ** -0.5
ALIBI_SLOPES = tuple(2.0 ** (-8.0 * (h + 1) / N_HEADS) for h in range(N_HEADS))
INT_MIN = -(2 ** 31)
VMEM_LIMIT = 56 * 1024 * 1024

C_QA, C_KA, C_VA, C_QB, C_KB, C_VB, C_QI = (i * WIDTH for i in range(7))
C_SMALL = 7 * WIDTH
C_KI2 = C_SMALL + LANES
C_GA = C_KI2 + LANES
C_GB = C_GA + D_MODEL
C_TOTAL = C_GB + D_MODEL
SM_WI = HEAD_DIM
SM_LOGF = HEAD_DIM + N_HEADS


def _topk_count(n_keys):
    return max(1, min(TOPK_MAX, n_keys // 4))


def _const_spec(shape):
    zeros = (0,) * len(shape)
    return pl.BlockSpec(shape, lambda *_: zeros, pipeline_mode=pl.Buffered(1))


def _params(*sem):
    return pltpu.CompilerParams(dimension_semantics=sem, vmem_limit_bytes=VMEM_LIMIT)


def _nt_dot(a, b):
    return lax.dot_general(a, b, (((1,), (1,)), ((), ())), preferred_element_type=F32)


def _dot(a, b):
    return jnp.dot(a, b, preferred_element_type=F32)


def _split3(x):
    p1 = x.astype(BF16)
    r1 = x - p1.astype(F32)
    p2 = r1.astype(BF16)
    p3 = (r1 - p2.astype(F32)).astype(BF16)
    return p1, p2, p3


def _ada_kernel(c_ref, w_ref, b_ref, o_ref):
    c = c_ref[...]
    s = c * jax.nn.sigmoid(c)
    o_ref[...] = _dot(s.astype(BF16), w_ref[...].astype(BF16)) + b_ref[...]


def _ada_mod(c, w_ada, b_ada):
    rows = c.shape[0]
    n_out = w_ada.shape[1]
    tn = D_MODEL
    return pl.pallas_call(
        _ada_kernel,
        out_shape=jax.ShapeDtypeStruct((rows, n_out), F32),
        grid=(n_out // tn,),
        in_specs=[pl.BlockSpec((rows, D_MODEL), lambda j: (0, 0)),
                  pl.BlockSpec((D_MODEL, tn), lambda j: (0, j)),
                  pl.BlockSpec((1, tn), lambda j: (0, j))],
        out_specs=pl.BlockSpec((rows, tn), lambda j: (0, j)),
        compiler_params=_params("arbitrary"),
        name="ada_mod",
    )(c, w_ada, b_ada.reshape(1, n_out))


def _log_sigmoid(x):
    return jnp.minimum(x, 0.0) - jnp.log(1.0 + jnp.exp(-jnp.abs(x)))


FX = 6


def _forget_placement():
    place = np.zeros((3, LANES, 2 * LANES), np.float32)
    const = np.zeros((1, 2 * LANES), np.float32)
    for h in range(N_HEADS):
        for i in range(3):
            place[i, SM_LOGF + h, FX * h + i] = 1.0
            place[i, SM_LOGF + h, LANES + FX * h + 3 + i] = 1.0
            const[0, FX * h + 3 + i] = 1.0
            const[0, LANES + FX * h + i] = -1.0
    return jnp.asarray(place, BF16), jnp.asarray(const, F32)


def _in_proj_kernel(x_ref, shift_ref, scale_ref, g1_ref, w_ref, gains_ref, bsm_ref, gmat_ref,
                    tri_ref, place_ref, fconst_ref,
                    ka_ref, va_ref, kb_ref, vb_ref, small_ref,
                    qa_bf, ka_bf, vat_bf, qb_bf, kb_bf, vbt_bf, qi_bf, ki2_bf, smallt_ref, sga_ref, sgb_ref,
                    kx_ref, qx_ref, carry_ref):
    x = x_ref[...]
    h = x * lax.rsqrt(jnp.mean(x * x, axis=-1, keepdims=True) + EPS) * g1_ref[...]
    h = h * (1.0 + scale_ref[...]) + shift_ref[...]
    hb = h.astype(BF16)

    def seg(col, width=WIDTH):
        return _dot(hb, w_ref[:, col:col + width])

    def head_norm(z, gi):
        msq = _dot((z * z).astype(BF16), gmat_ref[...])
        return z * lax.rsqrt(msq + EPS) * gains_ref[gi:gi + 1, :]

    qa = head_norm(seg(C_QA), 0)
    qa_bf[...] = (qa * QK_SCALE).astype(BF16)
    ka = head_norm(seg(C_KA), 1)
    ka_ref[...] = ka
    ka_bf[...] = ka.astype(BF16)
    va = seg(C_VA)
    va_ref[...] = va
    vat_bf[...] = va.T.astype(BF16)
    qb = head_norm(seg(C_QB), 2)
    qb_bf[...] = (qb * QK_SCALE).astype(BF16)
    kb = head_norm(seg(C_KB), 3)
    kb_ref[...] = kb
    kb_bf[...] = kb.astype(BF16)
    vb = seg(C_VB)
    vb_ref[...] = vb
    vbt_bf[...] = vb.T.astype(BF16)
    qi_bf[...] = seg(C_QI).astype(BF16)
    ki2_bf[...] = seg(C_KI2, LANES).astype(BF16)

    zs = seg(C_SMALL, LANES)
    lane = lax.broadcasted_iota(I32, zs.shape, 1)
    logf = _log_sigmoid(zs + bsm_ref[...])
    small = jnp.where(lane < SM_WI, zs,
                      jnp.where(lane < SM_LOGF, zs * WI_SCALE,
                                jnp.where(lane < SM_LOGF + N_HEADS, logf, 0.0)))
    small_ref[...] = small
    smallt_ref[...] = small.T

    @pl.when(pl.program_id(1) == 0)
    def _():
        carry_ref[...] = jnp.zeros_like(carry_ref)

    tri = tri_ref[...]
    p1, p2, p3 = _split3(small)
    cs = _dot(tri, p1) + _dot(tri, p2) + _dot(tri, p3) + carry_ref[...]
    n = cs.shape[0]
    carry_ref[...] = cs[n - 1:n, :]
    f1, f2, f3 = _split3(cs)
    ext = _dot(f1, place_ref[0]) + _dot(f2, place_ref[1]) + _dot(f3, place_ref[2]) + fconst_ref[...]
    kx_ref[...] = ext[:, :LANES].astype(BF16)
    qx_ref[...] = ext[:, LANES:].astype(BF16)

    sga_ref[...] = jax.nn.sigmoid(seg(C_GA, D_MODEL)).astype(BF16)
    sgb_ref[...] = jax.nn.sigmoid(seg(C_GB, D_MODEL)).astype(BF16)


def _in_proj(x, mod, g1, w_r, gains, bsm, gmat, tm):
    n_g, seq, _ = x.shape
    n_r = mod.shape[1]
    rb = 1 if n_r == 1 else tm
    nblk = seq // tm
    row = lambda width: pl.BlockSpec((None, tm, width), lambda g, i: (g, i, 0))
    modspec = lambda j: pl.BlockSpec((None, rb, D_MODEL), lambda g, i: (g, i if n_r > 1 else 0, j))
    sds = lambda shape, dt: jax.ShapeDtypeStruct(shape, dt)
    out_shape = (
        sds((n_g, seq, WIDTH), F32), sds((n_g, seq, WIDTH), F32), sds((n_g, seq, WIDTH), F32),
        sds((n_g, seq, WIDTH), F32), sds((n_g, seq, LANES), F32),
        sds((n_g, seq, WIDTH), BF16), sds((n_g, seq, WIDTH), BF16), sds((n_g, nblk, WIDTH, tm), BF16),
        sds((n_g, seq, WIDTH), BF16), sds((n_g, seq, WIDTH), BF16), sds((n_g, nblk, WIDTH, tm), BF16),
        sds((n_g, seq, WIDTH), BF16), sds((n_g, seq, LANES), BF16), sds((n_g, LANES, seq), F32),
        sds((n_g, seq, D_MODEL), BF16), sds((n_g, seq, D_MODEL), BF16),
        sds((n_g, seq, LANES), BF16), sds((n_g, seq, LANES), BF16),
    )
    chunk_t = pl.BlockSpec((None, None, WIDTH, tm), lambda g, i: (g, i, 0, 0))
    out_specs = (
        row(WIDTH), row(WIDTH), row(WIDTH), row(WIDTH), row(LANES),
        row(WIDTH), row(WIDTH), chunk_t,
        row(WIDTH), row(WIDTH), chunk_t,
        row(WIDTH), row(LANES), pl.BlockSpec((None, LANES, tm), lambda g, i: (g, 0, i)),
        row(D_MODEL), row(D_MODEL),
        row(LANES), row(LANES),
    )
    place, fconst = _forget_placement()
    return pl.pallas_call(
        _in_proj_kernel,
        out_shape=out_shape,
        grid=(n_g, nblk),
        in_specs=[row(D_MODEL), modspec(0), modspec(1), _const_spec((1, D_MODEL)),
                  _const_spec((D_MODEL, C_TOTAL)), _const_spec((4, WIDTH)), _const_spec((1, LANES)),
                  _const_spec((WIDTH, WIDTH)), _const_spec((tm, tm)), _const_spec((3, LANES, 2 * LANES)),
                  _const_spec((1, 2 * LANES))],
        out_specs=out_specs,
        scratch_shapes=[pltpu.VMEM((1, LANES), F32)],
        compiler_params=_params("arbitrary", "arbitrary"),
        name="in_proj",
    )(x, mod, mod, g1, w_r, gains, bsm, gmat, _lower_tri(tm), place, fconst)


def _flash_chunk(k_aug, qaug_scr, vt_of, x_scr, p_scr, m_scr, acc_scr, bias=None, visible=None):
    n_slots, ck = x_scr.shape[0], x_scr.shape[1]
    ones = jnp.ones((BF16_ROWS, ck), BF16)
    alphas = {}

    def logits(s):
        x = _nt_dot(k_aug(s), qaug_scr[s])
        if bias is not None:
            x = x + bias
        if visible is not None:
            x = jnp.where(visible, x, MASK_VALUE)
        x_scr[s] = x
        m_old = m_scr[s]
        m_new = jnp.maximum(m_old, jnp.max(x, axis=0, keepdims=True))
        alphas[s] = jnp.exp(m_old - m_new)
        m_scr[s] = m_new

    def probabilities(s):
        p_scr[s] = jnp.exp(x_scr[s] - m_scr[s]).astype(BF16)

    def values(s):
        acc_scr[s] = alphas[s] * acc_scr[s] + _dot(jnp.concatenate([vt_of(s), ones], axis=0), p_scr[s])

    for phase in (logits, probabilities, values):
        for s in range(n_slots):
            phase(s)


def _flash_scratch(n_slots, tq, ck):
    return [pltpu.VMEM((n_slots, tq, 2 * LANES), BF16), pltpu.VMEM((n_slots, ck, tq), F32),
            pltpu.VMEM((n_slots, ck, tq), BF16), pltpu.VMEM((n_slots, 1, tq), F32),
            pltpu.VMEM((n_slots, HEAD_DIM + BF16_ROWS, tq), F32)]


def _flash_init(m_scr, acc_scr):
    m_scr[...] = jnp.full(m_scr.shape, MASK_VALUE, F32)
    acc_scr[...] = jnp.zeros_like(acc_scr)


def _flash_output(o_ref, acc_scr):
    out_t = jnp.concatenate([acc_scr[s, :HEAD_DIM] / acc_scr[s, HEAD_DIM:HEAD_DIM + 1]
                             for s in range(acc_scr.shape[0])], axis=0)
    o_ref[...] = out_t.T.astype(BF16)


def _head_pick(tq, h):
    lane = lax.broadcasted_iota(I32, (tq, LANES), 1)
    return (lane >= HEAD_DIM) == (h % 2 == 1)


def _pair_lanes(h):
    return slice(LANES * (h // 2), LANES * (h // 2 + 1))


def _ordered_float(u):
    key = u ^ jnp.int32(INT_MIN)
    return pltpu.bitcast(key ^ ((key >> 31) & jnp.int32(0x7FFFFFFF)), F32)


BISECT_STEPS = 22
CUT_UNKNOWN, CUT_EXACT, CUT_RANKED = 0, 1, 2


def _selection_bias(score_scr, bias_scr, tri_ref, n_chunks, n_beyond, valid_fn, *, ck, tq, k_top, try_bisect=None):
    n_acc = 4

    def partials(blk, reduce):
        return reduce(blk.reshape(ck // (8 * n_acc), n_acc, 8, tq), axis=0)

    def finish(acc, reduce):
        return reduce(reduce(acc, axis=0), axis=0, keepdims=True)

    def count(cmp):
        def body(c, acc):
            c0 = pl.multiple_of(c * ck, ck)
            return acc + partials(jnp.where(cmp(score_scr[pl.ds(c0, ck), :]), 1, 0), jnp.sum)
        return finish(lax.fori_loop(0, n_chunks, body, jnp.zeros((n_acc, 8, tq), I32)), jnp.sum)

    def reaching(x):
        return count(lambda blk: blk >= x) + jnp.where(x <= MASK_VALUE, n_beyond, 0)

    def exact_cut(thr):
        def body(c, carry):
            c0 = pl.multiple_of(c * ck, ck)
            keep = jnp.where(score_scr[pl.ds(c0, ck), :] >= thr, 0.0, MASK_VALUE)
            bias_scr[pl.ds(c0, ck), :] = jnp.where(valid_fn(c0), keep, MASK_VALUE)
            return carry
        lax.fori_loop(0, n_chunks, body, 0)

    def ranked_cut(thr):
        n_greater = count(lambda blk: blk > thr) + jnp.where(thr < MASK_VALUE, n_beyond, 0)
        ties_wanted = (k_top - n_greater).astype(F32)

        def body(c, ties_before):
            c0 = pl.multiple_of(c * ck, ck)
            blk = score_scr[pl.ds(c0, ck), :]
            tie = blk == thr
            rank = _dot(tri_ref[...], jnp.where(tie, 1.0, 0.0).astype(BF16)) + ties_before
            keep = jnp.where(blk > thr, 0.0,
                             jnp.where(tie, jnp.where(rank <= ties_wanted, 0.0, MASK_VALUE), MASK_VALUE))
            bias_scr[pl.ds(c0, ck), :] = jnp.where(valid_fn(c0), keep, MASK_VALUE)
            return rank[ck - 1:ck, :]
        lax.fori_loop(0, n_chunks, body, jnp.zeros((1, tq), F32))

    def bitwise():
        def bit_step(i, carry):
            code, reached = carry
            cand_code = code | lax.shift_left(jnp.int32(1), 31 - i)
            cnt = reaching(_ordered_float(cand_code))
            take = cnt >= k_top
            return jnp.where(take, cand_code, code), jnp.where(take, cnt, reached)

        code, reached = lax.fori_loop(0, 32, bit_step, (jnp.zeros((1, tq), I32), jnp.full((1, tq), k_top, I32)))
        thr = _ordered_float(code)
        lax.cond(jnp.max(reached) > k_top, lambda: ranked_cut(thr), lambda: exact_cut(thr))

    if try_bisect is None:
        bitwise()
        return

    def unsettled(reached):
        return jnp.max(jnp.abs(reached - k_top)) > 0

    def bisect():
        def bounds(c, carry):
            lo, hi = carry
            c0 = pl.multiple_of(c * ck, ck)
            blk = score_scr[pl.ds(c0, ck), :]
            lo = jnp.minimum(lo, partials(jnp.where(valid_fn(c0), blk, jnp.inf), jnp.min))
            return lo, jnp.maximum(hi, partials(blk, jnp.max))
        lo, hi = lax.fori_loop(0, n_chunks, bounds, (jnp.full((n_acc, 8, tq), jnp.inf, F32),
                                                      jnp.full((n_acc, 8, tq), -jnp.inf, F32)))

        def halve(_, state):
            lo, hi, reached = state
            mid = lo + 0.5 * (hi - lo)
            cnt = reaching(mid)
            up = cnt >= k_top
            return jnp.where(up, mid, lo), jnp.where(up, hi, mid), jnp.where(up, cnt, reached)

        state = (finish(lo, jnp.min), finish(hi, jnp.max), jnp.full((1, tq), k_top + 1, I32))
        lo, hi, reached = lax.fori_loop(0, BISECT_STEPS, halve, state)

        def tied():
            def below(c, acc):
                c0 = pl.multiple_of(c * ck, ck)
                blk = score_scr[pl.ds(c0, ck), :]
                return jnp.maximum(acc, partials(jnp.where(blk < hi, blk, -jnp.inf), jnp.max))
            top = finish(lax.fori_loop(0, n_chunks, below, jnp.full((n_acc, 8, tq), -jnp.inf, F32)), jnp.max)
            settled = reached == k_top
            confirmed = jnp.min(jnp.where(settled | (reaching(top) >= k_top), 1, 0)) > 0
            return jnp.where(settled, lo, top), jnp.where(confirmed, CUT_RANKED, CUT_UNKNOWN)

        return lax.cond(unsettled(reached), tied, lambda: (lo, jnp.int32(CUT_EXACT)))

    thr, cut = lax.cond(try_bisect, bisect, lambda: (jnp.zeros((1, tq), F32), jnp.int32(CUT_UNKNOWN)))
    lax.switch(cut, [bitwise, lambda: exact_cut(thr), lambda: ranked_cut(thr)])


POS_SPLIT = 256


def _alibi_lanes(pos, slope=None):
    lane = lax.broadcasted_iota(I32, pos.shape, 1)
    lo = pos & (POS_SPLIT - 1)
    hi_f, lo_f = (pos - lo).astype(F32), lo.astype(F32)
    if slope is None:
        val = jnp.where(lane == 0, hi_f, jnp.where(lane == 1, lo_f, jnp.where(lane < 4, 1.0, 0.0)))
    else:
        val = jnp.where(lane < 2, slope, jnp.where(lane == 2, hi_f * -slope, jnp.where(lane == 3, lo_f * -slope, 0.0)))
    return val.astype(BF16)


def _dsa_kernel(qi_ref, ki2_ref, w_ref, qa_ref, ka_ref, vat_ref, tri_ref, o_ref,
                qm_scr, score_scr, bias_scr, kx_scr, qaug_scr, x_scr, p_scr, m_scr, acc_scr, *, tq, ck, k_top, seq):
    q0 = pl.program_id(1) * tq
    n_chunks = (q0 + tq + ck - 1) // ck
    n_beyond = seq - n_chunks * ck

    zeros = jnp.zeros((tq, LANES), BF16)
    qpos = lax.broadcasted_iota(I32, (tq, LANES), 0) + q0
    for h in range(N_HEADS):
        pick = _head_pick(tq, h)
        qm_scr[h] = jnp.where(pick, qi_ref[:, _pair_lanes(h)], zeros)
        qaug_scr[h] = jnp.concatenate([jnp.where(pick, qa_ref[:, _pair_lanes(h)], zeros),
                                       _alibi_lanes(qpos, ALIBI_SLOPES[h])], axis=1)

    w = w_ref[...] * QK_SCALE
    krow = lax.broadcasted_iota(I32, (ck, tq), 0)
    qcol = lax.broadcasted_iota(I32, (ck, tq), 1)

    def visible(c0):
        return (krow + c0) <= (qcol + q0)

    def score_chunk(c, carry):
        c0 = pl.multiple_of(c * ck, ck)
        kblk = ki2_ref[pl.ds(c0, ck), :]
        acc = jnp.zeros((ck, tq), F32)
        for h in range(N_HEADS):
            acc = acc + jnp.maximum(_nt_dot(kblk, qm_scr[h]), 0.0) * w[h:h + 1, :]
        score_scr[pl.ds(c0, ck), :] = jnp.where(visible(c0), acc, MASK_VALUE)
        return carry

    lax.fori_loop(0, n_chunks, score_chunk, 0)

    _selection_bias(score_scr, bias_scr, tri_ref, n_chunks, n_beyond, visible, ck=ck, tq=tq, k_top=k_top,
                    try_bisect=q0 >= k_top)

    _flash_init(m_scr, acc_scr)
    kpos0 = lax.broadcasted_iota(I32, (ck, LANES), 0)

    def attend_chunk(c, carry):
        c0 = pl.multiple_of(c * ck, ck)
        kx_scr[...] = _alibi_lanes(kpos0 + c0)
        k_aug = lambda h: jnp.concatenate([ka_ref[pl.ds(c0, ck), _pair_lanes(h)], kx_scr[...]], axis=1)
        _flash_chunk(k_aug, qaug_scr, lambda h: vat_ref[c, HEAD_DIM * h:HEAD_DIM * (h + 1), :],
                     x_scr, p_scr, m_scr, acc_scr, bias=bias_scr[pl.ds(c0, ck), :])
        return carry

    lax.fori_loop(0, n_chunks, attend_chunk, 0)
    _flash_output(o_ref, acc_scr)


def _dsa_prompt(qi, ki2, small_t, qa, ka, va_t, tri, tq, ck):
    n_b, seq, _ = qi.shape
    nblk = seq // tq
    kern = functools.partial(_dsa_kernel, tq=tq, ck=ck, k_top=_topk_count(seq), seq=seq)
    return pl.pallas_call(
        kern,
        out_shape=jax.ShapeDtypeStruct((n_b, seq, WIDTH), BF16),
        grid=(n_b, nblk),
        in_specs=[pl.BlockSpec((None, tq, WIDTH), lambda b, i: (b, i, 0)),
                  pl.BlockSpec((None, seq, LANES), lambda b, i: (b, 0, 0)),
                  pl.BlockSpec((None, N_HEADS, tq), lambda b, i: (b, SM_WI // N_HEADS, i)),
                  pl.BlockSpec((None, tq, WIDTH), lambda b, i: (b, i, 0)),
                  pl.BlockSpec((None, seq, WIDTH), lambda b, i: (b, 0, 0)),
                  pl.BlockSpec((None, seq // ck, WIDTH, ck), lambda b, i: (b, 0, 0, 0)),
                  _const_spec((ck, ck))],
        out_specs=pl.BlockSpec((None, tq, WIDTH), lambda b, i: (b, i, 0)),
        scratch_shapes=[pltpu.VMEM((N_HEADS, tq, LANES), BF16), pltpu.VMEM((seq, tq), F32),
                        pltpu.VMEM((seq, tq), F32), pltpu.VMEM((ck, LANES), BF16)]
                       + _flash_scratch(N_HEADS, tq, ck),
        compiler_params=_params("arbitrary", "arbitrary"),
        name="dsa_prompt",
    )(qi, ki2, small_t, qa, ka, va_t, tri)


def _head_diag(n_rows):
    lane = lax.broadcasted_iota(I32, (n_rows, WIDTH), 1)
    sub = lax.broadcasted_iota(I32, (n_rows, WIDTH), 0)
    return (lane >= sub * HEAD_DIM) & (lane < (sub + 1) * HEAD_DIM)


def _sidx_kernel(pt_ref, q_ref, w_ref, knew_ref, *rest, n_pages, page):
    page_refs, o_ref = rest[:n_pages], rest[n_pages]
    q = q_ref[...]
    qp = jnp.concatenate([q, jnp.zeros((BF16_ROWS - N_HEADS, HEAD_DIM), F32)], axis=0).astype(BF16)
    w = w_ref[...] * QK_SCALE
    for p in range(n_pages):
        s = _dot(qp, page_refs[p][...].astype(BF16))[:N_HEADS]
        o_ref[:, page * p:page * (p + 1)] = jnp.sum(jnp.maximum(s, 0.0) * w, axis=0, keepdims=True)
    knew = knew_ref[...].astype(BF16).astype(F32)
    s_new = jnp.sum(q * knew, axis=1, keepdims=True)
    sc_new = jnp.sum(jnp.maximum(s_new, 0.0) * w, axis=0, keepdims=True)
    lane = lax.broadcasted_iota(I32, (1, LANES), 1)
    o_ref[:, page * n_pages:] = jnp.where(lane == 0, sc_new, 0.0)


def _sample_index_scores(layer, page_table, cache_k_idx, qi_s, wi_s, ki_s):
    n_db, n_pages = page_table.shape
    page = cache_k_idx.shape[2]
    l_pad = n_pages * page + LANES
    page_spec = lambda p: pl.BlockSpec((None, None, HEAD_DIM, page), lambda b, pt: (layer, pt[b, p], 0, 0))
    k_idx_t = jnp.swapaxes(cache_k_idx, 2, 3)
    grid_spec = pltpu.PrefetchScalarGridSpec(
        num_scalar_prefetch=1,
        grid=(n_db,),
        in_specs=[pl.BlockSpec((None, N_HEADS, HEAD_DIM), lambda b, pt: (b, 0, 0)),
                  pl.BlockSpec((None, N_HEADS, 1), lambda b, pt: (b, 0, 0)),
                  pl.BlockSpec((None, 1, HEAD_DIM), lambda b, pt: (b, 0, 0))]
                 + [page_spec(p) for p in range(n_pages)],
        out_specs=pl.BlockSpec((None, 1, l_pad), lambda b, pt: (b, 0, 0)),
    )
    out = pl.pallas_call(
        functools.partial(_sidx_kernel, n_pages=n_pages, page=page),
        out_shape=jax.ShapeDtypeStruct((n_db, 1, l_pad), F32),
        grid_spec=grid_spec,
        compiler_params=_params("arbitrary"),
        name="sample_index_scores",
    )(page_table, qi_s.astype(F32).reshape(n_db, N_HEADS, HEAD_DIM), wi_s.reshape(n_db, N_HEADS, 1),
      ki_s.reshape(n_db, 1, HEAD_DIM), *([k_idx_t] * n_pages))
    return out.reshape(n_db, l_pad)


def _ssel_kernel(s_ref, tri_ref, o_ref, score_scr, *, n_keys, ck, k_top):
    l_pad, tq = s_ref.shape
    krow = lax.broadcasted_iota(I32, (ck, tq), 0)

    def valid(c0):
        return (krow + c0) < n_keys

    def drop_padding(c, carry):
        c0 = pl.multiple_of(c * ck, ck)
        score_scr[pl.ds(c0, ck), :] = jnp.where(valid(c0), s_ref[pl.ds(c0, ck), :], -jnp.inf)
        return carry

    lax.fori_loop(0, l_pad // ck, drop_padding, 0)
    _selection_bias(score_scr, o_ref, tri_ref, l_pad // ck, 0, valid, ck=ck, tq=tq, k_top=k_top)


def _sample_selection(scores_t, tri, n_keys):
    l_pad, n_db = scores_t.shape
    return pl.pallas_call(
        functools.partial(_ssel_kernel, n_keys=n_keys, ck=LANES, k_top=_topk_count(n_keys)),
        out_shape=jax.ShapeDtypeStruct((l_pad, n_db), F32),
        grid=(1,),
        in_specs=[pl.BlockSpec((l_pad, n_db), lambda i: (0, 0)), pl.BlockSpec((LANES, LANES), lambda i: (0, 0))],
        out_specs=pl.BlockSpec((l_pad, n_db), lambda i: (0, 0)),
        scratch_shapes=[pltpu.VMEM((l_pad, n_db), F32)],
        compiler_params=_params("arbitrary"),
        name="sample_selection",
    )(scores_t, tri)


def _paged_attend(q_ref, knew_ref, vnew_ref, k_pages, v_pages, logit_terms, o_ref):
    n_pages, page = len(k_pages), k_pages[0].shape[-1]
    diag = _head_diag(N_HEADS)
    qs = jnp.where(_head_diag(BF16_ROWS), jnp.broadcast_to(q_ref[...], (BF16_ROWS, WIDTH)), 0.0)
    qb = qs.astype(BF16)
    s = jnp.concatenate([_dot(qb, k_pages[p][...].astype(BF16))[:N_HEADS] for p in range(n_pages)], axis=1)
    s_new = jnp.sum(qs[:N_HEADS] * knew_ref[...].astype(BF16).astype(F32), axis=1, keepdims=True)
    x, x_new = logit_terms(s, s_new)
    m = jnp.maximum(jnp.max(x, axis=1, keepdims=True), x_new)
    p = jnp.exp(x - m)
    p_new = jnp.exp(x_new - m)
    denom = jnp.sum(p, axis=1, keepdims=True) + p_new
    pb = jnp.concatenate([p, jnp.zeros_like(p)], axis=0).astype(BF16)
    acc = p_new.astype(BF16).astype(F32) * vnew_ref[...].astype(BF16).astype(F32)
    for q in range(n_pages):
        acc = acc + _nt_dot(pb[:, page * q:page * (q + 1)], v_pages[q][...].astype(BF16))[:N_HEADS]
    o_ref[...] = jnp.sum(jnp.where(diag, acc / denom, 0.0), axis=0, keepdims=True)


def _sattn_kernel(pt_ref, qa_ref, qb_ref, kan_ref, van_ref, kbn_ref, vbn_ref, lfn_ref, bias_ref, slope_ref, *rest,
                  n_pages):
    ka_p, va_p, kb_p, vb_p, lf_p = (rest[g * n_pages:(g + 1) * n_pages] for g in range(5))
    oa_ref, ob_ref = rest[5 * n_pages:]
    past = n_pages * ka_p[0].shape[-1]

    def alibi_and_mask(s, s_new):
        rel = (lax.broadcasted_iota(I32, (1, past), 1) - past).astype(F32)
        bias = bias_ref[...]
        return s + slope_ref[...] * rel + bias[:, :past], s_new + bias[:, past:past + 1]

    _paged_attend(qa_ref, kan_ref, van_ref, ka_p, va_p, alibi_and_mask, oa_ref)

    lf = jnp.concatenate([lf_p[p][...] for p in range(n_pages)], axis=1)
    idx = lax.broadcasted_iota(I32, lf.shape, 1)
    d = 1
    while d < past:
        lf = lf + jnp.where(idx >= d, pltpu.roll(lf, d, 1), 0.0)
        d *= 2
    f_new = lf[:, past - 1:past] + lfn_ref[...]
    _paged_attend(qb_ref, kbn_ref, vbn_ref, kb_p, vb_p, lambda s, s_new: (s + (f_new - lf), s_new), ob_ref)


def _sample_attention(layer, page_table, caches, logf_t, qa_s, qb_s, new_rows, logf_new, bias):
    cache_k_a, cache_v_a, cache_k_b, cache_v_b = caches
    n_db, n_pages = page_table.shape
    n_pool, page = cache_k_a.shape[1], cache_k_a.shape[2]
    l_pad = bias.shape[1]
    flat = lambda c: jnp.transpose(c, (0, 1, 3, 4, 2)).reshape(c.shape[0], n_pool, WIDTH, page)
    row = lambda width: pl.BlockSpec((None, 1, width), lambda b, pt: (b, 0, 0))
    kv_spec = lambda p: pl.BlockSpec((None, None, WIDTH, page), lambda b, pt: (layer, pt[b, p], 0, 0))
    lf_spec = lambda p: pl.BlockSpec((None, N_HEADS, page), lambda b, pt: (pt[b, p], 0, 0))
    pages = range(n_pages)
    grid_spec = pltpu.PrefetchScalarGridSpec(
        num_scalar_prefetch=1,
        grid=(n_db,),
        in_specs=[row(WIDTH), row(WIDTH), row(WIDTH), row(WIDTH), row(WIDTH), row(WIDTH),
                  pl.BlockSpec((None, N_HEADS, 1), lambda b, pt: (b, 0, 0)), row(l_pad),
                  pl.BlockSpec((N_HEADS, 1), lambda b, pt: (0, 0))]
                 + [kv_spec(p) for p in pages] * 4 + [lf_spec(p) for p in pages],
        out_specs=(row(WIDTH), row(WIDTH)),
    )
    r3 = lambda a: a.reshape(n_db, 1, a.shape[-1])
    slopes = jnp.asarray(ALIBI_SLOPES, F32).reshape(N_HEADS, 1)
    return pl.pallas_call(
        functools.partial(_sattn_kernel, n_pages=n_pages),
        out_shape=(jax.ShapeDtypeStruct((n_db, 1, WIDTH), F32), jax.ShapeDtypeStruct((n_db, 1, WIDTH), F32)),
        grid_spec=grid_spec,
        compiler_params=_params("arbitrary"),
        name="sample_attention",
    )(page_table, r3(qa_s.astype(F32)), r3(qb_s.astype(F32)), *[r3(a) for a in new_rows],
      logf_new.reshape(n_db, N_HEADS, 1), r3(bias), slopes,
      *([flat(cache_k_a)] * n_pages), *([flat(cache_v_a)] * n_pages), *([flat(cache_k_b)] * n_pages),
      *([flat(cache_v_b)] * n_pages), *([logf_t] * n_pages))


def _fox_kernel(q_ref, qx_ref, k_ref, kx_ref, vt_ref, o_ref, qaug_scr, x_scr, p_scr, m_scr, acc_scr, *, tq, ck):
    q0 = pl.program_id(1) * tq
    n_chunks = (q0 + tq + ck - 1) // ck
    lane = lax.broadcasted_iota(I32, (tq, LANES), 1)
    zeros = jnp.zeros((tq, LANES), BF16)
    qx = qx_ref[...]
    for h in range(N_HEADS):
        own = (lane >= FX * h) & (lane < FX * (h + 1))
        qaug_scr[h] = jnp.concatenate([jnp.where(_head_pick(tq, h), q_ref[:, _pair_lanes(h)], zeros),
                                       jnp.where(own, qx, zeros)], axis=1)
    _flash_init(m_scr, acc_scr)
    krow = lax.broadcasted_iota(I32, (ck, tq), 0)
    qcol = lax.broadcasted_iota(I32, (ck, tq), 1)

    def chunk(c, masked):
        c0 = pl.multiple_of(c * ck, ck)
        k_aug = lambda h: jnp.concatenate([k_ref[pl.ds(c0, ck), _pair_lanes(h)], kx_ref[pl.ds(c0, ck), :]], axis=1)
        visible = ((krow + c0) <= (qcol + q0)) if masked else None
        _flash_chunk(k_aug, qaug_scr, lambda h: vt_ref[c, HEAD_DIM * h:HEAD_DIM * (h + 1), :],
                     x_scr, p_scr, m_scr, acc_scr, visible=visible)

    def full_chunk(c, carry):
        chunk(c, False)
        return carry

    lax.fori_loop(0, n_chunks - 1, full_chunk, 0)
    chunk(n_chunks - 1, True)
    _flash_output(o_ref, acc_scr)


def _fox_prompt(qb, qx, kb, kx, vb_t, tq, ck):
    n_b, seq, _ = qb.shape
    per_q = lambda width: pl.BlockSpec((None, tq, width), lambda b, i: (b, i, 0))
    whole = lambda width: pl.BlockSpec((None, seq, width), lambda b, i: (b, 0, 0))
    return pl.pallas_call(
        functools.partial(_fox_kernel, tq=tq, ck=ck),
        out_shape=jax.ShapeDtypeStruct((n_b, seq, WIDTH), BF16),
        grid=(n_b, seq // tq),
        in_specs=[per_q(WIDTH), per_q(LANES), whole(WIDTH), whole(LANES),
                  pl.BlockSpec((None, seq // ck, WIDTH, ck), lambda b, i: (b, 0, 0, 0))],
        out_specs=per_q(WIDTH),
        scratch_shapes=_flash_scratch(N_HEADS, tq, ck),
        compiler_params=_params("arbitrary", "arbitrary"),
        name="fox_prompt",
    )(qb, qx, kb, kx, vb_t)


def _post_kernel(x_ref, oa_ref, ob_ref, sga_ref, sgb_ref, gate1_ref, shift2_ref, scale2_ref, gate2_ref, g2_ref,
                 woa_ref, wob_ref, wout_ref, wup_ref, wdn_ref, y_ref, *, ff_chunk):
    merged = (sga_ref[...].astype(F32) * _dot(oa_ref[...], woa_ref[...])
              + sgb_ref[...].astype(F32) * _dot(ob_ref[...], wob_ref[...]))
    x1 = x_ref[...] + gate1_ref[...] * _dot(merged.astype(BF16), wout_ref[...])
    h2 = x1 * lax.rsqrt(jnp.mean(x1 * x1, axis=-1, keepdims=True) + EPS) * g2_ref[...]
    h2 = (h2 * (1.0 + scale2_ref[...]) + shift2_ref[...]).astype(BF16)
    mlp = jnp.zeros(x1.shape, F32)
    for c in range(0, D_FF, ff_chunk):
        u = jnp.maximum(_dot(h2, wup_ref[:, c:c + ff_chunk]), 0.0)
        mlp = mlp + _dot((u * u).astype(BF16), wdn_ref[c:c + ff_chunk, :])
    y_ref[...] = x1 + gate2_ref[...] * mlp


def _post(x, oa, ob, sga, sgb, mod, g2, woa, wob, wout, wup, wdn, tm):
    n_g, seq, _ = x.shape
    n_r = mod.shape[1]
    rb = 1 if n_r == 1 else tm
    row = lambda width: pl.BlockSpec((None, tm, width), lambda g, i: (g, i, 0))
    modspec = lambda j: pl.BlockSpec((None, rb, D_MODEL), lambda g, i: (g, i if n_r > 1 else 0, j))
    return pl.pallas_call(
        functools.partial(_post_kernel, ff_chunk=1024),
        out_shape=jax.ShapeDtypeStruct((n_g, seq, D_MODEL), F32),
        grid=(n_g, seq // tm),
        in_specs=[row(D_MODEL), row(WIDTH), row(WIDTH), row(D_MODEL), row(D_MODEL),
                  modspec(2), modspec(3), modspec(4), modspec(5), _const_spec((1, D_MODEL)),
                  _const_spec((WIDTH, D_MODEL)), _const_spec((WIDTH, D_MODEL)), _const_spec((D_MODEL, D_MODEL)),
                  _const_spec((D_MODEL, D_FF)), _const_spec((D_FF, D_MODEL))],
        out_specs=row(D_MODEL),
        compiler_params=_params("arbitrary", "arbitrary"),
        name="merge_out_mlp",
    )(x, oa, ob, sga, sgb, mod, mod, mod, mod, g2, woa, wob, wout, wup, wdn)


def _lower_tri(n):
    return jnp.asarray(np.tril(np.ones((n, n), np.float32)), BF16)


def _layer(l, x_p, x_s, caches, cache_k_idx, logf_t, page_table, mod_p, mod_s, lw, tiles):
    (g_norm1, w_r, gains, bsm, gmat, woa, wob, wout, g_norm2, wup, wdn) = lw
    tm, tq = tiles
    n_b, seq, _ = x_p.shape
    n_db = x_s.shape[0]

    (ka_p, va_p, kb_p, vb_p, small_p, qa_bf, ka_bf, vat_bf, qb_bf, kb_bf, vbt_bf, qi_bf, ki2_bf, smallt_p,
     sga_p, sgb_p, kx_b, qx_b) = _in_proj(x_p, mod_p, g_norm1, w_r, gains, bsm, gmat, tm)
    xs = x_s.reshape(1, n_db, D_MODEL)
    (ka_s, va_s, kb_s, vb_s, small_s, qa_s, _, _, qb_s, _, _, qi_s, _, _, sga_s, sgb_s, _, _) = _in_proj(
        xs, mod_s, g_norm1, w_r, gains, bsm, gmat, n_db)
    small_s2 = small_s[0]
    ki_s, wi_s = small_s2[:, :SM_WI], small_s2[:, SM_WI:SM_LOGF]
    logf_s = small_s2[:, SM_LOGF:SM_LOGF + N_HEADS]

    n_keys = page_table.shape[1] * cache_k_idx.shape[2] + 1
    scores = _sample_index_scores(l, page_table, cache_k_idx, qi_s[0], wi_s, ki_s)
    bias_t = _sample_selection(scores.T, _lower_tri(LANES), n_keys)

    oa_s, ob_s = _sample_attention(l, page_table, caches, logf_t, qa_s[0], qb_s[0],
                                   (ka_s[0], va_s[0], kb_s[0], vb_s[0]), logf_s, bias_t.T)
    ob_p = _fox_prompt(qb_bf, qx_b, kb_bf, kx_b, vbt_bf, tq, tm)
    oa_p = _dsa_prompt(qi_bf, ki2_bf, smallt_p, qa_bf, ka_bf, vat_bf, _lower_tri(tm), tq, tm)

    y_p = _post(x_p, oa_p, ob_p, sga_p, sgb_p, mod_p, g_norm2, woa, wob, wout, wup, wdn, tm)
    y_s = _post(xs, oa_s.reshape(1, n_db, WIDTH).astype(BF16), ob_s.reshape(1, n_db, WIDTH).astype(BF16),
                sga_s, sgb_s, mod_s, g_norm2, woa, wob, wout, wup, wdn, n_db)

    heads = lambda a, lead: a.reshape(*lead, N_HEADS, HEAD_DIM)
    lp, ls = (n_b, seq), (n_db, 1)
    from_t = lambda lo, hi: jnp.swapaxes(smallt_p[:, lo:hi, :], 1, 2)
    state_p = (heads(ka_p, lp), heads(va_p, lp), from_t(0, SM_WI), heads(kb_p, lp), heads(vb_p, lp),
               from_t(SM_LOGF, SM_LOGF + N_HEADS))
    state_s = (heads(ka_s[0], ls), heads(va_s[0], ls), ki_s.reshape(n_db, 1, HEAD_DIM), heads(kb_s[0], ls),
               heads(vb_s[0], ls), logf_s.reshape(n_db, 1, N_HEADS))
    return y_p, y_s.reshape(n_db, 1, D_MODEL), state_p, state_s


def kernel(x_prompt, x_sample, cache_k_a, cache_v_a, cache_k_idx, cache_k_b, cache_v_b, cache_logf_b, page_table,
           c_prompt, c_sample, w_ada, b_ada, g_norm1, w_in, b_forget, g_qa, g_ka, g_qb, g_kb, w_o_a, w_o_b, w_out,
           g_norm2, w_up, w_down):
    depth = w_in.shape[0]
    n_b, seq, _ = x_prompt.shape
    n_db, dec_seq, _ = x_sample.shape
    assert dec_seq == 1 and x_prompt.shape[-1] == D_MODEL
    assert cache_k_a.shape[3:] == (N_HEADS, HEAD_DIM) and cache_k_idx.shape[3] == HEAD_DIM
    tiles = (min(512, seq), min(256, seq))

    c_all = jnp.concatenate([c_prompt, c_sample], axis=0)
    pad = (-c_all.shape[0]) % 8
    c_all = jnp.pad(c_all, ((0, pad), (0, 0)))
    gmat = jnp.asarray(np.kron(np.eye(N_HEADS, dtype=np.float32),
                               np.full((HEAD_DIM, HEAD_DIM), 1.0 / HEAD_DIM, np.float32)), BF16)
    split = 7 * WIDTH
    y_p, y_s = x_prompt, x_sample.reshape(n_db, D_MODEL)
    states_p, states_s = [], []
    for l in range(depth):
        mod = _ada_mod(c_all, w_ada[l], b_ada[l])
        mod_p = mod[:n_b].reshape(n_b, 1, 6 * D_MODEL)
        mod_s = mod[n_b:n_b + n_db].reshape(1, n_db, 6 * D_MODEL)
        w = w_in[l]
        k_idx_cols = w[:, split:split + HEAD_DIM]
        w_r = jnp.concatenate(
            [w[:, :split], w[:, split:split + HEAD_DIM + 2 * N_HEADS],
             jnp.zeros((D_MODEL, LANES - HEAD_DIM - 2 * N_HEADS), w.dtype),
             k_idx_cols, k_idx_cols, w[:, split + HEAD_DIM + 2 * N_HEADS:]], axis=1).astype(BF16)
        gains = jnp.stack([jnp.tile(g[l], N_HEADS) for g in (g_qa, g_ka, g_qb, g_kb)])
        bsm = jnp.zeros((1, LANES), F32).at[0, SM_LOGF:SM_LOGF + N_HEADS].set(b_forget[l])
        lw = (g_norm1[l].reshape(1, D_MODEL), w_r, gains, bsm, gmat, w_o_a[l].astype(BF16), w_o_b[l].astype(BF16),
              w_out[l].astype(BF16), g_norm2[l].reshape(1, D_MODEL), w_up[l].astype(BF16), w_down[l].astype(BF16))
        logf_t = jnp.swapaxes(cache_logf_b[l], 1, 2)
        y_p, y_s, st_p, st_s = _layer(l, y_p, y_s, (cache_k_a, cache_v_a, cache_k_b, cache_v_b), cache_k_idx,
                                      logf_t, page_table, mod_p, mod_s, lw, tiles)
        y_s = y_s.reshape(n_db, D_MODEL)
        states_p.append(st_p)
        states_s.append(st_s)
    k_a_p, v_a_p, k_idx_p, k_b_p, v_b_p, logf_p = (jnp.stack(s) for s in zip(*states_p))
    k_a_s, v_a_s, k_idx_s, k_b_s, v_b_s, logf_s = (jnp.stack(s) for s in zip(*states_s))
    return (y_p, y_s.reshape(n_db, 1, D_MODEL), k_a_p, v_a_p, k_idx_p, k_b_p, v_b_p, logf_p,
            k_a_s, v_a_s, k_idx_s, k_b_s, v_b_s, logf_s)
```

```python
import functools

import numpy as np
import jax
import jax.numpy as jnp
from jax import lax
from jax.experimental import pallas as pl
from jax.experimental.pallas import tpu as pltpu

F32 = jnp.float32
BF16 = jnp.bfloat16
I32 = jnp.int32

D_MODEL = 1024
HEAD_DIM = 64
N_HEADS = 8
WIDTH = N_HEADS * HEAD_DIM
D_FF = 4 * D_MODEL
TOPK_MAX = 256
EPS = 1e-6
MASK_VALUE = -1e30
LANES = 128
BF16_ROWS = 16
QK_SCALE = HEAD_DIM ** -0.5
WI_SCALE = N_HEADS ** -0.5
ALIBI_SLOPES = tuple(2.0 ** (-8.0 * (h + 1) / N_HEADS) for h in range(N_HEADS))
INT_MIN = -(2 ** 31)
VMEM_LIMIT = 56 * 1024 * 1024

C_QA, C_KA, C_VA, C_QB, C_KB, C_VB, C_QI = (i * WIDTH for i in range(7))
C_SMALL = 7 * WIDTH
C_KI2 = C_SMALL + LANES
C_GA = C_KI2 + LANES
C_GB = C_GA + D_MODEL
C_TOTAL = C_GB + D_MODEL
SM_WI = HEAD_DIM
SM_LOGF = HEAD_DIM + N_HEADS


def _topk_count(n_keys):
    return max(1, min(TOPK_MAX, n_keys // 4))


def _const_spec(shape):
    zeros = (0,) * len(shape)
    return pl.BlockSpec(shape, lambda *_: zeros, pipeline_mode=pl.Buffered(1))


def _params(*sem):
    return pltpu.CompilerParams(dimension_semantics=sem, vmem_limit_bytes=VMEM_LIMIT)


def _nt_dot(a, b):
    return lax.dot_general(a, b, (((1,), (1,)), ((), ())), preferred_element_type=F32)


def _dot(a, b):
    return jnp.dot(a, b, preferred_element_type=F32)


def _split3(x):
    p1 = x.astype(BF16)
    r1 = x - p1.astype(F32)
    p2 = r1.astype(BF16)
    p3 = (r1 - p2.astype(F32)).astype(BF16)
    return p1, p2, p3


def _ada_kernel(c_ref, w_ref, b_ref, o_ref):
    c = c_ref[...]
    s = c * jax.nn.sigmoid(c)
    o_ref[...] = _dot(s.astype(BF16), w_ref[...].astype(BF16)) + b_ref[...]


def _ada_mod(c, w_ada, b_ada):
    rows = c.shape[0]
    n_out = w_ada.shape[1]
    tn = D_MODEL
    return pl.pallas_call(
        _ada_kernel,
        out_shape=jax.ShapeDtypeStruct((rows, n_out), F32),
        grid=(n_out // tn,),
        in_specs=[pl.BlockSpec((rows, D_MODEL), lambda j: (0, 0)),
                  pl.BlockSpec((D_MODEL, tn), lambda j: (0, j)),
                  pl.BlockSpec((1, tn), lambda j: (0, j))],
        out_specs=pl.BlockSpec((rows, tn), lambda j: (0, j)),
        compiler_params=_params("arbitrary"),
        name="ada_mod",
    )(c, w_ada, b_ada.reshape(1, n_out))


def _log_sigmoid(x):
    return jnp.minimum(x, 0.0) - jnp.log(1.0 + jnp.exp(-jnp.abs(x)))


FX = 6


def _forget_placement():
    place = np.zeros((3, LANES, 2 * LANES), np.float32)
    const = np.zeros((1, 2 * LANES), np.float32)
    for h in range(N_HEADS):
        for i in range(3):
            place[i, SM_LOGF + h, FX * h + i] = 1.0
            place[i, SM_LOGF + h, LANES + FX * h + 3 + i] = 1.0
            const[0, FX * h + 3 + i] = 1.0
            const[0, LANES + FX * h + i] = -1.0
    return jnp.asarray(place, BF16), jnp.asarray(const, F32)


def _in_proj_kernel(x_ref, shift_ref, scale_ref, g1_ref, w_ref, gains_ref, bsm_ref, gmat_ref,
                    tri_ref, place_ref, fconst_ref,
                    ka_ref, va_ref, kb_ref, vb_ref, small_ref,
                    qa_bf, ka_bf, vat_bf, qb_bf, kb_bf, vbt_bf, qi_bf, ki2_bf, smallt_ref, sga_ref, sgb_ref,
                    kx_ref, qx_ref, carry_ref):
    x = x_ref[...]
    h = x * lax.rsqrt(jnp.mean(x * x, axis=-1, keepdims=True) + EPS) * g1_ref[...]
    h = h * (1.0 + scale_ref[...]) + shift_ref[...]
    hb = h.astype(BF16)

    def seg(col, width=WIDTH):
        return _dot(hb, w_ref[:, col:col + width])

    def head_norm(z, gi):
        msq = _dot((z * z).astype(BF16), gmat_ref[...])
        return z * lax.rsqrt(msq + EPS) * gains_ref[gi:gi + 1, :]

    qa = head_norm(seg(C_QA), 0)
    qa_bf[...] = (qa * QK_SCALE).astype(BF16)
    ka = head_norm(seg(C_KA), 1)
    ka_ref[...] = ka
    ka_bf[...] = ka.astype(BF16)
    va = seg(C_VA)
    va_ref[...] = va
    vat_bf[...] = va.T.astype(BF16)
    qb = head_norm(seg(C_QB), 2)
    qb_bf[...] = (qb * QK_SCALE).astype(BF16)
    kb = head_norm(seg(C_KB), 3)
    kb_ref[...] = kb
    kb_bf[...] = kb.astype(BF16)
    vb = seg(C_VB)
    vb_ref[...] = vb
    vbt_bf[...] = vb.T.astype(BF16)
    qi_bf[...] = seg(C_QI).astype(BF16)
    ki2_bf[...] = seg(C_KI2, LANES).astype(BF16)

    zs = seg(C_SMALL, LANES)
    lane = lax.broadcasted_iota(I32, zs.shape, 1)
    logf = _log_sigmoid(zs + bsm_ref[...])
    small = jnp.where(lane < SM_WI, zs,
                      jnp.where(lane < SM_LOGF, zs * WI_SCALE,
                                jnp.where(lane < SM_LOGF + N_HEADS, logf, 0.0)))
    small_ref[...] = small
    smallt_ref[...] = small.T

    @pl.when(pl.program_id(1) == 0)
    def _():
        carry_ref[...] = jnp.zeros_like(carry_ref)

    tri = tri_ref[...]
    p1, p2, p3 = _split3(small)
    cs = _dot(tri, p1) + _dot(tri, p2) + _dot(tri, p3) + carry_ref[...]
    n = cs.shape[0]
    carry_ref[...] = cs[n - 1:n, :]
    f1, f2, f3 = _split3(cs)
    ext = _dot(f1, place_ref[0]) + _dot(f2, place_ref[1]) + _dot(f3, place_ref[2]) + fconst_ref[...]
    kx_ref[...] = ext[:, :LANES].astype(BF16)
    qx_ref[...] = ext[:, LANES:].astype(BF16)

    sga_ref[...] = jax.nn.sigmoid(seg(C_GA, D_MODEL)).astype(BF16)
    sgb_ref[...] = jax.nn.sigmoid(seg(C_GB, D_MODEL)).astype(BF16)


def _in_proj(x, mod, g1, w_r, gains, bsm, gmat, tm):
    n_g, seq, _ = x.shape
    n_r = mod.shape[1]
    rb = 1 if n_r == 1 else tm
    nblk = seq // tm
    row = lambda width: pl.BlockSpec((None, tm, width), lambda g, i: (g, i, 0))
    modspec = lambda j: pl.BlockSpec((None, rb, D_MODEL), lambda g, i: (g, i if n_r > 1 else 0, j))
    sds = lambda shape, dt: jax.ShapeDtypeStruct(shape, dt)
    out_shape = (
        sds((n_g, seq, WIDTH), F32), sds((n_g, seq, WIDTH), F32), sds((n_g, seq, WIDTH), F32),
        sds((n_g, seq, WIDTH), F32), sds((n_g, seq, LANES), F32),
        sds((n_g, seq, WIDTH), BF16), sds((n_g, seq, WIDTH), BF16), sds((n_g, nblk, WIDTH, tm), BF16),
        sds((n_g, seq, WIDTH), BF16), sds((n_g, seq, WIDTH), BF16), sds((n_g, nblk, WIDTH, tm), BF16),
        sds((n_g, seq, WIDTH), BF16), sds((n_g, seq, LANES), BF16), sds((n_g, LANES, seq), F32),
        sds((n_g, seq, D_MODEL), BF16), sds((n_g, seq, D_MODEL), BF16),
        sds((n_g, seq, LANES), BF16), sds((n_g, seq, LANES), BF16),
    )
    chunk_t = pl.BlockSpec((None, None, WIDTH, tm), lambda g, i: (g, i, 0, 0))
    out_specs = (
        row(WIDTH), row(WIDTH), row(WIDTH), row(WIDTH), row(LANES),
        row(WIDTH), row(WIDTH), chunk_t,
        row(WIDTH), row(WIDTH), chunk_t,
        row(WIDTH), row(LANES), pl.BlockSpec((None, LANES, tm), lambda g, i: (g, 0, i)),
        row(D_MODEL), row(D_MODEL),
        row(LANES), row(LANES),
    )
    place, fconst = _forget_placement()
    return pl.pallas_call(
        _in_proj_kernel,
        out_shape=out_shape,
        grid=(n_g, nblk),
        in_specs=[row(D_MODEL), modspec(0), modspec(1), _const_spec((1, D_MODEL)),
                  _const_spec((D_MODEL, C_TOTAL)), _const_spec((4, WIDTH)), _const_spec((1, LANES)),
                  _const_spec((WIDTH, WIDTH)), _const_spec((tm, tm)), _const_spec((3, LANES, 2 * LANES)),
                  _const_spec((1, 2 * LANES))],
        out_specs=out_specs,
        scratch_shapes=[pltpu.VMEM((1, LANES), F32)],
        compiler_params=_params("arbitrary", "arbitrary"),
        name="in_proj",
    )(x, mod, mod, g1, w_r, gains, bsm, gmat, _lower_tri(tm), place, fconst)


def _flash_chunk(k_aug, qaug_scr, vt_of, x_scr, p_scr, m_scr, acc_scr, bias=None, visible=None):
    n_slots, ck = x_scr.shape[0], x_scr.shape[1]
    ones = jnp.ones((BF16_ROWS, ck), BF16)
    alphas = {}

    def logits(s):
        x = _nt_dot(k_aug(s), qaug_scr[s])
        if bias is not None:
            x = x + bias
        if visible is not None:
            x = jnp.where(visible, x, MASK_VALUE)
        x_scr[s] = x
        m_old = m_scr[s]
        m_new = jnp.maximum(m_old, jnp.max(x, axis=0, keepdims=True))
        alphas[s] = jnp.exp(m_old - m_new)
        m_scr[s] = m_new

    def probabilities(s):
        p_scr[s] = jnp.exp(x_scr[s] - m_scr[s]).astype(BF16)

    def values(s):
        acc_scr[s] = alphas[s] * acc_scr[s] + _dot(jnp.concatenate([vt_of(s), ones], axis=0), p_scr[s])

    for phase in (logits, probabilities, values):
        for s in range(n_slots):
            phase(s)


def _flash_scratch(n_slots, tq, ck):
    return [pltpu.VMEM((n_slots, tq, 2 * LANES), BF16), pltpu.VMEM((n_slots, ck, tq), F32),
            pltpu.VMEM((n_slots, ck, tq), BF16), pltpu.VMEM((n_slots, 1, tq), F32),
            pltpu.VMEM((n_slots, HEAD_DIM + BF16_ROWS, tq), F32)]


def _flash_init(m_scr, acc_scr):
    m_scr[...] = jnp.full(m_scr.shape, MASK_VALUE, F32)
    acc_scr[...] = jnp.zeros_like(acc_scr)


def _flash_output(o_ref, acc_scr):
    out_t = jnp.concatenate([acc_scr[s, :HEAD_DIM] / acc_scr[s, HEAD_DIM:HEAD_DIM + 1]
                             for s in range(acc_scr.shape[0])], axis=0)
    o_ref[...] = out_t.T.astype(BF16)


def _head_pick(tq, h):
    lane = lax.broadcasted_iota(I32, (tq, LANES), 1)
    return (lane >= HEAD_DIM) == (h % 2 == 1)


def _pair_lanes(h):
    return slice(LANES * (h // 2), LANES * (h // 2 + 1))


def _ordered_float(u):
    key = u ^ jnp.int32(INT_MIN)
    return pltpu.bitcast(key ^ ((key >> 31) & jnp.int32(0x7FFFFFFF)), F32)


BISECT_STEPS = 22
CUT_UNKNOWN, CUT_EXACT, CUT_RANKED = 0, 1, 2


def _selection_bias(score_scr, bias_scr, tri_ref, n_chunks, n_beyond, valid_fn, *, ck, tq, k_top, try_bisect=None,
                    bounds=None):
    n_acc = 4

    def partials(blk, reduce):
        return reduce(blk.reshape(ck // (8 * n_acc), n_acc, 8, tq), axis=0)

    def finish(acc, reduce):
        return reduce(reduce(acc, axis=0), axis=0, keepdims=True)

    def count(cmp):
        def body(c, acc):
            c0 = pl.multiple_of(c * ck, ck)
            return acc + partials(jnp.where(cmp(score_scr[pl.ds(c0, ck), :]), 1, 0), jnp.sum)
        return finish(lax.fori_loop(0, n_chunks, body, jnp.zeros((n_acc, 8, tq), I32)), jnp.sum)

    def reaching(x):
        return count(lambda blk: blk >= x) + jnp.where(x <= MASK_VALUE, n_beyond, 0)

    def exact_cut(thr):
        def body(c, carry):
            c0 = pl.multiple_of(c * ck, ck)
            keep = jnp.where(score_scr[pl.ds(c0, ck), :] >= thr, 0.0, MASK_VALUE)
            bias_scr[pl.ds(c0, ck), :] = jnp.where(valid_fn(c0), keep, MASK_VALUE)
            return carry
        lax.fori_loop(0, n_chunks, body, 0)

    def ranked_cut(thr):
        n_greater = count(lambda blk: blk > thr) + jnp.where(thr < MASK_VALUE, n_beyond, 0)
        ties_wanted = (k_top - n_greater).astype(F32)

        def body(c, ties_before):
            c0 = pl.multiple_of(c * ck, ck)
            blk = score_scr[pl.ds(c0, ck), :]
            tie = blk == thr
            rank = _dot(tri_ref[...], jnp.where(tie, 1.0, 0.0).astype(BF16)) + ties_before
            keep = jnp.where(blk > thr, 0.0,
                             jnp.where(tie, jnp.where(rank <= ties_wanted, 0.0, MASK_VALUE), MASK_VALUE))
            bias_scr[pl.ds(c0, ck), :] = jnp.where(valid_fn(c0), keep, MASK_VALUE)
            return rank[ck - 1:ck, :]
        lax.fori_loop(0, n_chunks, body, jnp.zeros((1, tq), F32))

    def bitwise():
        def bit_step(i, carry):
            code, reached = carry
            cand_code = code | lax.shift_left(jnp.int32(1), 31 - i)
            cnt = reaching(_ordered_float(cand_code))
            take = cnt >= k_top
            return jnp.where(take, cand_code, code), jnp.where(take, cnt, reached)

        code, reached = lax.fori_loop(0, 32, bit_step, (jnp.zeros((1, tq), I32), jnp.full((1, tq), k_top, I32)))
        thr = _ordered_float(code)
        lax.cond(jnp.max(reached) > k_top, lambda: ranked_cut(thr), lambda: exact_cut(thr))

    if try_bisect is None:
        bitwise()
        return

    def unsettled(reached):
        return jnp.max(jnp.abs(reached - k_top)) > 0

    def bisect():
        lo, top = bounds

        def halve(_, state):
            lo, hi, reached = state
            mid = 0.5 * lo + 0.5 * hi
            cnt = reaching(mid)
            up = cnt >= k_top
            return jnp.where(up, mid, lo), jnp.where(up, hi, mid), jnp.where(up, cnt, reached)

        above = top + (jnp.maximum(top - lo, jnp.abs(top)) * 2.0 ** -10 + 1e-30)
        lo, hi, reached = lax.fori_loop(0, BISECT_STEPS, halve, (lo, above, jnp.full((1, tq), k_top + 1, I32)))

        def tied():
            def below(c, acc):
                c0 = pl.multiple_of(c * ck, ck)
                blk = score_scr[pl.ds(c0, ck), :]
                return jnp.maximum(acc, partials(jnp.where(blk < hi, blk, -jnp.inf), jnp.max))
            top = finish(lax.fori_loop(0, n_chunks, below, jnp.full((n_acc, 8, tq), -jnp.inf, F32)), jnp.max)
            settled = reached == k_top
            confirmed = jnp.min(jnp.where(settled | (reaching(top) >= k_top), 1, 0)) > 0
            return jnp.where(settled, lo, top), jnp.where(confirmed, CUT_RANKED, CUT_UNKNOWN)

        return lax.cond(unsettled(reached), tied, lambda: (lo, jnp.int32(CUT_EXACT)))

    thr, cut = lax.cond(try_bisect, bisect, lambda: (jnp.zeros((1, tq), F32), jnp.int32(CUT_UNKNOWN)))
    lax.switch(cut, [bitwise, lambda: exact_cut(thr), lambda: ranked_cut(thr)])


POS_SPLIT = 256


def _alibi_lanes(pos, slope=None):
    lane = lax.broadcasted_iota(I32, pos.shape, 1)
    lo = pos & (POS_SPLIT - 1)
    hi_f, lo_f = (pos - lo).astype(F32), lo.astype(F32)
    if slope is None:
        val = jnp.where(lane == 0, hi_f, jnp.where(lane == 1, lo_f, jnp.where(lane < 4, 1.0, 0.0)))
    else:
        val = jnp.where(lane < 2, slope, jnp.where(lane == 2, hi_f * -slope, jnp.where(lane == 3, lo_f * -slope, 0.0)))
    return val.astype(BF16)


def _dsa_kernel(qi_ref, ki2_ref, w_ref, qa_ref, ka_ref, vat_ref, tri_ref, o_ref,
                qm_scr, score_scr, bias_scr, kx_scr, qaug_scr, x_scr, p_scr, m_scr, acc_scr, *, tq, ck, k_top, seq):
    q0 = pl.program_id(1) * tq
    n_chunks = (q0 + tq + ck - 1) // ck
    n_beyond = seq - n_chunks * ck

    zeros = jnp.zeros((tq, LANES), BF16)
    qpos = lax.broadcasted_iota(I32, (tq, LANES), 0) + q0
    for h in range(N_HEADS):
        pick = _head_pick(tq, h)
        qm_scr[h] = jnp.where(pick, qi_ref[:, _pair_lanes(h)], zeros)
        qaug_scr[h] = jnp.concatenate([jnp.where(pick, qa_ref[:, _pair_lanes(h)], zeros),
                                       _alibi_lanes(qpos, ALIBI_SLOPES[h])], axis=1)

    w = w_ref[...] * QK_SCALE
    krow = lax.broadcasted_iota(I32, (ck, tq), 0)
    qcol = lax.broadcasted_iota(I32, (ck, tq), 1)

    def visible(c0):
        return (krow + c0) <= (qcol + q0)

    def score_chunk(c, carry):
        lo, hi = carry
        c0 = pl.multiple_of(c * ck, ck)
        kblk = ki2_ref[pl.ds(c0, ck), :]
        acc = jnp.zeros((ck, tq), F32)
        for h in range(N_HEADS):
            acc = acc + jnp.maximum(_nt_dot(kblk, qm_scr[h]), 0.0) * w[h:h + 1, :]
        seen = visible(c0)
        score_scr[pl.ds(c0, ck), :] = jnp.where(seen, acc, MASK_VALUE)
        lo = jnp.minimum(lo, jnp.min(jnp.where(seen, acc, jnp.inf).reshape(ck // 8, 8, tq), axis=0))
        hi = jnp.maximum(hi, jnp.max(jnp.where(seen, acc, MASK_VALUE).reshape(ck // 8, 8, tq), axis=0))
        return lo, hi

    lo, hi = lax.fori_loop(0, n_chunks, score_chunk,
                           (jnp.full((8, tq), jnp.inf, F32), jnp.full((8, tq), -jnp.inf, F32)))
    bounds = (jnp.min(lo, axis=0, keepdims=True), jnp.max(hi, axis=0, keepdims=True))

    _selection_bias(score_scr, bias_scr, tri_ref, n_chunks, n_beyond, visible, ck=ck, tq=tq, k_top=k_top,
                    try_bisect=q0 >= k_top, bounds=bounds)

    _flash_init(m_scr, acc_scr)
    kpos0 = lax.broadcasted_iota(I32, (ck, LANES), 0)

    def attend_chunk(c, carry):
        c0 = pl.multiple_of(c * ck, ck)
        kx_scr[...] = _alibi_lanes(kpos0 + c0)
        k_aug = lambda h: jnp.concatenate([ka_ref[pl.ds(c0, ck), _pair_lanes(h)], kx_scr[...]], axis=1)
        _flash_chunk(k_aug, qaug_scr, lambda h: vat_ref[c, HEAD_DIM * h:HEAD_DIM * (h + 1), :],
                     x_scr, p_scr, m_scr, acc_scr, bias=bias_scr[pl.ds(c0, ck), :])
        return carry

    lax.fori_loop(0, n_chunks, attend_chunk, 0)
    _flash_output(o_ref, acc_scr)


def _dsa_prompt(qi, ki2, small_t, qa, ka, va_t, tri, tq, ck):
    n_b, seq, _ = qi.shape
    nblk = seq // tq
    kern = functools.partial(_dsa_kernel, tq=tq, ck=ck, k_top=_topk_count(seq), seq=seq)
    return pl.pallas_call(
        kern,
        out_shape=jax.ShapeDtypeStruct((n_b, seq, WIDTH), BF16),
        grid=(n_b, nblk),
        in_specs=[pl.BlockSpec((None, tq, WIDTH), lambda b, i: (b, i, 0)),
                  pl.BlockSpec((None, seq, LANES), lambda b, i: (b, 0, 0)),
                  pl.BlockSpec((None, N_HEADS, tq), lambda b, i: (b, SM_WI // N_HEADS, i)),
                  pl.BlockSpec((None, tq, WIDTH), lambda b, i: (b, i, 0)),
                  pl.BlockSpec((None, seq, WIDTH), lambda b, i: (b, 0, 0)),
                  pl.BlockSpec((None, seq // ck, WIDTH, ck), lambda b, i: (b, 0, 0, 0)),
                  _const_spec((ck, ck))],
        out_specs=pl.BlockSpec((None, tq, WIDTH), lambda b, i: (b, i, 0)),
        scratch_shapes=[pltpu.VMEM((N_HEADS, tq, LANES), BF16), pltpu.VMEM((seq, tq), F32),
                        pltpu.VMEM((seq, tq), F32), pltpu.VMEM((ck, LANES), BF16)]
                       + _flash_scratch(N_HEADS, tq, ck),
        compiler_params=_params("arbitrary", "arbitrary"),
        name="dsa_prompt",
    )(qi, ki2, small_t, qa, ka, va_t, tri)


def _head_diag(n_rows):
    lane = lax.broadcasted_iota(I32, (n_rows, WIDTH), 1)
    sub = lax.broadcasted_iota(I32, (n_rows, WIDTH), 0)
    return (lane >= sub * HEAD_DIM) & (lane < (sub + 1) * HEAD_DIM)


def _sidx_kernel(pt_ref, q_ref, w_ref, knew_ref, *rest, n_pages, page):
    page_refs, o_ref = rest[:n_pages], rest[n_pages]
    q = q_ref[...]
    qp = jnp.concatenate([q, jnp.zeros((BF16_ROWS - N_HEADS, HEAD_DIM), F32)], axis=0).astype(BF16)
    w = w_ref[...] * QK_SCALE
    for p in range(n_pages):
        s = _dot(qp, page_refs[p][...].astype(BF16))[:N_HEADS]
        o_ref[:, page * p:page * (p + 1)] = jnp.sum(jnp.maximum(s, 0.0) * w, axis=0, keepdims=True)
    knew = knew_ref[...].astype(BF16).astype(F32)
    s_new = jnp.sum(q * knew, axis=1, keepdims=True)
    sc_new = jnp.sum(jnp.maximum(s_new, 0.0) * w, axis=0, keepdims=True)
    lane = lax.broadcasted_iota(I32, (1, LANES), 1)
    o_ref[:, page * n_pages:] = jnp.where(lane == 0, sc_new, 0.0)


def _sample_index_scores(layer, page_table, cache_k_idx, qi_s, wi_s, ki_s):
    n_db, n_pages = page_table.shape
    page = cache_k_idx.shape[2]
    l_pad = n_pages * page + LANES
    page_spec = lambda p: pl.BlockSpec((None, None, HEAD_DIM, page), lambda b, pt: (layer, pt[b, p], 0, 0))
    k_idx_t = jnp.swapaxes(cache_k_idx, 2, 3)
    grid_spec = pltpu.PrefetchScalarGridSpec(
        num_scalar_prefetch=1,
        grid=(n_db,),
        in_specs=[pl.BlockSpec((None, N_HEADS, HEAD_DIM), lambda b, pt: (b, 0, 0)),
                  pl.BlockSpec((None, N_HEADS, 1), lambda b, pt: (b, 0, 0)),
                  pl.BlockSpec((None, 1, HEAD_DIM), lambda b, pt: (b, 0, 0))]
                 + [page_spec(p) for p in range(n_pages)],
        out_specs=pl.BlockSpec((None, 1, l_pad), lambda b, pt: (b, 0, 0)),
    )
    out = pl.pallas_call(
        functools.partial(_sidx_kernel, n_pages=n_pages, page=page),
        out_shape=jax.ShapeDtypeStruct((n_db, 1, l_pad), F32),
        grid_spec=grid_spec,
        compiler_params=_params("arbitrary"),
        name="sample_index_scores",
    )(page_table, qi_s.astype(F32).reshape(n_db, N_HEADS, HEAD_DIM), wi_s.reshape(n_db, N_HEADS, 1),
      ki_s.reshape(n_db, 1, HEAD_DIM), *([k_idx_t] * n_pages))
    return out.reshape(n_db, l_pad)


def _ssel_kernel(s_ref, tri_ref, o_ref, score_scr, *, n_keys, ck, k_top):
    l_pad, tq = s_ref.shape
    krow = lax.broadcasted_iota(I32, (ck, tq), 0)

    def valid(c0):
        return (krow + c0) < n_keys

    def drop_padding(c, carry):
        c0 = pl.multiple_of(c * ck, ck)
        score_scr[pl.ds(c0, ck), :] = jnp.where(valid(c0), s_ref[pl.ds(c0, ck), :], -jnp.inf)
        return carry

    lax.fori_loop(0, l_pad // ck, drop_padding, 0)
    _selection_bias(score_scr, o_ref, tri_ref, l_pad // ck, 0, valid, ck=ck, tq=tq, k_top=k_top)


def _sample_selection(scores_t, tri, n_keys):
    l_pad, n_db = scores_t.shape
    return pl.pallas_call(
        functools.partial(_ssel_kernel, n_keys=n_keys, ck=LANES, k_top=_topk_count(n_keys)),
        out_shape=jax.ShapeDtypeStruct((l_pad, n_db), F32),
        grid=(1,),
        in_specs=[pl.BlockSpec((l_pad, n_db), lambda i: (0, 0)), pl.BlockSpec((LANES, LANES), lambda i: (0, 0))],
        out_specs=pl.BlockSpec((l_pad, n_db), lambda i: (0, 0)),
        scratch_shapes=[pltpu.VMEM((l_pad, n_db), F32)],
        compiler_params=_params("arbitrary"),
        name="sample_selection",
    )(scores_t, tri)


def _paged_attend(q_ref, knew_ref, vnew_ref, k_pages, v_pages, logit_terms, o_ref):
    n_pages, page = len(k_pages), k_pages[0].shape[-1]
    diag = _head_diag(N_HEADS)
    qs = jnp.where(_head_diag(BF16_ROWS), jnp.broadcast_to(q_ref[...], (BF16_ROWS, WIDTH)), 0.0)
    qb = qs.astype(BF16)
    s = jnp.concatenate([_dot(qb, k_pages[p][...].astype(BF16))[:N_HEADS] for p in range(n_pages)], axis=1)
    s_new = jnp.sum(qs[:N_HEADS] * knew_ref[...].astype(BF16).astype(F32), axis=1, keepdims=True)
    x, x_new = logit_terms(s, s_new)
    m = jnp.maximum(jnp.max(x, axis=1, keepdims=True), x_new)
    p = jnp.exp(x - m)
    p_new = jnp.exp(x_new - m)
    denom = jnp.sum(p, axis=1, keepdims=True) + p_new
    pb = jnp.concatenate([p, jnp.zeros_like(p)], axis=0).astype(BF16)
    acc = p_new.astype(BF16).astype(F32) * vnew_ref[...].astype(BF16).astype(F32)
    for q in range(n_pages):
        acc = acc + _nt_dot(pb[:, page * q:page * (q + 1)], v_pages[q][...].astype(BF16))[:N_HEADS]
    o_ref[...] = jnp.sum(jnp.where(diag, acc / denom, 0.0), axis=0, keepdims=True)


def _sattn_kernel(pt_ref, qa_ref, qb_ref, kan_ref, van_ref, kbn_ref, vbn_ref, lfn_ref, bias_ref, slope_ref, *rest,
                  n_pages):
    ka_p, va_p, kb_p, vb_p, lf_p = (rest[g * n_pages:(g + 1) * n_pages] for g in range(5))
    oa_ref, ob_ref = rest[5 * n_pages:]
    past = n_pages * ka_p[0].shape[-1]

    def alibi_and_mask(s, s_new):
        rel = (lax.broadcasted_iota(I32, (1, past), 1) - past).astype(F32)
        bias = bias_ref[...]
        return s + slope_ref[...] * rel + bias[:, :past], s_new + bias[:, past:past + 1]

    _paged_attend(qa_ref, kan_ref, van_ref, ka_p, va_p, alibi_and_mask, oa_ref)

    lf = jnp.concatenate([lf_p[p][...] for p in range(n_pages)], axis=1)
    idx = lax.broadcasted_iota(I32, lf.shape, 1)
    d = 1
    while d < past:
        lf = lf + jnp.where(idx >= d, pltpu.roll(lf, d, 1), 0.0)
        d *= 2
    f_new = lf[:, past - 1:past] + lfn_ref[...]
    _paged_attend(qb_ref, kbn_ref, vbn_ref, kb_p, vb_p, lambda s, s_new: (s + (f_new - lf), s_new), ob_ref)


def _sample_attention(layer, page_table, caches, logf_t, qa_s, qb_s, new_rows, logf_new, bias):
    cache_k_a, cache_v_a, cache_k_b, cache_v_b = caches
    n_db, n_pages = page_table.shape
    n_pool, page = cache_k_a.shape[1], cache_k_a.shape[2]
    l_pad = bias.shape[1]
    flat = lambda c: jnp.transpose(c, (0, 1, 3, 4, 2)).reshape(c.shape[0], n_pool, WIDTH, page)
    row = lambda width: pl.BlockSpec((None, 1, width), lambda b, pt: (b, 0, 0))
    kv_spec = lambda p: pl.BlockSpec((None, None, WIDTH, page), lambda b, pt: (layer, pt[b, p], 0, 0))
    lf_spec = lambda p: pl.BlockSpec((None, N_HEADS, page), lambda b, pt: (pt[b, p], 0, 0))
    pages = range(n_pages)
    grid_spec = pltpu.PrefetchScalarGridSpec(
        num_scalar_prefetch=1,
        grid=(n_db,),
        in_specs=[row(WIDTH), row(WIDTH), row(WIDTH), row(WIDTH), row(WIDTH), row(WIDTH),
                  pl.BlockSpec((None, N_HEADS, 1), lambda b, pt: (b, 0, 0)), row(l_pad),
                  pl.BlockSpec((N_HEADS, 1), lambda b, pt: (0, 0))]
                 + [kv_spec(p) for p in pages] * 4 + [lf_spec(p) for p in pages],
        out_specs=(row(WIDTH), row(WIDTH)),
    )
    r3 = lambda a: a.reshape(n_db, 1, a.shape[-1])
    slopes = jnp.asarray(ALIBI_SLOPES, F32).reshape(N_HEADS, 1)
    return pl.pallas_call(
        functools.partial(_sattn_kernel, n_pages=n_pages),
        out_shape=(jax.ShapeDtypeStruct((n_db, 1, WIDTH), F32), jax.ShapeDtypeStruct((n_db, 1, WIDTH), F32)),
        grid_spec=grid_spec,
        compiler_params=_params("arbitrary"),
        name="sample_attention",
    )(page_table, r3(qa_s.astype(F32)), r3(qb_s.astype(F32)), *[r3(a) for a in new_rows],
      logf_new.reshape(n_db, N_HEADS, 1), r3(bias), slopes,
      *([flat(cache_k_a)] * n_pages), *([flat(cache_v_a)] * n_pages), *([flat(cache_k_b)] * n_pages),
      *([flat(cache_v_b)] * n_pages), *([logf_t] * n_pages))


def _fox_kernel(q_ref, qx_ref, k_ref, kx_ref, vt_ref, o_ref, qaug_scr, x_scr, p_scr, m_scr, acc_scr, *, tq, ck):
    q0 = pl.program_id(1) * tq
    n_chunks = (q0 + tq + ck - 1) // ck
    lane = lax.broadcasted_iota(I32, (tq, LANES), 1)
    zeros = jnp.zeros((tq, LANES), BF16)
    qx = qx_ref[...]
    for h in range(N_HEADS):
        own = (lane >= FX * h) & (lane < FX * (h + 1))
        qaug_scr[h] = jnp.concatenate([jnp.where(_head_pick(tq, h), q_ref[:, _pair_lanes(h)], zeros),
                                       jnp.where(own, qx, zeros)], axis=1)
    _flash_init(m_scr, acc_scr)
    krow = lax.broadcasted_iota(I32, (ck, tq), 0)
    qcol = lax.broadcasted_iota(I32, (ck, tq), 1)

    def chunk(c, masked):
        c0 = pl.multiple_of(c * ck, ck)
        k_aug = lambda h: jnp.concatenate([k_ref[pl.ds(c0, ck), _pair_lanes(h)], kx_ref[pl.ds(c0, ck), :]], axis=1)
        visible = ((krow + c0) <= (qcol + q0)) if masked else None
        _flash_chunk(k_aug, qaug_scr, lambda h: vt_ref[c, HEAD_DIM * h:HEAD_DIM * (h + 1), :],
                     x_scr, p_scr, m_scr, acc_scr, visible=visible)

    def full_chunk(c, carry):
        chunk(c, False)
        return carry

    lax.fori_loop(0, n_chunks - 1, full_chunk, 0)
    chunk(n_chunks - 1, True)
    _flash_output(o_ref, acc_scr)


def _fox_prompt(qb, qx, kb, kx, vb_t, tq, ck):
    n_b, seq, _ = qb.shape
    per_q = lambda width: pl.BlockSpec((None, tq, width), lambda b, i: (b, i, 0))
    whole = lambda width: pl.BlockSpec((None, seq, width), lambda b, i: (b, 0, 0))
    return pl.pallas_call(
        functools.partial(_fox_kernel, tq=tq, ck=ck),
        out_shape=jax.ShapeDtypeStruct((n_b, seq, WIDTH), BF16),
        grid=(n_b, seq // tq),
        in_specs=[per_q(WIDTH), per_q(LANES), whole(WIDTH), whole(LANES),
                  pl.BlockSpec((None, seq // ck, WIDTH, ck), lambda b, i: (b, 0, 0, 0))],
        out_specs=per_q(WIDTH),
        scratch_shapes=_flash_scratch(N_HEADS, tq, ck),
        compiler_params=_params("arbitrary", "arbitrary"),
        name="fox_prompt",
    )(qb, qx, kb, kx, vb_t)


def _post_kernel(x_ref, oa_ref, ob_ref, sga_ref, sgb_ref, gate1_ref, shift2_ref, scale2_ref, gate2_ref, g2_ref,
                 woa_ref, wob_ref, wout_ref, wup_ref, wdn_ref, y_ref, *, ff_chunk):
    merged = (sga_ref[...].astype(F32) * _dot(oa_ref[...], woa_ref[...])
              + sgb_ref[...].astype(F32) * _dot(ob_ref[...], wob_ref[...]))
    x1 = x_ref[...] + gate1_ref[...] * _dot(merged.astype(BF16), wout_ref[...])
    h2 = x1 * lax.rsqrt(jnp.mean(x1 * x1, axis=-1, keepdims=True) + EPS) * g2_ref[...]
    h2 = (h2 * (1.0 + scale2_ref[...]) + shift2_ref[...]).astype(BF16)
    mlp = jnp.zeros(x1.shape, F32)
    for c in range(0, D_FF, ff_chunk):
        u = jnp.maximum(_dot(h2, wup_ref[:, c:c + ff_chunk]), 0.0)
        mlp = mlp + _dot((u * u).astype(BF16), wdn_ref[c:c + ff_chunk, :])
    y_ref[...] = x1 + gate2_ref[...] * mlp


def _post(x, oa, ob, sga, sgb, mod, g2, woa, wob, wout, wup, wdn, tm):
    n_g, seq, _ = x.shape
    n_r = mod.shape[1]
    rb = 1 if n_r == 1 else tm
    row = lambda width: pl.BlockSpec((None, tm, width), lambda g, i: (g, i, 0))
    modspec = lambda j: pl.BlockSpec((None, rb, D_MODEL), lambda g, i: (g, i if n_r > 1 else 0, j))
    return pl.pallas_call(
        functools.partial(_post_kernel, ff_chunk=1024),
        out_shape=jax.ShapeDtypeStruct((n_g, seq, D_MODEL), F32),
        grid=(n_g, seq // tm),
        in_specs=[row(D_MODEL), row(WIDTH), row(WIDTH), row(D_MODEL), row(D_MODEL),
                  modspec(2), modspec(3), modspec(4), modspec(5), _const_spec((1, D_MODEL)),
                  _const_spec((WIDTH, D_MODEL)), _const_spec((WIDTH, D_MODEL)), _const_spec((D_MODEL, D_MODEL)),
                  _const_spec((D_MODEL, D_FF)), _const_spec((D_FF, D_MODEL))],
        out_specs=row(D_MODEL),
        compiler_params=_params("arbitrary", "arbitrary"),
        name="merge_out_mlp",
    )(x, oa, ob, sga, sgb, mod, mod, mod, mod, g2, woa, wob, wout, wup, wdn)


def _lower_tri(n):
    return jnp.asarray(np.tril(np.ones((n, n), np.float32)), BF16)


def _layer(l, x_p, x_s, caches, cache_k_idx, logf_t, page_table, mod_p, mod_s, lw, tiles):
    (g_norm1, w_r, gains, bsm, gmat, woa, wob, wout, g_norm2, wup, wdn) = lw
    tm, tq = tiles
    n_b, seq, _ = x_p.shape
    n_db = x_s.shape[0]

    (ka_p, va_p, kb_p, vb_p, small_p, qa_bf, ka_bf, vat_bf, qb_bf, kb_bf, vbt_bf, qi_bf, ki2_bf, smallt_p,
     sga_p, sgb_p, kx_b, qx_b) = _in_proj(x_p, mod_p, g_norm1, w_r, gains, bsm, gmat, tm)
    xs = x_s.reshape(1, n_db, D_MODEL)
    (ka_s, va_s, kb_s, vb_s, small_s, qa_s, _, _, qb_s, _, _, qi_s, _, _, sga_s, sgb_s, _, _) = _in_proj(
        xs, mod_s, g_norm1, w_r, gains, bsm, gmat, n_db)
    small_s2 = small_s[0]
    ki_s, wi_s = small_s2[:, :SM_WI], small_s2[:, SM_WI:SM_LOGF]
    logf_s = small_s2[:, SM_LOGF:SM_LOGF + N_HEADS]

    n_keys = page_table.shape[1] * cache_k_idx.shape[2] + 1
    scores = _sample_index_scores(l, page_table, cache_k_idx, qi_s[0], wi_s, ki_s)
    bias_t = _sample_selection(scores.T, _lower_tri(LANES), n_keys)

    oa_s, ob_s = _sample_attention(l, page_table, caches, logf_t, qa_s[0], qb_s[0],
                                   (ka_s[0], va_s[0], kb_s[0], vb_s[0]), logf_s, bias_t.T)
    ob_p = _fox_prompt(qb_bf, qx_b, kb_bf, kx_b, vbt_bf, tq, tm)
    oa_p = _dsa_prompt(qi_bf, ki2_bf, smallt_p, qa_bf, ka_bf, vat_bf, _lower_tri(tm), tq, tm)

    y_p = _post(x_p, oa_p, ob_p, sga_p, sgb_p, mod_p, g_norm2, woa, wob, wout, wup, wdn, tm)
    y_s = _post(xs, oa_s.reshape(1, n_db, WIDTH).astype(BF16), ob_s.reshape(1, n_db, WIDTH).astype(BF16),
                sga_s, sgb_s, mod_s, g_norm2, woa, wob, wout, wup, wdn, n_db)

    heads = lambda a, lead: a.reshape(*lead, N_HEADS, HEAD_DIM)
    lp, ls = (n_b, seq), (n_db, 1)
    from_t = lambda lo, hi: jnp.swapaxes(smallt_p[:, lo:hi, :], 1, 2)
    state_p = (heads(ka_p, lp), heads(va_p, lp), from_t(0, SM_WI), heads(kb_p, lp), heads(vb_p, lp),
               from_t(SM_LOGF, SM_LOGF + N_HEADS))
    state_s = (heads(ka_s[0], ls), heads(va_s[0], ls), ki_s.reshape(n_db, 1, HEAD_DIM), heads(kb_s[0], ls),
               heads(vb_s[0], ls), logf_s.reshape(n_db, 1, N_HEADS))
    return y_p, y_s.reshape(n_db, 1, D_MODEL), state_p, state_s


def kernel(x_prompt, x_sample, cache_k_a, cache_v_a, cache_k_idx, cache_k_b, cache_v_b, cache_logf_b, page_table,
           c_prompt, c_sample, w_ada, b_ada, g_norm1, w_in, b_forget, g_qa, g_ka, g_qb, g_kb, w_o_a, w_o_b, w_out,
           g_norm2, w_up, w_down):
    depth = w_in.shape[0]
    n_b, seq, _ = x_prompt.shape
    n_db, dec_seq, _ = x_sample.shape
    assert dec_seq == 1 and x_prompt.shape[-1] == D_MODEL
    assert cache_k_a.shape[3:] == (N_HEADS, HEAD_DIM) and cache_k_idx.shape[3] == HEAD_DIM
    tiles = (min(512, seq), min(256, seq))

    c_all = jnp.concatenate([c_prompt, c_sample], axis=0)
    pad = (-c_all.shape[0]) % 8
    c_all = jnp.pad(c_all, ((0, pad), (0, 0)))
    gmat = jnp.asarray(np.kron(np.eye(N_HEADS, dtype=np.float32),
                               np.full((HEAD_DIM, HEAD_DIM), 1.0 / HEAD_DIM, np.float32)), BF16)
    split = 7 * WIDTH
    y_p, y_s = x_prompt, x_sample.reshape(n_db, D_MODEL)
    states_p, states_s = [], []
    for l in range(depth):
        mod = _ada_mod(c_all, w_ada[l], b_ada[l])
        mod_p = mod[:n_b].reshape(n_b, 1, 6 * D_MODEL)
        mod_s = mod[n_b:n_b + n_db].reshape(1, n_db, 6 * D_MODEL)
        w = w_in[l]
        k_idx_cols = w[:, split:split + HEAD_DIM]
        w_r = jnp.concatenate(
            [w[:, :split], w[:, split:split + HEAD_DIM + 2 * N_HEADS],
             jnp.zeros((D_MODEL, LANES - HEAD_DIM - 2 * N_HEADS), w.dtype),
             k_idx_cols, k_idx_cols, w[:, split + HEAD_DIM + 2 * N_HEADS:]], axis=1).astype(BF16)
        gains = jnp.stack([jnp.tile(g[l], N_HEADS) for g in (g_qa, g_ka, g_qb, g_kb)])
        bsm = jnp.zeros((1, LANES), F32).at[0, SM_LOGF:SM_LOGF + N_HEADS].set(b_forget[l])
        lw = (g_norm1[l].reshape(1, D_MODEL), w_r, gains, bsm, gmat, w_o_a[l].astype(BF16), w_o_b[l].astype(BF16),
              w_out[l].astype(BF16), g_norm2[l].reshape(1, D_MODEL), w_up[l].astype(BF16), w_down[l].astype(BF16))
        logf_t = jnp.swapaxes(cache_logf_b[l], 1, 2)
        y_p, y_s, st_p, st_s = _layer(l, y_p, y_s, (cache_k_a, cache_v_a, cache_k_b, cache_v_b), cache_k_idx,
                                      logf_t, page_table, mod_p, mod_s, lw, tiles)
        y_s = y_s.reshape(n_db, D_MODEL)
        states_p.append(st_p)
        states_s.append(st_s)
    k_a_p, v_a_p, k_idx_p, k_b_p, v_b_p, logf_p = (jnp.stack(s) for s in zip(*states_p))
    k_a_s, v_a_s, k_idx_s, k_b_s, v_b_s, logf_s = (jnp.stack(s) for s in zip(*states_s))
    return (y_p, y_s.reshape(n_db, 1, D_MODEL), k_a_p, v_a_p, k_idx_p, k_b_p, v_b_p, logf_p,
            k_a_s, v_a_s, k_idx_s, k_b_s, v_b_s, logf_s)
```

```python
import functools

import numpy as np
import jax
import jax.numpy as jnp
from jax import lax
from jax.experimental import pallas as pl
from jax.experimental.pallas import tpu as pltpu

F32 = jnp.float32
BF16 = jnp.bfloat16
I32 = jnp.int32

D_MODEL = 1024
HEAD_DIM = 64
N_HEADS = 8
WIDTH = N_HEADS * HEAD_DIM
D_FF = 4 * D_MODEL
TOPK_MAX = 256
EPS = 1e-6
MASK_VALUE = -1e30
LANES = 128
BF16_ROWS = 16
QK_SCALE = HEAD_DIM ** -0.5
WI_SCALE = N_HEADS ** -0.5
ALIBI_SLOPES = tuple(2.0 ** (-8.0 * (h + 1) / N_HEADS) for h in range(N_HEADS))
INT_MIN = -(2 ** 31)
VMEM_LIMIT = 56 * 1024 * 1024

C_QA, C_KA, C_VA, C_QB, C_KB, C_VB, C_QI = (i * WIDTH for i in range(7))
C_SMALL = 7 * WIDTH
C_KI2 = C_SMALL + LANES
C_GA = C_KI2 + LANES
C_GB = C_GA + D_MODEL
C_TOTAL = C_GB + D_MODEL
SM_WI = HEAD_DIM
SM_LOGF = HEAD_DIM + N_HEADS


def _topk_count(n_keys):
    return max(1, min(TOPK_MAX, n_keys // 4))


def _const_spec(shape):
    zeros = (0,) * len(shape)
    return pl.BlockSpec(shape, lambda *_: zeros, pipeline_mode=pl.Buffered(1))


def _params(*sem):
    return pltpu.CompilerParams(dimension_semantics=sem, vmem_limit_bytes=VMEM_LIMIT)


def _nt_dot(a, b):
    return lax.dot_general(a, b, (((1,), (1,)), ((), ())), preferred_element_type=F32)


def _dot(a, b):
    return jnp.dot(a, b, preferred_element_type=F32)


def _split3(x):
    p1 = x.astype(BF16)
    r1 = x - p1.astype(F32)
    p2 = r1.astype(BF16)
    p3 = (r1 - p2.astype(F32)).astype(BF16)
    return p1, p2, p3


def _ada_kernel(c_ref, w_ref, b_ref, o_ref):
    c = c_ref[...]
    s = c * jax.nn.sigmoid(c)
    o_ref[...] = _dot(s.astype(BF16), w_ref[...].astype(BF16)) + b_ref[...]


def _ada_mod(c, w_ada, b_ada):
    rows = c.shape[0]
    n_out = w_ada.shape[1]
    tn = D_MODEL
    return pl.pallas_call(
        _ada_kernel,
        out_shape=jax.ShapeDtypeStruct((rows, n_out), F32),
        grid=(n_out // tn,),
        in_specs=[pl.BlockSpec((rows, D_MODEL), lambda j: (0, 0)),
                  pl.BlockSpec((D_MODEL, tn), lambda j: (0, j)),
                  pl.BlockSpec((1, tn), lambda j: (0, j))],
        out_specs=pl.BlockSpec((rows, tn), lambda j: (0, j)),
        compiler_params=_params("arbitrary"),
        name="ada_mod",
    )(c, w_ada, b_ada.reshape(1, n_out))


def _log_sigmoid(x):
    return jnp.minimum(x, 0.0) - jnp.log(1.0 + jnp.exp(-jnp.abs(x)))


FX = 6


def _forget_placement():
    place = np.zeros((3, LANES, 2 * LANES), np.float32)
    const = np.zeros((1, 2 * LANES), np.float32)
    for h in range(N_HEADS):
        for i in range(3):
            place[i, SM_LOGF + h, FX * h + i] = 1.0
            place[i, SM_LOGF + h, LANES + FX * h + 3 + i] = 1.0
            const[0, FX * h + 3 + i] = 1.0
            const[0, LANES + FX * h + i] = -1.0
    return jnp.asarray(place, BF16), jnp.asarray(const, F32)


def _in_proj_kernel(x_ref, shift_ref, scale_ref, g1_ref, w_ref, gains_ref, bsm_ref, gmat_ref,
                    tri_ref, place_ref, fconst_ref,
                    ka_ref, va_ref, kb_ref, vb_ref, small_ref,
                    qa_bf, ka_bf, vat_bf, qb_bf, kb_bf, vbt_bf, qi_bf, ki2_bf, smallt_ref, sga_ref, sgb_ref,
                    kx_ref, qx_ref, carry_ref):
    x = x_ref[...]
    h = x * lax.rsqrt(jnp.mean(x * x, axis=-1, keepdims=True) + EPS) * g1_ref[...]
    h = h * (1.0 + scale_ref[...]) + shift_ref[...]
    hb = h.astype(BF16)

    def seg(col, width=WIDTH):
        return _dot(hb, w_ref[:, col:col + width])

    def head_norm(z, gi):
        msq = _dot((z * z).astype(BF16), gmat_ref[...])
        return z * lax.rsqrt(msq + EPS) * gains_ref[gi:gi + 1, :]

    qa = head_norm(seg(C_QA), 0)
    qa_bf[...] = (qa * QK_SCALE).astype(BF16)
    ka = head_norm(seg(C_KA), 1)
    ka_ref[...] = ka
    ka_bf[...] = ka.astype(BF16)
    va = seg(C_VA)
    va_ref[...] = va
    vat_bf[...] = va.T.astype(BF16)
    qb = head_norm(seg(C_QB), 2)
    qb_bf[...] = (qb * QK_SCALE).astype(BF16)
    kb = head_norm(seg(C_KB), 3)
    kb_ref[...] = kb
    kb_bf[...] = kb.astype(BF16)
    vb = seg(C_VB)
    vb_ref[...] = vb
    vbt_bf[...] = vb.T.astype(BF16)
    qi_bf[...] = seg(C_QI).astype(BF16)
    ki2_bf[...] = seg(C_KI2, LANES).astype(BF16)

    zs = seg(C_SMALL, LANES)
    lane = lax.broadcasted_iota(I32, zs.shape, 1)
    logf = _log_sigmoid(zs + bsm_ref[...])
    small = jnp.where(lane < SM_WI, zs,
                      jnp.where(lane < SM_LOGF, zs * WI_SCALE,
                                jnp.where(lane < SM_LOGF + N_HEADS, logf, 0.0)))
    small_ref[...] = small
    smallt_ref[...] = small.T

    @pl.when(pl.program_id(1) == 0)
    def _():
        carry_ref[...] = jnp.zeros_like(carry_ref)

    tri = tri_ref[...]
    p1, p2, p3 = _split3(small)
    cs = _dot(tri, p1) + _dot(tri, p2) + _dot(tri, p3) + carry_ref[...]
    n = cs.shape[0]
    carry_ref[...] = cs[n - 1:n, :]
    f1, f2, f3 = _split3(cs)
    ext = _dot(f1, place_ref[0]) + _dot(f2, place_ref[1]) + _dot(f3, place_ref[2]) + fconst_ref[...]
    kx_ref[...] = ext[:, :LANES].astype(BF16)
    qx_ref[...] = ext[:, LANES:].astype(BF16)

    sga_ref[...] = jax.nn.sigmoid(seg(C_GA, D_MODEL)).astype(BF16)
    sgb_ref[...] = jax.nn.sigmoid(seg(C_GB, D_MODEL)).astype(BF16)


def _in_proj(x, mod, g1, w_r, gains, bsm, gmat, tm):
    n_g, seq, _ = x.shape
    n_r = mod.shape[1]
    rb = 1 if n_r == 1 else tm
    nblk = seq // tm
    row = lambda width: pl.BlockSpec((None, tm, width), lambda g, i: (g, i, 0))
    modspec = lambda j: pl.BlockSpec((None, rb, D_MODEL), lambda g, i: (g, i if n_r > 1 else 0, j))
    sds = lambda shape, dt: jax.ShapeDtypeStruct(shape, dt)
    out_shape = (
        sds((n_g, seq, WIDTH), F32), sds((n_g, seq, WIDTH), F32), sds((n_g, seq, WIDTH), F32),
        sds((n_g, seq, WIDTH), F32), sds((n_g, seq, LANES), F32),
        sds((n_g, seq, WIDTH), BF16), sds((n_g, seq, WIDTH), BF16), sds((n_g, nblk, WIDTH, tm), BF16),
        sds((n_g, seq, WIDTH), BF16), sds((n_g, seq, WIDTH), BF16), sds((n_g, nblk, WIDTH, tm), BF16),
        sds((n_g, seq, WIDTH), BF16), sds((n_g, seq, LANES), BF16), sds((n_g, LANES, seq), F32),
        sds((n_g, seq, D_MODEL), BF16), sds((n_g, seq, D_MODEL), BF16),
        sds((n_g, seq, LANES), BF16), sds((n_g, seq, LANES), BF16),
    )
    chunk_t = pl.BlockSpec((None, None, WIDTH, tm), lambda g, i: (g, i, 0, 0))
    out_specs = (
        row(WIDTH), row(WIDTH), row(WIDTH), row(WIDTH), row(LANES),
        row(WIDTH), row(WIDTH), chunk_t,
        row(WIDTH), row(WIDTH), chunk_t,
        row(WIDTH), row(LANES), pl.BlockSpec((None, LANES, tm), lambda g, i: (g, 0, i)),
        row(D_MODEL), row(D_MODEL),
        row(LANES), row(LANES),
    )
    place, fconst = _forget_placement()
    return pl.pallas_call(
        _in_proj_kernel,
        out_shape=out_shape,
        grid=(n_g, nblk),
        in_specs=[row(D_MODEL), modspec(0), modspec(1), _const_spec((1, D_MODEL)),
                  _const_spec((D_MODEL, C_TOTAL)), _const_spec((4, WIDTH)), _const_spec((1, LANES)),
                  _const_spec((WIDTH, WIDTH)), _const_spec((tm, tm)), _const_spec((3, LANES, 2 * LANES)),
                  _const_spec((1, 2 * LANES))],
        out_specs=out_specs,
        scratch_shapes=[pltpu.VMEM((1, LANES), F32)],
        compiler_params=_params("arbitrary", "arbitrary"),
        name="in_proj",
    )(x, mod, mod, g1, w_r, gains, bsm, gmat, _lower_tri(tm), place, fconst)


def _flash_chunk(k_aug, qaug_scr, vt_of, x_scr, p_scr, m_scr, acc_scr, bias=None, visible=None):
    n_slots, ck = x_scr.shape[0], x_scr.shape[1]
    ones = jnp.ones((BF16_ROWS, ck), BF16)
    alphas = {}

    def logits(s):
        x = _nt_dot(k_aug(s), qaug_scr[s])
        if bias is not None:
            x = x + bias
        if visible is not None:
            x = jnp.where(visible, x, MASK_VALUE)
        x_scr[s] = x
        m_old = m_scr[s]
        m_new = jnp.maximum(m_old, jnp.max(x, axis=0, keepdims=True))
        alphas[s] = jnp.exp(m_old - m_new)
        m_scr[s] = m_new

    def probabilities(s):
        p_scr[s] = jnp.exp(x_scr[s] - m_scr[s]).astype(BF16)

    def values(s):
        acc_scr[s] = alphas[s] * acc_scr[s] + _dot(jnp.concatenate([vt_of(s), ones], axis=0), p_scr[s])

    for phase in (logits, probabilities, values):
        for s in range(n_slots):
            phase(s)


def _flash_scratch(n_slots, tq, ck):
    return [pltpu.VMEM((n_slots, tq, 2 * LANES), BF16), pltpu.VMEM((n_slots, ck, tq), F32),
            pltpu.VMEM((n_slots, ck, tq), BF16), pltpu.VMEM((n_slots, 1, tq), F32),
            pltpu.VMEM((n_slots, HEAD_DIM + BF16_ROWS, tq), F32)]


def _flash_init(m_scr, acc_scr):
    m_scr[...] = jnp.full(m_scr.shape, MASK_VALUE, F32)
    acc_scr[...] = jnp.zeros_like(acc_scr)


def _flash_output(o_ref, acc_scr):
    out_t = jnp.concatenate([acc_scr[s, :HEAD_DIM] / acc_scr[s, HEAD_DIM:HEAD_DIM + 1]
                             for s in range(acc_scr.shape[0])], axis=0)
    o_ref[...] = out_t.T.astype(BF16)


def _head_pick(tq, h):
    lane = lax.broadcasted_iota(I32, (tq, LANES), 1)
    return (lane >= HEAD_DIM) == (h % 2 == 1)


def _pair_lanes(h):
    return slice(LANES * (h // 2), LANES * (h // 2 + 1))


def _ordered_float(u):
    key = u ^ jnp.int32(INT_MIN)
    return pltpu.bitcast(key ^ ((key >> 31) & jnp.int32(0x7FFFFFFF)), F32)


BISECT_STEPS = 22
CUT_UNKNOWN, CUT_EXACT, CUT_RANKED = 0, 1, 2


def _selection_bias(score_scr, bias_scr, tri_ref, n_chunks, n_beyond, valid_fn, *, ck, tq, k_top, try_bisect=None,
                    bounds=None):
    n_acc = 4

    def partials(blk, reduce):
        return reduce(blk.reshape(ck // (8 * n_acc), n_acc, 8, tq), axis=0)

    def finish(acc, reduce):
        return reduce(reduce(acc, axis=0), axis=0, keepdims=True)

    def count(cmp):
        def body(c, acc):
            c0 = pl.multiple_of(c * ck, ck)
            return acc + partials(jnp.where(cmp(score_scr[pl.ds(c0, ck), :]), 1, 0), jnp.sum)
        return finish(lax.fori_loop(0, n_chunks, body, jnp.zeros((n_acc, 8, tq), I32)), jnp.sum)

    def reaching(x):
        return count(lambda blk: blk >= x) + jnp.where(x <= MASK_VALUE, n_beyond, 0)

    def exact_cut(thr):
        def body(c, carry):
            c0 = pl.multiple_of(c * ck, ck)
            keep = jnp.where(score_scr[pl.ds(c0, ck), :] >= thr, 0.0, MASK_VALUE)
            bias_scr[pl.ds(c0, ck), :] = jnp.where(valid_fn(c0), keep, MASK_VALUE)
            return carry
        lax.fori_loop(0, n_chunks, body, 0)

    def ranked_cut(thr):
        n_greater = count(lambda blk: blk > thr) + jnp.where(thr < MASK_VALUE, n_beyond, 0)
        ties_wanted = (k_top - n_greater).astype(F32)

        def body(c, ties_before):
            c0 = pl.multiple_of(c * ck, ck)
            blk = score_scr[pl.ds(c0, ck), :]
            tie = blk == thr
            rank = _dot(tri_ref[...], jnp.where(tie, 1.0, 0.0).astype(BF16)) + ties_before
            keep = jnp.where(blk > thr, 0.0,
                             jnp.where(tie, jnp.where(rank <= ties_wanted, 0.0, MASK_VALUE), MASK_VALUE))
            bias_scr[pl.ds(c0, ck), :] = jnp.where(valid_fn(c0), keep, MASK_VALUE)
            return rank[ck - 1:ck, :]
        lax.fori_loop(0, n_chunks, body, jnp.zeros((1, tq), F32))

    def bitwise():
        def bit_step(i, carry):
            code, reached = carry
            cand_code = code | lax.shift_left(jnp.int32(1), 31 - i)
            cnt = reaching(_ordered_float(cand_code))
            take = cnt >= k_top
            return jnp.where(take, cand_code, code), jnp.where(take, cnt, reached)

        code, reached = lax.fori_loop(0, 32, bit_step, (jnp.zeros((1, tq), I32), jnp.full((1, tq), k_top, I32)))
        thr = _ordered_float(code)
        lax.cond(jnp.max(reached) > k_top, lambda: ranked_cut(thr), lambda: exact_cut(thr))

    if try_bisect is None:
        bitwise()
        return

    def unsettled(reached):
        return jnp.max(jnp.abs(reached - k_top)) > 0

    def bisect():
        lo, top = bounds

        def halve(_, state):
            lo, hi, reached = state
            mid = 0.5 * lo + 0.5 * hi
            cnt = reaching(mid)
            up = cnt >= k_top
            return jnp.where(up, mid, lo), jnp.where(up, hi, mid), jnp.where(up, cnt, reached)

        above = top + (jnp.maximum(top - lo, jnp.abs(top)) * 2.0 ** -10 + 1e-30)
        lo, hi, reached = lax.fori_loop(0, BISECT_STEPS, halve, (lo, above, jnp.full((1, tq), k_top + 1, I32)))

        def tied():
            def below(c, acc):
                c0 = pl.multiple_of(c * ck, ck)
                blk = score_scr[pl.ds(c0, ck), :]
                return jnp.maximum(acc, partials(jnp.where(blk < hi, blk, -jnp.inf), jnp.max))
            top = finish(lax.fori_loop(0, n_chunks, below, jnp.full((n_acc, 8, tq), -jnp.inf, F32)), jnp.max)
            settled = reached == k_top
            confirmed = jnp.min(jnp.where(settled | (reaching(top) >= k_top), 1, 0)) > 0
            return jnp.where(settled, lo, top), jnp.where(confirmed, CUT_RANKED, CUT_UNKNOWN)

        return lax.cond(unsettled(reached), tied, lambda: (lo, jnp.int32(CUT_EXACT)))

    thr, cut = lax.cond(try_bisect, bisect, lambda: (jnp.zeros((1, tq), F32), jnp.int32(CUT_UNKNOWN)))
    lax.switch(cut, [bitwise, lambda: exact_cut(thr), lambda: ranked_cut(thr)])


POS_SPLIT = 256


def _alibi_lanes(pos, slope=None):
    lane = lax.broadcasted_iota(I32, pos.shape, 1)
    lo = pos & (POS_SPLIT - 1)
    hi_f, lo_f = (pos - lo).astype(F32), lo.astype(F32)
    if slope is None:
        val = jnp.where(lane == 0, hi_f, jnp.where(lane == 1, lo_f, jnp.where(lane < 4, 1.0, 0.0)))
    else:
        val = jnp.where(lane < 2, slope, jnp.where(lane == 2, hi_f * -slope, jnp.where(lane == 3, lo_f * -slope, 0.0)))
    return val.astype(BF16)


def _dsa_kernel(qi_ref, ki2_ref, w_ref, qa_ref, ka_ref, vat_ref, tri_ref, o_ref,
                qm_scr, score_scr, bias_scr, kx_scr, qaug_scr, x_scr, p_scr, m_scr, acc_scr, *, tq, ck, k_top, seq):
    q0 = pl.program_id(1) * tq
    n_chunks = (q0 + tq + ck - 1) // ck
    n_beyond = seq - n_chunks * ck

    zeros = jnp.zeros((tq, LANES), BF16)
    qpos = lax.broadcasted_iota(I32, (tq, LANES), 0) + q0
    for h in range(N_HEADS):
        pick = _head_pick(tq, h)
        qm_scr[h] = jnp.where(pick, qi_ref[:, _pair_lanes(h)], zeros)
        qaug_scr[h] = jnp.concatenate([jnp.where(pick, qa_ref[:, _pair_lanes(h)], zeros),
                                       _alibi_lanes(qpos, ALIBI_SLOPES[h])], axis=1)

    w = w_ref[...] * QK_SCALE
    krow = lax.broadcasted_iota(I32, (ck, tq), 0)
    qcol = lax.broadcasted_iota(I32, (ck, tq), 1)

    def visible(c0):
        return (krow + c0) <= (qcol + q0)

    def score_chunk(c, carry):
        lo, hi = carry
        c0 = pl.multiple_of(c * ck, ck)
        kblk = ki2_ref[pl.ds(c0, ck), :]
        acc = jnp.zeros((ck, tq), F32)
        for h in range(N_HEADS):
            acc = acc + jnp.maximum(_nt_dot(kblk, qm_scr[h]), 0.0) * w[h:h + 1, :]
        seen = visible(c0)
        score_scr[pl.ds(c0, ck), :] = jnp.where(seen, acc, MASK_VALUE)
        lo = jnp.minimum(lo, jnp.min(jnp.where(seen, acc, jnp.inf).reshape(ck // 8, 8, tq), axis=0))
        hi = jnp.maximum(hi, jnp.max(jnp.where(seen, acc, MASK_VALUE).reshape(ck // 8, 8, tq), axis=0))
        return lo, hi

    lo, hi = lax.fori_loop(0, n_chunks, score_chunk,
                           (jnp.full((8, tq), jnp.inf, F32), jnp.full((8, tq), -jnp.inf, F32)))
    bounds = (jnp.min(lo, axis=0, keepdims=True), jnp.max(hi, axis=0, keepdims=True))

    _selection_bias(score_scr, bias_scr, tri_ref, n_chunks, n_beyond, visible, ck=ck, tq=tq, k_top=k_top,
                    try_bisect=q0 >= k_top, bounds=bounds)

    _flash_init(m_scr, acc_scr)
    kpos0 = lax.broadcasted_iota(I32, (ck, LANES), 0)

    def attend_chunk(c, carry):
        c0 = pl.multiple_of(c * ck, ck)
        kx_scr[...] = _alibi_lanes(kpos0 + c0)
        k_aug = lambda h: jnp.concatenate([ka_ref[pl.ds(c0, ck), _pair_lanes(h)], kx_scr[...]], axis=1)
        _flash_chunk(k_aug, qaug_scr, lambda h: vat_ref[c, HEAD_DIM * h:HEAD_DIM * (h + 1), :],
                     x_scr, p_scr, m_scr, acc_scr, bias=bias_scr[pl.ds(c0, ck), :])
        return carry

    lax.fori_loop(0, n_chunks, attend_chunk, 0)
    _flash_output(o_ref, acc_scr)


def _dsa_prompt(qi, ki2, small_t, qa, ka, va_t, tri, tq, ck):
    n_b, seq, _ = qi.shape
    nblk = seq // tq
    kern = functools.partial(_dsa_kernel, tq=tq, ck=ck, k_top=_topk_count(seq), seq=seq)
    return pl.pallas_call(
        kern,
        out_shape=jax.ShapeDtypeStruct((n_b, seq, WIDTH), BF16),
        grid=(n_b, nblk),
        in_specs=[pl.BlockSpec((None, tq, WIDTH), lambda b, i: (b, i, 0)),
                  pl.BlockSpec((None, seq, LANES), lambda b, i: (b, 0, 0)),
                  pl.BlockSpec((None, N_HEADS, tq), lambda b, i: (b, SM_WI // N_HEADS, i)),
                  pl.BlockSpec((None, tq, WIDTH), lambda b, i: (b, i, 0)),
                  pl.BlockSpec((None, seq, WIDTH), lambda b, i: (b, 0, 0)),
                  pl.BlockSpec((None, seq // ck, WIDTH, ck), lambda b, i: (b, 0, 0, 0)),
                  _const_spec((ck, ck))],
        out_specs=pl.BlockSpec((None, tq, WIDTH), lambda b, i: (b, i, 0)),
        scratch_shapes=[pltpu.VMEM((N_HEADS, tq, LANES), BF16), pltpu.VMEM((seq, tq), F32),
                        pltpu.VMEM((seq, tq), F32), pltpu.VMEM((ck, LANES), BF16)]
                       + _flash_scratch(N_HEADS, tq, ck),
        compiler_params=_params("arbitrary", "arbitrary"),
        name="dsa_prompt",
    )(qi, ki2, small_t, qa, ka, va_t, tri)


def _head_diag(n_rows):
    lane = lax.broadcasted_iota(I32, (n_rows, WIDTH), 1)
    sub = lax.broadcasted_iota(I32, (n_rows, WIDTH), 0)
    return (lane >= sub * HEAD_DIM) & (lane < (sub + 1) * HEAD_DIM)


def _sidx_kernel(pt_ref, q_ref, w_ref, knew_ref, *rest, n_seq, n_pages, page):
    page_refs, o_ref = rest[:n_seq * n_pages], rest[n_seq * n_pages]
    lane = lax.broadcasted_iota(I32, (1, LANES), 1)
    for g in range(n_seq):
        q = q_ref[g]
        qp = jnp.concatenate([q, jnp.zeros((BF16_ROWS - N_HEADS, HEAD_DIM), F32)], axis=0).astype(BF16)
        w = w_ref[g] * QK_SCALE
        for p in range(n_pages):
            s = _dot(qp, page_refs[g * n_pages + p][...].astype(BF16))[:N_HEADS]
            o_ref[g, :, page * p:page * (p + 1)] = jnp.sum(jnp.maximum(s, 0.0) * w, axis=0, keepdims=True)
        knew = knew_ref[g].astype(BF16).astype(F32)
        s_new = jnp.sum(q * knew, axis=1, keepdims=True)
        sc_new = jnp.sum(jnp.maximum(s_new, 0.0) * w, axis=0, keepdims=True)
        o_ref[g, :, page * n_pages:] = jnp.where(lane == 0, sc_new, 0.0)


def _sample_index_scores(layer, page_table, cache_k_idx, qi_s, wi_s, ki_s):
    n_db, n_pages = page_table.shape
    page = cache_k_idx.shape[2]
    l_pad = n_pages * page + LANES
    n_seq = next(g for g in (4, 2, 1) if n_db % g == 0)
    page_spec = lambda g, p: pl.BlockSpec((None, None, HEAD_DIM, page),
                                          lambda b, pt: (layer, pt[b * n_seq + g, p], 0, 0))
    k_idx_t = jnp.swapaxes(cache_k_idx, 2, 3)
    grid_spec = pltpu.PrefetchScalarGridSpec(
        num_scalar_prefetch=1,
        grid=(n_db // n_seq,),
        in_specs=[pl.BlockSpec((n_seq, N_HEADS, HEAD_DIM), lambda b, pt: (b, 0, 0)),
                  pl.BlockSpec((n_seq, N_HEADS, 1), lambda b, pt: (b, 0, 0)),
                  pl.BlockSpec((n_seq, 1, HEAD_DIM), lambda b, pt: (b, 0, 0))]
                 + [page_spec(g, p) for g in range(n_seq) for p in range(n_pages)],
        out_specs=pl.BlockSpec((n_seq, 1, l_pad), lambda b, pt: (b, 0, 0)),
    )
    out = pl.pallas_call(
        functools.partial(_sidx_kernel, n_seq=n_seq, n_pages=n_pages, page=page),
        out_shape=jax.ShapeDtypeStruct((n_db, 1, l_pad), F32),
        grid_spec=grid_spec,
        compiler_params=_params("arbitrary"),
        name="sample_index_scores",
    )(page_table, qi_s.astype(F32).reshape(n_db, N_HEADS, HEAD_DIM), wi_s.reshape(n_db, N_HEADS, 1),
      ki_s.reshape(n_db, 1, HEAD_DIM), *([k_idx_t] * (n_seq * n_pages)))
    return out.reshape(n_db, l_pad)


def _ssel_kernel(s_ref, tri_ref, o_ref, score_scr, *, n_keys, ck, k_top):
    l_pad, tq = s_ref.shape
    krow = lax.broadcasted_iota(I32, (ck, tq), 0)

    def valid(c0):
        return (krow + c0) < n_keys

    def drop_padding(c, carry):
        c0 = pl.multiple_of(c * ck, ck)
        score_scr[pl.ds(c0, ck), :] = jnp.where(valid(c0), s_ref[pl.ds(c0, ck), :], -jnp.inf)
        return carry

    lax.fori_loop(0, l_pad // ck, drop_padding, 0)
    _selection_bias(score_scr, o_ref, tri_ref, l_pad // ck, 0, valid, ck=ck, tq=tq, k_top=k_top)


def _sample_selection(scores_t, tri, n_keys):
    l_pad, n_db = scores_t.shape
    return pl.pallas_call(
        functools.partial(_ssel_kernel, n_keys=n_keys, ck=LANES, k_top=_topk_count(n_keys)),
        out_shape=jax.ShapeDtypeStruct((l_pad, n_db), F32),
        grid=(1,),
        in_specs=[pl.BlockSpec((l_pad, n_db), lambda i: (0, 0)), pl.BlockSpec((LANES, LANES), lambda i: (0, 0))],
        out_specs=pl.BlockSpec((l_pad, n_db), lambda i: (0, 0)),
        scratch_shapes=[pltpu.VMEM((l_pad, n_db), F32)],
        compiler_params=_params("arbitrary"),
        name="sample_selection",
    )(scores_t, tri)


def _paged_attend(q_ref, knew_ref, vnew_ref, k_pages, v_pages, logit_terms, o_ref):
    n_pages, page = len(k_pages), k_pages[0].shape[-1]
    diag = _head_diag(N_HEADS)
    qs = jnp.where(_head_diag(BF16_ROWS), jnp.broadcast_to(q_ref[...], (BF16_ROWS, WIDTH)), 0.0)
    qb = qs.astype(BF16)
    s = jnp.concatenate([_dot(qb, k_pages[p][...].astype(BF16))[:N_HEADS] for p in range(n_pages)], axis=1)
    s_new = jnp.sum(qs[:N_HEADS] * knew_ref[...].astype(BF16).astype(F32), axis=1, keepdims=True)
    x, x_new = logit_terms(s, s_new)
    m = jnp.maximum(jnp.max(x, axis=1, keepdims=True), x_new)
    p = jnp.exp(x - m)
    p_new = jnp.exp(x_new - m)
    denom = jnp.sum(p, axis=1, keepdims=True) + p_new
    pb = jnp.concatenate([p, jnp.zeros_like(p)], axis=0).astype(BF16)
    acc = p_new.astype(BF16).astype(F32) * vnew_ref[...].astype(BF16).astype(F32)
    for q in range(n_pages):
        acc = acc + _nt_dot(pb[:, page * q:page * (q + 1)], v_pages[q][...].astype(BF16))[:N_HEADS]
    o_ref[...] = jnp.sum(jnp.where(diag, acc / denom, 0.0), axis=0, keepdims=True)


def _sattn_kernel(pt_ref, qa_ref, qb_ref, kan_ref, van_ref, kbn_ref, vbn_ref, lfn_ref, bias_ref, slope_ref, *rest,
                  n_pages):
    ka_p, va_p, kb_p, vb_p, lf_p = (rest[g * n_pages:(g + 1) * n_pages] for g in range(5))
    oa_ref, ob_ref = rest[5 * n_pages:]
    past = n_pages * ka_p[0].shape[-1]

    def alibi_and_mask(s, s_new):
        rel = (lax.broadcasted_iota(I32, (1, past), 1) - past).astype(F32)
        bias = bias_ref[...]
        return s + slope_ref[...] * rel + bias[:, :past], s_new + bias[:, past:past + 1]

    _paged_attend(qa_ref, kan_ref, van_ref, ka_p, va_p, alibi_and_mask, oa_ref)

    lf = jnp.concatenate([lf_p[p][...] for p in range(n_pages)], axis=1)
    idx = lax.broadcasted_iota(I32, lf.shape, 1)
    d = 1
    while d < past:
        lf = lf + jnp.where(idx >= d, pltpu.roll(lf, d, 1), 0.0)
        d *= 2
    f_new = lf[:, past - 1:past] + lfn_ref[...]
    _paged_attend(qb_ref, kbn_ref, vbn_ref, kb_p, vb_p, lambda s, s_new: (s + (f_new - lf), s_new), ob_ref)


def _sample_attention(layer, page_table, caches, logf_t, qa_s, qb_s, new_rows, logf_new, bias):
    cache_k_a, cache_v_a, cache_k_b, cache_v_b = caches
    n_db, n_pages = page_table.shape
    n_pool, page = cache_k_a.shape[1], cache_k_a.shape[2]
    l_pad = bias.shape[1]
    flat = lambda c: jnp.transpose(c, (0, 1, 3, 4, 2)).reshape(c.shape[0], n_pool, WIDTH, page)
    row = lambda width: pl.BlockSpec((None, 1, width), lambda b, pt: (b, 0, 0))
    kv_spec = lambda p: pl.BlockSpec((None, None, WIDTH, page), lambda b, pt: (layer, pt[b, p], 0, 0))
    lf_spec = lambda p: pl.BlockSpec((None, N_HEADS, page), lambda b, pt: (pt[b, p], 0, 0))
    pages = range(n_pages)
    grid_spec = pltpu.PrefetchScalarGridSpec(
        num_scalar_prefetch=1,
        grid=(n_db,),
        in_specs=[row(WIDTH), row(WIDTH), row(WIDTH), row(WIDTH), row(WIDTH), row(WIDTH),
                  pl.BlockSpec((None, N_HEADS, 1), lambda b, pt: (b, 0, 0)), row(l_pad),
                  pl.BlockSpec((N_HEADS, 1), lambda b, pt: (0, 0))]
                 + [kv_spec(p) for p in pages] * 4 + [lf_spec(p) for p in pages],
        out_specs=(row(WIDTH), row(WIDTH)),
    )
    r3 = lambda a: a.reshape(n_db, 1, a.shape[-1])
    slopes = jnp.asarray(ALIBI_SLOPES, F32).reshape(N_HEADS, 1)
    return pl.pallas_call(
        functools.partial(_sattn_kernel, n_pages=n_pages),
        out_shape=(jax.ShapeDtypeStruct((n_db, 1, WIDTH), F32), jax.ShapeDtypeStruct((n_db, 1, WIDTH), F32)),
        grid_spec=grid_spec,
        compiler_params=_params("arbitrary"),
        name="sample_attention",
    )(page_table, r3(qa_s.astype(F32)), r3(qb_s.astype(F32)), *[r3(a) for a in new_rows],
      logf_new.reshape(n_db, N_HEADS, 1), r3(bias), slopes,
      *([flat(cache_k_a)] * n_pages), *([flat(cache_v_a)] * n_pages), *([flat(cache_k_b)] * n_pages),
      *([flat(cache_v_b)] * n_pages), *([logf_t] * n_pages))


def _fox_kernel(q_ref, qx_ref, k_ref, kx_ref, vt_ref, o_ref, qaug_scr, x_scr, p_scr, m_scr, acc_scr, *, tq, ck):
    q0 = pl.program_id(1) * tq
    n_chunks = (q0 + tq + ck - 1) // ck
    lane = lax.broadcasted_iota(I32, (tq, LANES), 1)
    zeros = jnp.zeros((tq, LANES), BF16)
    qx = qx_ref[...]
    for h in range(N_HEADS):
        own = (lane >= FX * h) & (lane < FX * (h + 1))
        qaug_scr[h] = jnp.concatenate([jnp.where(_head_pick(tq, h), q_ref[:, _pair_lanes(h)], zeros),
                                       jnp.where(own, qx, zeros)], axis=1)
    _flash_init(m_scr, acc_scr)
    krow = lax.broadcasted_iota(I32, (ck, tq), 0)
    qcol = lax.broadcasted_iota(I32, (ck, tq), 1)

    def chunk(c, masked):
        c0 = pl.multiple_of(c * ck, ck)
        k_aug = lambda h: jnp.concatenate([k_ref[pl.ds(c0, ck), _pair_lanes(h)], kx_ref[pl.ds(c0, ck), :]], axis=1)
        visible = ((krow + c0) <= (qcol + q0)) if masked else None
        _flash_chunk(k_aug, qaug_scr, lambda h: vt_ref[c, HEAD_DIM * h:HEAD_DIM * (h + 1), :],
                     x_scr, p_scr, m_scr, acc_scr, visible=visible)

    def full_chunk(c, carry):
        chunk(c, False)
        return carry

    lax.fori_loop(0, n_chunks - 1, full_chunk, 0)
    chunk(n_chunks - 1, True)
    _flash_output(o_ref, acc_scr)


def _fox_prompt(qb, qx, kb, kx, vb_t, tq, ck):
    n_b, seq, _ = qb.shape
    per_q = lambda width: pl.BlockSpec((None, tq, width), lambda b, i: (b, i, 0))
    whole = lambda width: pl.BlockSpec((None, seq, width), lambda b, i: (b, 0, 0))
    return pl.pallas_call(
        functools.partial(_fox_kernel, tq=tq, ck=ck),
        out_shape=jax.ShapeDtypeStruct((n_b, seq, WIDTH), BF16),
        grid=(n_b, seq // tq),
        in_specs=[per_q(WIDTH), per_q(LANES), whole(WIDTH), whole(LANES),
                  pl.BlockSpec((None, seq // ck, WIDTH, ck), lambda b, i: (b, 0, 0, 0))],
        out_specs=per_q(WIDTH),
        scratch_shapes=_flash_scratch(N_HEADS, tq, ck),
        compiler_params=_params("arbitrary", "arbitrary"),
        name="fox_prompt",
    )(qb, qx, kb, kx, vb_t)


def _post_kernel(x_ref, oa_ref, ob_ref, sga_ref, sgb_ref, gate1_ref, shift2_ref, scale2_ref, gate2_ref, g2_ref,
                 woa_ref, wob_ref, wout_ref, wup_ref, wdn_ref, y_ref, *, ff_chunk):
    merged = (sga_ref[...].astype(F32) * _dot(oa_ref[...], woa_ref[...])
              + sgb_ref[...].astype(F32) * _dot(ob_ref[...], wob_ref[...]))
    x1 = x_ref[...] + gate1_ref[...] * _dot(merged.astype(BF16), wout_ref[...])
    h2 = x1 * lax.rsqrt(jnp.mean(x1 * x1, axis=-1, keepdims=True) + EPS) * g2_ref[...]
    h2 = (h2 * (1.0 + scale2_ref[...]) + shift2_ref[...]).astype(BF16)
    mlp = jnp.zeros(x1.shape, F32)
    for c in range(0, D_FF, ff_chunk):
        u = jnp.maximum(_dot(h2, wup_ref[:, c:c + ff_chunk]), 0.0)
        mlp = mlp + _dot((u * u).astype(BF16), wdn_ref[c:c + ff_chunk, :])
    y_ref[...] = x1 + gate2_ref[...] * mlp


def _post(x, oa, ob, sga, sgb, mod, g2, woa, wob, wout, wup, wdn, tm):
    n_g, seq, _ = x.shape
    n_r = mod.shape[1]
    rb = 1 if n_r == 1 else tm
    row = lambda width: pl.BlockSpec((None, tm, width), lambda g, i: (g, i, 0))
    modspec = lambda j: pl.BlockSpec((None, rb, D_MODEL), lambda g, i: (g, i if n_r > 1 else 0, j))
    return pl.pallas_call(
        functools.partial(_post_kernel, ff_chunk=1024),
        out_shape=jax.ShapeDtypeStruct((n_g, seq, D_MODEL), F32),
        grid=(n_g, seq // tm),
        in_specs=[row(D_MODEL), row(WIDTH), row(WIDTH), row(D_MODEL), row(D_MODEL),
                  modspec(2), modspec(3), modspec(4), modspec(5), _const_spec((1, D_MODEL)),
                  _const_spec((WIDTH, D_MODEL)), _const_spec((WIDTH, D_MODEL)), _const_spec((D_MODEL, D_MODEL)),
                  _const_spec((D_MODEL, D_FF)), _const_spec((D_FF, D_MODEL))],
        out_specs=row(D_MODEL),
        compiler_params=_params("arbitrary", "arbitrary"),
        name="merge_out_mlp",
    )(x, oa, ob, sga, sgb, mod, mod, mod, mod, g2, woa, wob, wout, wup, wdn)


def _lower_tri(n):
    return jnp.asarray(np.tril(np.ones((n, n), np.float32)), BF16)


def _layer(l, x_p, x_s, caches, cache_k_idx, logf_t, page_table, mod_p, mod_s, lw, tiles):
    (g_norm1, w_r, gains, bsm, gmat, woa, wob, wout, g_norm2, wup, wdn) = lw
    tm, tq = tiles
    n_b, seq, _ = x_p.shape
    n_db = x_s.shape[0]

    (ka_p, va_p, kb_p, vb_p, small_p, qa_bf, ka_bf, vat_bf, qb_bf, kb_bf, vbt_bf, qi_bf, ki2_bf, smallt_p,
     sga_p, sgb_p, kx_b, qx_b) = _in_proj(x_p, mod_p, g_norm1, w_r, gains, bsm, gmat, tm)
    xs = x_s.reshape(1, n_db, D_MODEL)
    (ka_s, va_s, kb_s, vb_s, small_s, qa_s, _, _, qb_s, _, _, qi_s, _, _, sga_s, sgb_s, _, _) = _in_proj(
        xs, mod_s, g_norm1, w_r, gains, bsm, gmat, n_db)
    small_s2 = small_s[0]
    ki_s, wi_s = small_s2[:, :SM_WI], small_s2[:, SM_WI:SM_LOGF]
    logf_s = small_s2[:, SM_LOGF:SM_LOGF + N_HEADS]

    n_keys = page_table.shape[1] * cache_k_idx.shape[2] + 1
    scores = _sample_index_scores(l, page_table, cache_k_idx, qi_s[0], wi_s, ki_s)
    bias_t = _sample_selection(scores.T, _lower_tri(LANES), n_keys)

    oa_s, ob_s = _sample_attention(l, page_table, caches, logf_t, qa_s[0], qb_s[0],
                                   (ka_s[0], va_s[0], kb_s[0], vb_s[0]), logf_s, bias_t.T)
    ob_p = _fox_prompt(qb_bf, qx_b, kb_bf, kx_b, vbt_bf, tq, tm)
    oa_p = _dsa_prompt(qi_bf, ki2_bf, smallt_p, qa_bf, ka_bf, vat_bf, _lower_tri(tm), tq, tm)

    y_p = _post(x_p, oa_p, ob_p, sga_p, sgb_p, mod_p, g_norm2, woa, wob, wout, wup, wdn, tm)
    y_s = _post(xs, oa_s.reshape(1, n_db, WIDTH).astype(BF16), ob_s.reshape(1, n_db, WIDTH).astype(BF16),
                sga_s, sgb_s, mod_s, g_norm2, woa, wob, wout, wup, wdn, n_db)

    heads = lambda a, lead: a.reshape(*lead, N_HEADS, HEAD_DIM)
    lp, ls = (n_b, seq), (n_db, 1)
    from_t = lambda lo, hi: jnp.swapaxes(smallt_p[:, lo:hi, :], 1, 2)
    state_p = (heads(ka_p, lp), heads(va_p, lp), from_t(0, SM_WI), heads(kb_p, lp), heads(vb_p, lp),
               from_t(SM_LOGF, SM_LOGF + N_HEADS))
    state_s = (heads(ka_s[0], ls), heads(va_s[0], ls), ki_s.reshape(n_db, 1, HEAD_DIM), heads(kb_s[0], ls),
               heads(vb_s[0], ls), logf_s.reshape(n_db, 1, N_HEADS))
    return y_p, y_s.reshape(n_db, 1, D_MODEL), state_p, state_s


def kernel(x_prompt, x_sample, cache_k_a, cache_v_a, cache_k_idx, cache_k_b, cache_v_b, cache_logf_b, page_table,
           c_prompt, c_sample, w_ada, b_ada, g_norm1, w_in, b_forget, g_qa, g_ka, g_qb, g_kb, w_o_a, w_o_b, w_out,
           g_norm2, w_up, w_down):
    depth = w_in.shape[0]
    n_b, seq, _ = x_prompt.shape
    n_db, dec_seq, _ = x_sample.shape
    assert dec_seq == 1 and x_prompt.shape[-1] == D_MODEL
    assert cache_k_a.shape[3:] == (N_HEADS, HEAD_DIM) and cache_k_idx.shape[3] == HEAD_DIM
    tiles = (min(512, seq), min(256, seq))

    c_all = jnp.concatenate([c_prompt, c_sample], axis=0)
    pad = (-c_all.shape[0]) % 8
    c_all = jnp.pad(c_all, ((0, pad), (0, 0)))
    gmat = jnp.asarray(np.kron(np.eye(N_HEADS, dtype=np.float32),
                               np.full((HEAD_DIM, HEAD_DIM), 1.0 / HEAD_DIM, np.float32)), BF16)
    split = 7 * WIDTH
    y_p, y_s = x_prompt, x_sample.reshape(n_db, D_MODEL)
    states_p, states_s = [], []
    for l in range(depth):
        mod = _ada_mod(c_all, w_ada[l], b_ada[l])
        mod_p = mod[:n_b].reshape(n_b, 1, 6 * D_MODEL)
        mod_s = mod[n_b:n_b + n_db].reshape(1, n_db, 6 * D_MODEL)
        w = w_in[l]
        k_idx_cols = w[:, split:split + HEAD_DIM]
        w_r = jnp.concatenate(
            [w[:, :split], w[:, split:split + HEAD_DIM + 2 * N_HEADS],
             jnp.zeros((D_MODEL, LANES - HEAD_DIM - 2 * N_HEADS), w.dtype),
             k_idx_cols, k_idx_cols, w[:, split + HEAD_DIM + 2 * N_HEADS:]], axis=1).astype(BF16)
        gains = jnp.stack([jnp.tile(g[l], N_HEADS) for g in (g_qa, g_ka, g_qb, g_kb)])
        bsm = jnp.zeros((1, LANES), F32).at[0, SM_LOGF:SM_LOGF + N_HEADS].set(b_forget[l])
        lw = (g_norm1[l].reshape(1, D_MODEL), w_r, gains, bsm, gmat, w_o_a[l].astype(BF16), w_o_b[l].astype(BF16),
              w_out[l].astype(BF16), g_norm2[l].reshape(1, D_MODEL), w_up[l].astype(BF16), w_down[l].astype(BF16))
        logf_t = jnp.swapaxes(cache_logf_b[l], 1, 2)
        y_p, y_s, st_p, st_s = _layer(l, y_p, y_s, (cache_k_a, cache_v_a, cache_k_b, cache_v_b), cache_k_idx,
                                      logf_t, page_table, mod_p, mod_s, lw, tiles)
        y_s = y_s.reshape(n_db, D_MODEL)
        states_p.append(st_p)
        states_s.append(st_s)
    k_a_p, v_a_p, k_idx_p, k_b_p, v_b_p, logf_p = (jnp.stack(s) for s in zip(*states_p))
    k_a_s, v_a_s, k_idx_s, k_b_s, v_b_s, logf_s = (jnp.stack(s) for s in zip(*states_s))
    return (y_p, y_s.reshape(n_db, 1, D_MODEL), k_a_p, v_a_p, k_idx_p, k_b_p, v_b_p, logf_p,
            k_a_s, v_a_s, k_idx_s, k_b_s, v_b_s, logf_s)
```

```python
import functools

import numpy as np
import jax
import jax.numpy as jnp
from jax import lax
from jax.experimental import pallas as pl
from jax.experimental.pallas import tpu as pltpu

F32 = jnp.float32
BF16 = jnp.bfloat16
I32 = jnp.int32

D_MODEL = 1024
HEAD_DIM = 64
N_HEADS = 8
WIDTH = N_HEADS * HEAD_DIM
D_FF = 4 * D_MODEL
TOPK_MAX = 256
EPS = 1e-6
MASK_VALUE = -1e30
LANES = 128
BF16_ROWS = 16
QK_SCALE = HEAD_DIM ** -0.5
WI_SCALE = N_HEADS ** -0.5
ALIBI_SLOPES = tuple(2.0 ** (-8.0 * (h + 1) / N_HEADS) for h in range(N_HEADS))
INT_MIN = -(2 ** 31)
VMEM_LIMIT = 56 * 1024 * 1024

C_QA, C_KA, C_VA, C_QB, C_KB, C_VB, C_QI = (i * WIDTH for i in range(7))
C_SMALL = 7 * WIDTH
C_KI2 = C_SMALL + LANES
C_GA = C_KI2 + LANES
C_GB = C_GA + D_MODEL
C_TOTAL = C_GB + D_MODEL
SM_WI = HEAD_DIM
SM_LOGF = HEAD_DIM + N_HEADS


def _topk_count(n_keys):
    return max(1, min(TOPK_MAX, n_keys // 4))


def _const_spec(shape):
    zeros = (0,) * len(shape)
    return pl.BlockSpec(shape, lambda *_: zeros, pipeline_mode=pl.Buffered(1))


def _params(*sem):
    return pltpu.CompilerParams(dimension_semantics=sem, vmem_limit_bytes=VMEM_LIMIT)


def _nt_dot(a, b):
    return lax.dot_general(a, b, (((1,), (1,)), ((), ())), preferred_element_type=F32)


def _dot(a, b):
    return jnp.dot(a, b, preferred_element_type=F32)


def _split3(x):
    p1 = x.astype(BF16)
    r1 = x - p1.astype(F32)
    p2 = r1.astype(BF16)
    p3 = (r1 - p2.astype(F32)).astype(BF16)
    return p1, p2, p3


def _ada_kernel(c_ref, w_ref, b_ref, o_ref):
    c = c_ref[...]
    s = c * jax.nn.sigmoid(c)
    o_ref[...] = _dot(s.astype(BF16), w_ref[...].astype(BF16)) + b_ref[...]


def _ada_mod(c, w_ada, b_ada):
    rows = c.shape[0]
    n_out = w_ada.shape[1]
    tn = D_MODEL
    return pl.pallas_call(
        _ada_kernel,
        out_shape=jax.ShapeDtypeStruct((rows, n_out), F32),
        grid=(n_out // tn,),
        in_specs=[pl.BlockSpec((rows, D_MODEL), lambda j: (0, 0)),
                  pl.BlockSpec((D_MODEL, tn), lambda j: (0, j)),
                  pl.BlockSpec((1, tn), lambda j: (0, j))],
        out_specs=pl.BlockSpec((rows, tn), lambda j: (0, j)),
        compiler_params=_params("arbitrary"),
        name="ada_mod",
    )(c, w_ada, b_ada.reshape(1, n_out))


def _log_sigmoid(x):
    return jnp.minimum(x, 0.0) - jnp.log(1.0 + jnp.exp(-jnp.abs(x)))


FX = 6


def _forget_placement():
    place = np.zeros((3, LANES, 2 * LANES), np.float32)
    const = np.zeros((1, 2 * LANES), np.float32)
    for h in range(N_HEADS):
        for i in range(3):
            place[i, SM_LOGF + h, FX * h + i] = 1.0
            place[i, SM_LOGF + h, LANES + FX * h + 3 + i] = 1.0
            const[0, FX * h + 3 + i] = 1.0
            const[0, LANES + FX * h + i] = -1.0
    return jnp.asarray(place, BF16), jnp.asarray(const, F32)


def _in_proj_kernel(x_ref, shift_ref, scale_ref, g1_ref, w_ref, gains_ref, bsm_ref, gmat_ref,
                    tri_ref, place_ref, fconst_ref,
                    ka_ref, va_ref, kb_ref, vb_ref, small_ref,
                    qa_bf, ka_bf, vat_bf, qb_bf, kb_bf, vbt_bf, qi_bf, ki2_bf, smallt_ref, sga_ref, sgb_ref,
                    kx_ref, qx_ref, carry_ref):
    x = x_ref[...]
    h = x * lax.rsqrt(jnp.mean(x * x, axis=-1, keepdims=True) + EPS) * g1_ref[...]
    h = h * (1.0 + scale_ref[...]) + shift_ref[...]
    hb = h.astype(BF16)

    def seg(col, width=WIDTH):
        return _dot(hb, w_ref[:, col:col + width])

    def head_norm(z, gi):
        msq = _dot((z * z).astype(BF16), gmat_ref[...])
        return z * lax.rsqrt(msq + EPS) * gains_ref[gi:gi + 1, :]

    qa = head_norm(seg(C_QA), 0)
    qa_bf[...] = (qa * QK_SCALE).astype(BF16)
    ka = head_norm(seg(C_KA), 1)
    ka_ref[...] = ka
    ka_bf[...] = ka.astype(BF16)
    va = seg(C_VA)
    va_ref[...] = va
    vat_bf[...] = va.T.astype(BF16)
    qb = head_norm(seg(C_QB), 2)
    qb_bf[...] = (qb * QK_SCALE).astype(BF16)
    kb = head_norm(seg(C_KB), 3)
    kb_ref[...] = kb
    kb_bf[...] = kb.astype(BF16)
    vb = seg(C_VB)
    vb_ref[...] = vb
    vbt_bf[...] = vb.T.astype(BF16)
    qi_bf[...] = seg(C_QI).astype(BF16)
    ki2_bf[...] = seg(C_KI2, LANES).astype(BF16)

    zs = seg(C_SMALL, LANES)
    lane = lax.broadcasted_iota(I32, zs.shape, 1)
    logf = _log_sigmoid(zs + bsm_ref[...])
    small = jnp.where(lane < SM_WI, zs,
                      jnp.where(lane < SM_LOGF, zs * WI_SCALE,
                                jnp.where(lane < SM_LOGF + N_HEADS, logf, 0.0)))
    small_ref[...] = small
    smallt_ref[...] = small.T

    @pl.when(pl.program_id(1) == 0)
    def _():
        carry_ref[...] = jnp.zeros_like(carry_ref)

    tri = tri_ref[...]
    p1, p2, p3 = _split3(small)
    cs = _dot(tri, p1) + _dot(tri, p2) + _dot(tri, p3) + carry_ref[...]
    n = cs.shape[0]
    carry_ref[...] = cs[n - 1:n, :]
    f1, f2, f3 = _split3(cs)
    ext = _dot(f1, place_ref[0]) + _dot(f2, place_ref[1]) + _dot(f3, place_ref[2]) + fconst_ref[...]
    kx_ref[...] = ext[:, :LANES].astype(BF16)
    qx_ref[...] = ext[:, LANES:].astype(BF16)

    sga_ref[...] = jax.nn.sigmoid(seg(C_GA, D_MODEL)).astype(BF16)
    sgb_ref[...] = jax.nn.sigmoid(seg(C_GB, D_MODEL)).astype(BF16)


def _in_proj(x, mod, g1, w_r, gains, bsm, gmat, tm):
    n_g, seq, _ = x.shape
    n_r = mod.shape[1]
    rb = 1 if n_r == 1 else tm
    nblk = seq // tm
    row = lambda width: pl.BlockSpec((None, tm, width), lambda g, i: (g, i, 0))
    modspec = lambda j: pl.BlockSpec((None, rb, D_MODEL), lambda g, i: (g, i if n_r > 1 else 0, j))
    sds = lambda shape, dt: jax.ShapeDtypeStruct(shape, dt)
    out_shape = (
        sds((n_g, seq, WIDTH), F32), sds((n_g, seq, WIDTH), F32), sds((n_g, seq, WIDTH), F32),
        sds((n_g, seq, WIDTH), F32), sds((n_g, seq, LANES), F32),
        sds((n_g, seq, WIDTH), BF16), sds((n_g, seq, WIDTH), BF16), sds((n_g, nblk, WIDTH, tm), BF16),
        sds((n_g, seq, WIDTH), BF16), sds((n_g, seq, WIDTH), BF16), sds((n_g, nblk, WIDTH, tm), BF16),
        sds((n_g, seq, WIDTH), BF16), sds((n_g, seq, LANES), BF16), sds((n_g, LANES, seq), F32),
        sds((n_g, seq, D_MODEL), BF16), sds((n_g, seq, D_MODEL), BF16),
        sds((n_g, seq, LANES), BF16), sds((n_g, seq, LANES), BF16),
    )
    chunk_t = pl.BlockSpec((None, None, WIDTH, tm), lambda g, i: (g, i, 0, 0))
    out_specs = (
        row(WIDTH), row(WIDTH), row(WIDTH), row(WIDTH), row(LANES),
        row(WIDTH), row(WIDTH), chunk_t,
        row(WIDTH), row(WIDTH), chunk_t,
        row(WIDTH), row(LANES), pl.BlockSpec((None, LANES, tm), lambda g, i: (g, 0, i)),
        row(D_MODEL), row(D_MODEL),
        row(LANES), row(LANES),
    )
    place, fconst = _forget_placement()
    return pl.pallas_call(
        _in_proj_kernel,
        out_shape=out_shape,
        grid=(n_g, nblk),
        in_specs=[row(D_MODEL), modspec(0), modspec(1), _const_spec((1, D_MODEL)),
                  _const_spec((D_MODEL, C_TOTAL)), _const_spec((4, WIDTH)), _const_spec((1, LANES)),
                  _const_spec((WIDTH, WIDTH)), _const_spec((tm, tm)), _const_spec((3, LANES, 2 * LANES)),
                  _const_spec((1, 2 * LANES))],
        out_specs=out_specs,
        scratch_shapes=[pltpu.VMEM((1, LANES), F32)],
        compiler_params=_params("arbitrary", "arbitrary"),
        name="in_proj",
    )(x, mod, mod, g1, w_r, gains, bsm, gmat, _lower_tri(tm), place, fconst)


def _flash_chunk(k_aug, qaug_scr, vt_of, x_scr, p_scr, m_scr, acc_scr, bias=None, visible=None):
    n_slots, ck = x_scr.shape[0], x_scr.shape[1]
    ones = jnp.ones((BF16_ROWS, ck), BF16)
    alphas = {}

    def logits(s):
        x = _nt_dot(k_aug(s), qaug_scr[s])
        if bias is not None:
            x = x + bias
        if visible is not None:
            x = jnp.where(visible, x, MASK_VALUE)
        x_scr[s] = x
        m_old = m_scr[s]
        m_new = jnp.maximum(m_old, jnp.max(x, axis=0, keepdims=True))
        alphas[s] = jnp.exp(m_old - m_new)
        m_scr[s] = m_new

    def probabilities(s):
        p_scr[s] = jnp.exp(x_scr[s] - m_scr[s]).astype(BF16)

    def values(s):
        acc_scr[s] = alphas[s] * acc_scr[s] + _dot(jnp.concatenate([vt_of(s), ones], axis=0), p_scr[s])

    for phase in (logits, probabilities, values):
        for s in range(n_slots):
            phase(s)


def _flash_scratch(n_slots, tq, ck):
    return [pltpu.VMEM((n_slots, tq, 2 * LANES), BF16), pltpu.VMEM((n_slots, ck, tq), F32),
            pltpu.VMEM((n_slots, ck, tq), BF16), pltpu.VMEM((n_slots, 1, tq), F32),
            pltpu.VMEM((n_slots, HEAD_DIM + BF16_ROWS, tq), F32)]


def _flash_init(m_scr, acc_scr):
    m_scr[...] = jnp.full(m_scr.shape, MASK_VALUE, F32)
    acc_scr[...] = jnp.zeros_like(acc_scr)


def _flash_output(o_ref, acc_scr):
    out_t = jnp.concatenate([acc_scr[s, :HEAD_DIM] / acc_scr[s, HEAD_DIM:HEAD_DIM + 1]
                             for s in range(acc_scr.shape[0])], axis=0)
    o_ref[...] = out_t.T.astype(BF16)


def _head_pick(tq, h):
    lane = lax.broadcasted_iota(I32, (tq, LANES), 1)
    return (lane >= HEAD_DIM) == (h % 2 == 1)


def _pair_lanes(h):
    return slice(LANES * (h // 2), LANES * (h // 2 + 1))


def _ordered_float(u):
    key = u ^ jnp.int32(INT_MIN)
    return pltpu.bitcast(key ^ ((key >> 31) & jnp.int32(0x7FFFFFFF)), F32)


BISECT_STEPS = 22
CUT_UNKNOWN, CUT_EXACT, CUT_RANKED = 0, 1, 2


def _selection_bias(score_scr, bias_scr, tri_ref, n_chunks, n_beyond, valid_fn, *, ck, tq, k_top, try_bisect=None,
                    bounds=None):
    n_acc = 4

    def partials(blk, reduce):
        return reduce(blk.reshape(ck // (8 * n_acc), n_acc, 8, tq), axis=0)

    def finish(acc, reduce):
        return reduce(reduce(acc, axis=0), axis=0, keepdims=True)

    def count(cmp):
        def body(c, acc):
            c0 = pl.multiple_of(c * ck, ck)
            return acc + partials(jnp.where(cmp(score_scr[pl.ds(c0, ck), :]), 1, 0), jnp.sum)
        return finish(lax.fori_loop(0, n_chunks, body, jnp.zeros((n_acc, 8, tq), I32)), jnp.sum)

    def reaching(x):
        return count(lambda blk: blk >= x) + jnp.where(x <= MASK_VALUE, n_beyond, 0)

    def exact_cut(thr):
        def body(c, carry):
            c0 = pl.multiple_of(c * ck, ck)
            keep = jnp.where(score_scr[pl.ds(c0, ck), :] >= thr, 0.0, MASK_VALUE)
            bias_scr[pl.ds(c0, ck), :] = jnp.where(valid_fn(c0), keep, MASK_VALUE)
            return carry
        lax.fori_loop(0, n_chunks, body, 0)

    def ranked_cut(thr):
        n_greater = count(lambda blk: blk > thr) + jnp.where(thr < MASK_VALUE, n_beyond, 0)
        ties_wanted = (k_top - n_greater).astype(F32)

        def body(c, ties_before):
            c0 = pl.multiple_of(c * ck, ck)
            blk = score_scr[pl.ds(c0, ck), :]
            tie = blk == thr
            rank = _dot(tri_ref[...], jnp.where(tie, 1.0, 0.0).astype(BF16)) + ties_before
            keep = jnp.where(blk > thr, 0.0,
                             jnp.where(tie, jnp.where(rank <= ties_wanted, 0.0, MASK_VALUE), MASK_VALUE))
            bias_scr[pl.ds(c0, ck), :] = jnp.where(valid_fn(c0), keep, MASK_VALUE)
            return rank[ck - 1:ck, :]
        lax.fori_loop(0, n_chunks, body, jnp.zeros((1, tq), F32))

    def bitwise():
        def bit_step(i, carry):
            code, reached = carry
            cand_code = code | lax.shift_left(jnp.int32(1), 31 - i)
            cnt = reaching(_ordered_float(cand_code))
            take = cnt >= k_top
            return jnp.where(take, cand_code, code), jnp.where(take, cnt, reached)

        code, reached = lax.fori_loop(0, 32, bit_step, (jnp.zeros((1, tq), I32), jnp.full((1, tq), k_top, I32)))
        thr = _ordered_float(code)
        lax.cond(jnp.max(reached) > k_top, lambda: ranked_cut(thr), lambda: exact_cut(thr))

    if try_bisect is None:
        bitwise()
        return

    def unsettled(reached):
        return jnp.max(jnp.abs(reached - k_top)) > 0

    def bisect():
        lo, top = bounds

        def halve(_, state):
            lo, hi, reached = state
            mid = 0.5 * lo + 0.5 * hi
            cnt = reaching(mid)
            up = cnt >= k_top
            return jnp.where(up, mid, lo), jnp.where(up, hi, mid), jnp.where(up, cnt, reached)

        above = top + (jnp.maximum(top - lo, jnp.abs(top)) * 2.0 ** -10 + 1e-30)
        lo, hi, reached = lax.fori_loop(0, BISECT_STEPS, halve, (lo, above, jnp.full((1, tq), k_top + 1, I32)))

        def tied():
            def below(c, acc):
                c0 = pl.multiple_of(c * ck, ck)
                blk = score_scr[pl.ds(c0, ck), :]
                return jnp.maximum(acc, partials(jnp.where(blk < hi, blk, -jnp.inf), jnp.max))
            top = finish(lax.fori_loop(0, n_chunks, below, jnp.full((n_acc, 8, tq), -jnp.inf, F32)), jnp.max)
            settled = reached == k_top
            confirmed = jnp.min(jnp.where(settled | (reaching(top) >= k_top), 1, 0)) > 0
            return jnp.where(settled, lo, top), jnp.where(confirmed, CUT_RANKED, CUT_UNKNOWN)

        return lax.cond(unsettled(reached), tied, lambda: (lo, jnp.int32(CUT_EXACT)))

    thr, cut = lax.cond(try_bisect, bisect, lambda: (jnp.zeros((1, tq), F32), jnp.int32(CUT_UNKNOWN)))
    lax.switch(cut, [bitwise, lambda: exact_cut(thr), lambda: ranked_cut(thr)])


POS_SPLIT = 256


def _alibi_lanes(pos, slope=None):
    lane = lax.broadcasted_iota(I32, pos.shape, 1)
    lo = pos & (POS_SPLIT - 1)
    hi_f, lo_f = (pos - lo).astype(F32), lo.astype(F32)
    if slope is None:
        val = jnp.where(lane == 0, hi_f, jnp.where(lane == 1, lo_f, jnp.where(lane < 4, 1.0, 0.0)))
    else:
        val = jnp.where(lane < 2, slope, jnp.where(lane == 2, hi_f * -slope, jnp.where(lane == 3, lo_f * -slope, 0.0)))
    return val.astype(BF16)


def _dsa_kernel(qi_ref, ki2_ref, w_ref, qa_ref, ka_ref, vat_ref, tri_ref, o_ref,
                qm_scr, score_scr, bias_scr, kx_scr, qaug_scr, x_scr, p_scr, m_scr, acc_scr, *, tq, ck, k_top, seq):
    q0 = pl.program_id(1) * tq
    n_chunks = (q0 + tq + ck - 1) // ck
    n_beyond = seq - n_chunks * ck

    zeros = jnp.zeros((tq, LANES), BF16)
    qpos = lax.broadcasted_iota(I32, (tq, LANES), 0) + q0
    for h in range(N_HEADS):
        pick = _head_pick(tq, h)
        qm_scr[h] = jnp.where(pick, qi_ref[:, _pair_lanes(h)], zeros)
        qaug_scr[h] = jnp.concatenate([jnp.where(pick, qa_ref[:, _pair_lanes(h)], zeros),
                                       _alibi_lanes(qpos, ALIBI_SLOPES[h])], axis=1)

    w = w_ref[...] * QK_SCALE
    krow = lax.broadcasted_iota(I32, (ck, tq), 0)
    qcol = lax.broadcasted_iota(I32, (ck, tq), 1)

    def visible(c0):
        return (krow + c0) <= (qcol + q0)

    def score_chunk(c, carry):
        lo, hi = carry
        c0 = pl.multiple_of(c * ck, ck)
        kblk = ki2_ref[pl.ds(c0, ck), :]
        acc = jnp.zeros((ck, tq), F32)
        for h in range(N_HEADS):
            acc = acc + jnp.maximum(_nt_dot(kblk, qm_scr[h]), 0.0) * w[h:h + 1, :]
        seen = visible(c0)
        score_scr[pl.ds(c0, ck), :] = jnp.where(seen, acc, MASK_VALUE)
        lo = jnp.minimum(lo, jnp.min(jnp.where(seen, acc, jnp.inf).reshape(ck // 8, 8, tq), axis=0))
        hi = jnp.maximum(hi, jnp.max(jnp.where(seen, acc, MASK_VALUE).reshape(ck // 8, 8, tq), axis=0))
        return lo, hi

    lo, hi = lax.fori_loop(0, n_chunks, score_chunk,
                           (jnp.full((8, tq), jnp.inf, F32), jnp.full((8, tq), -jnp.inf, F32)))
    bounds = (jnp.min(lo, axis=0, keepdims=True), jnp.max(hi, axis=0, keepdims=True))

    _selection_bias(score_scr, bias_scr, tri_ref, n_chunks, n_beyond, visible, ck=ck, tq=tq, k_top=k_top,
                    try_bisect=q0 >= k_top, bounds=bounds)

    _flash_init(m_scr, acc_scr)
    kpos0 = lax.broadcasted_iota(I32, (ck, LANES), 0)

    def attend_chunk(c, carry):
        c0 = pl.multiple_of(c * ck, ck)
        kx_scr[...] = _alibi_lanes(kpos0 + c0)
        k_aug = lambda h: jnp.concatenate([ka_ref[pl.ds(c0, ck), _pair_lanes(h)], kx_scr[...]], axis=1)
        _flash_chunk(k_aug, qaug_scr, lambda h: vat_ref[c, HEAD_DIM * h:HEAD_DIM * (h + 1), :],
                     x_scr, p_scr, m_scr, acc_scr, bias=bias_scr[pl.ds(c0, ck), :])
        return carry

    lax.fori_loop(0, n_chunks, attend_chunk, 0)
    _flash_output(o_ref, acc_scr)


def _dsa_prompt(qi, ki2, small_t, qa, ka, va_t, tri, tq, ck):
    n_b, seq, _ = qi.shape
    nblk = seq // tq
    kern = functools.partial(_dsa_kernel, tq=tq, ck=ck, k_top=_topk_count(seq), seq=seq)
    return pl.pallas_call(
        kern,
        out_shape=jax.ShapeDtypeStruct((n_b, seq, WIDTH), BF16),
        grid=(n_b, nblk),
        in_specs=[pl.BlockSpec((None, tq, WIDTH), lambda b, i: (b, i, 0)),
                  pl.BlockSpec((None, seq, LANES), lambda b, i: (b, 0, 0)),
                  pl.BlockSpec((None, N_HEADS, tq), lambda b, i: (b, SM_WI // N_HEADS, i)),
                  pl.BlockSpec((None, tq, WIDTH), lambda b, i: (b, i, 0)),
                  pl.BlockSpec((None, seq, WIDTH), lambda b, i: (b, 0, 0)),
                  pl.BlockSpec((None, seq // ck, WIDTH, ck), lambda b, i: (b, 0, 0, 0)),
                  _const_spec((ck, ck))],
        out_specs=pl.BlockSpec((None, tq, WIDTH), lambda b, i: (b, i, 0)),
        scratch_shapes=[pltpu.VMEM((N_HEADS, tq, LANES), BF16), pltpu.VMEM((seq, tq), F32),
                        pltpu.VMEM((seq, tq), F32), pltpu.VMEM((ck, LANES), BF16)]
                       + _flash_scratch(N_HEADS, tq, ck),
        compiler_params=_params("arbitrary", "arbitrary"),
        name="dsa_prompt",
    )(qi, ki2, small_t, qa, ka, va_t, tri)


def _head_diag(n_rows):
    lane = lax.broadcasted_iota(I32, (n_rows, WIDTH), 1)
    sub = lax.broadcasted_iota(I32, (n_rows, WIDTH), 0)
    return (lane >= sub * HEAD_DIM) & (lane < (sub + 1) * HEAD_DIM)


def _sidx_kernel(pt_ref, q_ref, w_ref, knew_ref, *rest, n_seq, n_pages, page):
    page_refs, o_ref = rest[:n_seq * n_pages], rest[n_seq * n_pages]
    lane = lax.broadcasted_iota(I32, (1, LANES), 1)
    for g in range(n_seq):
        q = q_ref[g]
        qp = jnp.concatenate([q, jnp.zeros((BF16_ROWS - N_HEADS, HEAD_DIM), F32)], axis=0).astype(BF16)
        w = w_ref[g] * QK_SCALE
        for p in range(n_pages):
            s = _dot(qp, page_refs[g * n_pages + p][...].astype(BF16))[:N_HEADS]
            o_ref[g, :, page * p:page * (p + 1)] = jnp.sum(jnp.maximum(s, 0.0) * w, axis=0, keepdims=True)
        knew = knew_ref[g].astype(BF16).astype(F32)
        s_new = jnp.sum(q * knew, axis=1, keepdims=True)
        sc_new = jnp.sum(jnp.maximum(s_new, 0.0) * w, axis=0, keepdims=True)
        o_ref[g, :, page * n_pages:] = jnp.where(lane == 0, sc_new, 0.0)


def _sample_index_scores(layer, page_table, cache_k_idx, qi_s, wi_s, ki_s):
    n_db, n_pages = page_table.shape
    page = cache_k_idx.shape[2]
    l_pad = n_pages * page + LANES
    n_seq = next(g for g in (8, 4, 2, 1) if n_db % g == 0)
    page_spec = lambda g, p: pl.BlockSpec((None, None, HEAD_DIM, page),
                                          lambda b, pt: (layer, pt[b * n_seq + g, p], 0, 0))
    k_idx_t = jnp.swapaxes(cache_k_idx, 2, 3)
    grid_spec = pltpu.PrefetchScalarGridSpec(
        num_scalar_prefetch=1,
        grid=(n_db // n_seq,),
        in_specs=[pl.BlockSpec((n_seq, N_HEADS, HEAD_DIM), lambda b, pt: (b, 0, 0)),
                  pl.BlockSpec((n_seq, N_HEADS, 1), lambda b, pt: (b, 0, 0)),
                  pl.BlockSpec((n_seq, 1, HEAD_DIM), lambda b, pt: (b, 0, 0))]
                 + [page_spec(g, p) for g in range(n_seq) for p in range(n_pages)],
        out_specs=pl.BlockSpec((n_seq, 1, l_pad), lambda b, pt: (b, 0, 0)),
    )
    out = pl.pallas_call(
        functools.partial(_sidx_kernel, n_seq=n_seq, n_pages=n_pages, page=page),
        out_shape=jax.ShapeDtypeStruct((n_db, 1, l_pad), F32),
        grid_spec=grid_spec,
        compiler_params=_params("arbitrary"),
        name="sample_index_scores",
    )(page_table, qi_s.astype(F32).reshape(n_db, N_HEADS, HEAD_DIM), wi_s.reshape(n_db, N_HEADS, 1),
      ki_s.reshape(n_db, 1, HEAD_DIM), *([k_idx_t] * (n_seq * n_pages)))
    return out.reshape(n_db, l_pad)


def _ssel_kernel(s_ref, tri_ref, o_ref, score_scr, *, n_keys, ck, k_top):
    l_pad, tq = s_ref.shape
    krow = lax.broadcasted_iota(I32, (ck, tq), 0)

    def valid(c0):
        return (krow + c0) < n_keys

    def drop_padding(c, carry):
        c0 = pl.multiple_of(c * ck, ck)
        score_scr[pl.ds(c0, ck), :] = jnp.where(valid(c0), s_ref[pl.ds(c0, ck), :], -jnp.inf)
        return carry

    lax.fori_loop(0, l_pad // ck, drop_padding, 0)
    _selection_bias(score_scr, o_ref, tri_ref, l_pad // ck, 0, valid, ck=ck, tq=tq, k_top=k_top)


def _sample_selection(scores_t, tri, n_keys):
    l_pad, n_db = scores_t.shape
    return pl.pallas_call(
        functools.partial(_ssel_kernel, n_keys=n_keys, ck=LANES, k_top=_topk_count(n_keys)),
        out_shape=jax.ShapeDtypeStruct((l_pad, n_db), F32),
        grid=(1,),
        in_specs=[pl.BlockSpec((l_pad, n_db), lambda i: (0, 0)), pl.BlockSpec((LANES, LANES), lambda i: (0, 0))],
        out_specs=pl.BlockSpec((l_pad, n_db), lambda i: (0, 0)),
        scratch_shapes=[pltpu.VMEM((l_pad, n_db), F32)],
        compiler_params=_params("arbitrary"),
        name="sample_selection",
    )(scores_t, tri)


def _paged_attend(q_ref, knew_ref, vnew_ref, k_pages, v_pages, logit_terms, o_ref):
    n_pages, page = len(k_pages), k_pages[0].shape[-1]
    diag = _head_diag(N_HEADS)
    qs = jnp.where(_head_diag(BF16_ROWS), jnp.broadcast_to(q_ref[...], (BF16_ROWS, WIDTH)), 0.0)
    qb = qs.astype(BF16)
    s = jnp.concatenate([_dot(qb, k_pages[p][...].astype(BF16))[:N_HEADS] for p in range(n_pages)], axis=1)
    s_new = jnp.sum(qs[:N_HEADS] * knew_ref[...].astype(BF16).astype(F32), axis=1, keepdims=True)
    x, x_new = logit_terms(s, s_new)
    m = jnp.maximum(jnp.max(x, axis=1, keepdims=True), x_new)
    p = jnp.exp(x - m)
    p_new = jnp.exp(x_new - m)
    denom = jnp.sum(p, axis=1, keepdims=True) + p_new
    pb = jnp.concatenate([p, jnp.zeros_like(p)], axis=0).astype(BF16)
    acc = p_new.astype(BF16).astype(F32) * vnew_ref[...].astype(BF16).astype(F32)
    for q in range(n_pages):
        acc = acc + _nt_dot(pb[:, page * q:page * (q + 1)], v_pages[q][...].astype(BF16))[:N_HEADS]
    o_ref[...] = jnp.sum(jnp.where(diag, acc / denom, 0.0), axis=0, keepdims=True)


def _sattn_kernel(pt_ref, qa_ref, qb_ref, kan_ref, van_ref, kbn_ref, vbn_ref, lfn_ref, bias_ref, slope_ref, *rest,
                  n_pages):
    ka_p, va_p, kb_p, vb_p, lf_p = (rest[g * n_pages:(g + 1) * n_pages] for g in range(5))
    oa_ref, ob_ref = rest[5 * n_pages:]
    past = n_pages * ka_p[0].shape[-1]

    def alibi_and_mask(s, s_new):
        rel = (lax.broadcasted_iota(I32, (1, past), 1) - past).astype(F32)
        bias = bias_ref[...]
        return s + slope_ref[...] * rel + bias[:, :past], s_new + bias[:, past:past + 1]

    _paged_attend(qa_ref, kan_ref, van_ref, ka_p, va_p, alibi_and_mask, oa_ref)

    lf = jnp.concatenate([lf_p[p][...] for p in range(n_pages)], axis=1)
    idx = lax.broadcasted_iota(I32, lf.shape, 1)
    d = 1
    while d < past:
        lf = lf + jnp.where(idx >= d, pltpu.roll(lf, d, 1), 0.0)
        d *= 2
    f_new = lf[:, past - 1:past] + lfn_ref[...]
    _paged_attend(qb_ref, kbn_ref, vbn_ref, kb_p, vb_p, lambda s, s_new: (s + (f_new - lf), s_new), ob_ref)


def _sample_attention(layer, page_table, caches, logf_t, qa_s, qb_s, new_rows, logf_new, bias):
    cache_k_a, cache_v_a, cache_k_b, cache_v_b = caches
    n_db, n_pages = page_table.shape
    n_pool, page = cache_k_a.shape[1], cache_k_a.shape[2]
    l_pad = bias.shape[1]
    flat = lambda c: jnp.transpose(c, (0, 1, 3, 4, 2)).reshape(c.shape[0], n_pool, WIDTH, page)
    row = lambda width: pl.BlockSpec((None, 1, width), lambda b, pt: (b, 0, 0))
    kv_spec = lambda p: pl.BlockSpec((None, None, WIDTH, page), lambda b, pt: (layer, pt[b, p], 0, 0))
    lf_spec = lambda p: pl.BlockSpec((None, N_HEADS, page), lambda b, pt: (pt[b, p], 0, 0))
    pages = range(n_pages)
    grid_spec = pltpu.PrefetchScalarGridSpec(
        num_scalar_prefetch=1,
        grid=(n_db,),
        in_specs=[row(WIDTH), row(WIDTH), row(WIDTH), row(WIDTH), row(WIDTH), row(WIDTH),
                  pl.BlockSpec((None, N_HEADS, 1), lambda b, pt: (b, 0, 0)), row(l_pad),
                  pl.BlockSpec((N_HEADS, 1), lambda b, pt: (0, 0))]
                 + [kv_spec(p) for p in pages] * 4 + [lf_spec(p) for p in pages],
        out_specs=(row(WIDTH), row(WIDTH)),
    )
    r3 = lambda a: a.reshape(n_db, 1, a.shape[-1])
    slopes = jnp.asarray(ALIBI_SLOPES, F32).reshape(N_HEADS, 1)
    return pl.pallas_call(
        functools.partial(_sattn_kernel, n_pages=n_pages),
        out_shape=(jax.ShapeDtypeStruct((n_db, 1, WIDTH), F32), jax.ShapeDtypeStruct((n_db, 1, WIDTH), F32)),
        grid_spec=grid_spec,
        compiler_params=_params("arbitrary"),
        name="sample_attention",
    )(page_table, r3(qa_s.astype(F32)), r3(qb_s.astype(F32)), *[r3(a) for a in new_rows],
      logf_new.reshape(n_db, N_HEADS, 1), r3(bias), slopes,
      *([flat(cache_k_a)] * n_pages), *([flat(cache_v_a)] * n_pages), *([flat(cache_k_b)] * n_pages),
      *([flat(cache_v_b)] * n_pages), *([logf_t] * n_pages))


def _fox_kernel(q_ref, qx_ref, k_ref, kx_ref, vt_ref, o_ref, qaug_scr, x_scr, p_scr, m_scr, acc_scr, *, tq, ck):
    q0 = pl.program_id(1) * tq
    n_chunks = (q0 + tq + ck - 1) // ck
    lane = lax.broadcasted_iota(I32, (tq, LANES), 1)
    zeros = jnp.zeros((tq, LANES), BF16)
    qx = qx_ref[...]
    for h in range(N_HEADS):
        own = (lane >= FX * h) & (lane < FX * (h + 1))
        qaug_scr[h] = jnp.concatenate([jnp.where(_head_pick(tq, h), q_ref[:, _pair_lanes(h)], zeros),
                                       jnp.where(own, qx, zeros)], axis=1)
    _flash_init(m_scr, acc_scr)
    krow = lax.broadcasted_iota(I32, (ck, tq), 0)
    qcol = lax.broadcasted_iota(I32, (ck, tq), 1)

    def chunk(c, masked):
        c0 = pl.multiple_of(c * ck, ck)
        k_aug = lambda h: jnp.concatenate([k_ref[pl.ds(c0, ck), _pair_lanes(h)], kx_ref[pl.ds(c0, ck), :]], axis=1)
        visible = ((krow + c0) <= (qcol + q0)) if masked else None
        _flash_chunk(k_aug, qaug_scr, lambda h: vt_ref[c, HEAD_DIM * h:HEAD_DIM * (h + 1), :],
                     x_scr, p_scr, m_scr, acc_scr, visible=visible)

    def full_chunk(c, carry):
        chunk(c, False)
        return carry

    lax.fori_loop(0, n_chunks - 1, full_chunk, 0)
    chunk(n_chunks - 1, True)
    _flash_output(o_ref, acc_scr)


def _fox_prompt(qb, qx, kb, kx, vb_t, tq, ck):
    n_b, seq, _ = qb.shape
    per_q = lambda width: pl.BlockSpec((None, tq, width), lambda b, i: (b, i, 0))
    whole = lambda width: pl.BlockSpec((None, seq, width), lambda b, i: (b, 0, 0))
    return pl.pallas_call(
        functools.partial(_fox_kernel, tq=tq, ck=ck),
        out_shape=jax.ShapeDtypeStruct((n_b, seq, WIDTH), BF16),
        grid=(n_b, seq // tq),
        in_specs=[per_q(WIDTH), per_q(LANES), whole(WIDTH), whole(LANES),
                  pl.BlockSpec((None, seq // ck, WIDTH, ck), lambda b, i: (b, 0, 0, 0))],
        out_specs=per_q(WIDTH),
        scratch_shapes=_flash_scratch(N_HEADS, tq, ck),
        compiler_params=_params("arbitrary", "arbitrary"),
        name="fox_prompt",
    )(qb, qx, kb, kx, vb_t)


def _post_kernel(x_ref, oa_ref, ob_ref, sga_ref, sgb_ref, gate1_ref, shift2_ref, scale2_ref, gate2_ref, g2_ref,
                 woa_ref, wob_ref, wout_ref, wup_ref, wdn_ref, y_ref, *, ff_chunk):
    merged = (sga_ref[...].astype(F32) * _dot(oa_ref[...], woa_ref[...])
              + sgb_ref[...].astype(F32) * _dot(ob_ref[...], wob_ref[...]))
    x1 = x_ref[...] + gate1_ref[...] * _dot(merged.astype(BF16), wout_ref[...])
    h2 = x1 * lax.rsqrt(jnp.mean(x1 * x1, axis=-1, keepdims=True) + EPS) * g2_ref[...]
    h2 = (h2 * (1.0 + scale2_ref[...]) + shift2_ref[...]).astype(BF16)
    mlp = jnp.zeros(x1.shape, F32)
    for c in range(0, D_FF, ff_chunk):
        u = jnp.maximum(_dot(h2, wup_ref[:, c:c + ff_chunk]), 0.0)
        mlp = mlp + _dot((u * u).astype(BF16), wdn_ref[c:c + ff_chunk, :])
    y_ref[...] = x1 + gate2_ref[...] * mlp


def _post(x, oa, ob, sga, sgb, mod, g2, woa, wob, wout, wup, wdn, tm):
    n_g, seq, _ = x.shape
    n_r = mod.shape[1]
    rb = 1 if n_r == 1 else tm
    row = lambda width: pl.BlockSpec((None, tm, width), lambda g, i: (g, i, 0))
    modspec = lambda j: pl.BlockSpec((None, rb, D_MODEL), lambda g, i: (g, i if n_r > 1 else 0, j))
    return pl.pallas_call(
        functools.partial(_post_kernel, ff_chunk=1024),
        out_shape=jax.ShapeDtypeStruct((n_g, seq, D_MODEL), F32),
        grid=(n_g, seq // tm),
        in_specs=[row(D_MODEL), row(WIDTH), row(WIDTH), row(D_MODEL), row(D_MODEL),
                  modspec(2), modspec(3), modspec(4), modspec(5), _const_spec((1, D_MODEL)),
                  _const_spec((WIDTH, D_MODEL)), _const_spec((WIDTH, D_MODEL)), _const_spec((D_MODEL, D_MODEL)),
                  _const_spec((D_MODEL, D_FF)), _const_spec((D_FF, D_MODEL))],
        out_specs=row(D_MODEL),
        compiler_params=_params("arbitrary", "arbitrary"),
        name="merge_out_mlp",
    )(x, oa, ob, sga, sgb, mod, mod, mod, mod, g2, woa, wob, wout, wup, wdn)


def _lower_tri(n):
    return jnp.asarray(np.tril(np.ones((n, n), np.float32)), BF16)


def _layer(l, x_p, x_s, caches, cache_k_idx, logf_t, page_table, mod_p, mod_s, lw, tiles):
    (g_norm1, w_r, gains, bsm, gmat, woa, wob, wout, g_norm2, wup, wdn) = lw
    tm, tq = tiles
    n_b, seq, _ = x_p.shape
    n_db = x_s.shape[0]

    (ka_p, va_p, kb_p, vb_p, small_p, qa_bf, ka_bf, vat_bf, qb_bf, kb_bf, vbt_bf, qi_bf, ki2_bf, smallt_p,
     sga_p, sgb_p, kx_b, qx_b) = _in_proj(x_p, mod_p, g_norm1, w_r, gains, bsm, gmat, tm)
    xs = x_s.reshape(1, n_db, D_MODEL)
    (ka_s, va_s, kb_s, vb_s, small_s, qa_s, _, _, qb_s, _, _, qi_s, _, _, sga_s, sgb_s, _, _) = _in_proj(
        xs, mod_s, g_norm1, w_r, gains, bsm, gmat, n_db)
    small_s2 = small_s[0]
    ki_s, wi_s = small_s2[:, :SM_WI], small_s2[:, SM_WI:SM_LOGF]
    logf_s = small_s2[:, SM_LOGF:SM_LOGF + N_HEADS]

    n_keys = page_table.shape[1] * cache_k_idx.shape[2] + 1
    scores = _sample_index_scores(l, page_table, cache_k_idx, qi_s[0], wi_s, ki_s)
    bias_t = _sample_selection(scores.T, _lower_tri(LANES), n_keys)

    oa_s, ob_s = _sample_attention(l, page_table, caches, logf_t, qa_s[0], qb_s[0],
                                   (ka_s[0], va_s[0], kb_s[0], vb_s[0]), logf_s, bias_t.T)
    ob_p = _fox_prompt(qb_bf, qx_b, kb_bf, kx_b, vbt_bf, tq, tm)
    oa_p = _dsa_prompt(qi_bf, ki2_bf, smallt_p, qa_bf, ka_bf, vat_bf, _lower_tri(tm), tq, tm)

    y_p = _post(x_p, oa_p, ob_p, sga_p, sgb_p, mod_p, g_norm2, woa, wob, wout, wup, wdn, tm)
    y_s = _post(xs, oa_s.reshape(1, n_db, WIDTH).astype(BF16), ob_s.reshape(1, n_db, WIDTH).astype(BF16),
                sga_s, sgb_s, mod_s, g_norm2, woa, wob, wout, wup, wdn, n_db)

    heads = lambda a, lead: a.reshape(*lead, N_HEADS, HEAD_DIM)
    lp, ls = (n_b, seq), (n_db, 1)
    from_t = lambda lo, hi: jnp.swapaxes(smallt_p[:, lo:hi, :], 1, 2)
    state_p = (heads(ka_p, lp), heads(va_p, lp), from_t(0, SM_WI), heads(kb_p, lp), heads(vb_p, lp),
               from_t(SM_LOGF, SM_LOGF + N_HEADS))
    state_s = (heads(ka_s[0], ls), heads(va_s[0], ls), ki_s.reshape(n_db, 1, HEAD_DIM), heads(kb_s[0], ls),
               heads(vb_s[0], ls), logf_s.reshape(n_db, 1, N_HEADS))
    return y_p, y_s.reshape(n_db, 1, D_MODEL), state_p, state_s


def kernel(x_prompt, x_sample, cache_k_a, cache_v_a, cache_k_idx, cache_k_b, cache_v_b, cache_logf_b, page_table,
           c_prompt, c_sample, w_ada, b_ada, g_norm1, w_in, b_forget, g_qa, g_ka, g_qb, g_kb, w_o_a, w_o_b, w_out,
           g_norm2, w_up, w_down):
    depth = w_in.shape[0]
    n_b, seq, _ = x_prompt.shape
    n_db, dec_seq, _ = x_sample.shape
    assert dec_seq == 1 and x_prompt.shape[-1] == D_MODEL
    assert cache_k_a.shape[3:] == (N_HEADS, HEAD_DIM) and cache_k_idx.shape[3] == HEAD_DIM
    tiles = (min(512, seq), min(256, seq))

    c_all = jnp.concatenate([c_prompt, c_sample], axis=0)
    pad = (-c_all.shape[0]) % 8
    c_all = jnp.pad(c_all, ((0, pad), (0, 0)))
    gmat = jnp.asarray(np.kron(np.eye(N_HEADS, dtype=np.float32),
                               np.full((HEAD_DIM, HEAD_DIM), 1.0 / HEAD_DIM, np.float32)), BF16)
    split = 7 * WIDTH
    y_p, y_s = x_prompt, x_sample.reshape(n_db, D_MODEL)
    states_p, states_s = [], []
    for l in range(depth):
        mod = _ada_mod(c_all, w_ada[l], b_ada[l])
        mod_p = mod[:n_b].reshape(n_b, 1, 6 * D_MODEL)
        mod_s = mod[n_b:n_b + n_db].reshape(1, n_db, 6 * D_MODEL)
        w = w_in[l]
        k_idx_cols = w[:, split:split + HEAD_DIM]
        w_r = jnp.concatenate(
            [w[:, :split], w[:, split:split + HEAD_DIM + 2 * N_HEADS],
             jnp.zeros((D_MODEL, LANES - HEAD_DIM - 2 * N_HEADS), w.dtype),
             k_idx_cols, k_idx_cols, w[:, split + HEAD_DIM + 2 * N_HEADS:]], axis=1).astype(BF16)
        gains = jnp.stack([jnp.tile(g[l], N_HEADS) for g in (g_qa, g_ka, g_qb, g_kb)])
        bsm = jnp.zeros((1, LANES), F32).at[0, SM_LOGF:SM_LOGF + N_HEADS].set(b_forget[l])
        lw = (g_norm1[l].reshape(1, D_MODEL), w_r, gains, bsm, gmat, w_o_a[l].astype(BF16), w_o_b[l].astype(BF16),
              w_out[l].astype(BF16), g_norm2[l].reshape(1, D_MODEL), w_up[l].astype(BF16), w_down[l].astype(BF16))
        logf_t = jnp.swapaxes(cache_logf_b[l], 1, 2)
        y_p, y_s, st_p, st_s = _layer(l, y_p, y_s, (cache_k_a, cache_v_a, cache_k_b, cache_v_b), cache_k_idx,
                                      logf_t, page_table, mod_p, mod_s, lw, tiles)
        y_s = y_s.reshape(n_db, D_MODEL)
        states_p.append(st_p)
        states_s.append(st_s)
    k_a_p, v_a_p, k_idx_p, k_b_p, v_b_p, logf_p = (jnp.stack(s) for s in zip(*states_p))
    k_a_s, v_a_s, k_idx_s, k_b_s, v_b_s, logf_s = (jnp.stack(s) for s in zip(*states_s))
    return (y_p, y_s.reshape(n_db, 1, D_MODEL), k_a_p, v_a_p, k_idx_p, k_b_p, v_b_p, logf_p,
            k_a_s, v_a_s, k_idx_s, k_b_s, v_b_s, logf_s)
```

```python
import functools

import numpy as np
import jax
import jax.numpy as jnp
from jax import lax
from jax.experimental import pallas as pl
from jax.experimental.pallas import tpu as pltpu

F32 = jnp.float32
BF16 = jnp.bfloat16
I32 = jnp.int32

D_MODEL = 1024
HEAD_DIM = 64
N_HEADS = 8
WIDTH = N_HEADS * HEAD_DIM
D_FF = 4 * D_MODEL
TOPK_MAX = 256
EPS = 1e-6
MASK_VALUE = -1e30
LANES = 128
BF16_ROWS = 16
QK_SCALE = HEAD_DIM ** -0.5
WI_SCALE = N_HEADS ** -0.5
ALIBI_SLOPES = tuple(2.0 ** (-8.0 * (h + 1) / N_HEADS) for h in range(N_HEADS))
INT_MIN = -(2 ** 31)
VMEM_LIMIT = 56 * 1024 * 1024

C_QA, C_KA, C_VA, C_QB, C_KB, C_VB, C_QI = (i * WIDTH for i in range(7))
C_SMALL = 7 * WIDTH
C_KI2 = C_SMALL + LANES
C_GA = C_KI2 + LANES
C_GB = C_GA + D_MODEL
C_TOTAL = C_GB + D_MODEL
SM_WI = HEAD_DIM
SM_LOGF = HEAD_DIM + N_HEADS


def _topk_count(n_keys):
    return max(1, min(TOPK_MAX, n_keys // 4))


def _const_spec(shape):
    zeros = (0,) * len(shape)
    return pl.BlockSpec(shape, lambda *_: zeros, pipeline_mode=pl.Buffered(1))


def _params(*sem):
    return pltpu.CompilerParams(dimension_semantics=sem, vmem_limit_bytes=VMEM_LIMIT)


def _nt_dot(a, b):
    return lax.dot_general(a, b, (((1,), (1,)), ((), ())), preferred_element_type=F32)


def _dot(a, b):
    return jnp.dot(a, b, preferred_element_type=F32)


def _split3(x):
    p1 = x.astype(BF16)
    r1 = x - p1.astype(F32)
    p2 = r1.astype(BF16)
    p3 = (r1 - p2.astype(F32)).astype(BF16)
    return p1, p2, p3


def _ada_kernel(c_ref, w_ref, b_ref, o_ref):
    c = c_ref[...]
    s = c * jax.nn.sigmoid(c)
    o_ref[...] = _dot(s.astype(BF16), w_ref[...].astype(BF16)) + b_ref[...]


def _ada_mod(c, w_ada, b_ada):
    rows = c.shape[0]
    n_out = w_ada.shape[1]
    tn = D_MODEL
    return pl.pallas_call(
        _ada_kernel,
        out_shape=jax.ShapeDtypeStruct((rows, n_out), F32),
        grid=(n_out // tn,),
        in_specs=[pl.BlockSpec((rows, D_MODEL), lambda j: (0, 0)),
                  pl.BlockSpec((D_MODEL, tn), lambda j: (0, j)),
                  pl.BlockSpec((1, tn), lambda j: (0, j))],
        out_specs=pl.BlockSpec((rows, tn), lambda j: (0, j)),
        compiler_params=_params("arbitrary"),
        name="ada_mod",
    )(c, w_ada, b_ada.reshape(1, n_out))


def _log_sigmoid(x):
    return jnp.minimum(x, 0.0) - jnp.log(1.0 + jnp.exp(-jnp.abs(x)))


FX = 6


def _forget_placement():
    place = np.zeros((3, LANES, 2 * LANES), np.float32)
    const = np.zeros((1, 2 * LANES), np.float32)
    for h in range(N_HEADS):
        for i in range(3):
            place[i, SM_LOGF + h, FX * h + i] = 1.0
            place[i, SM_LOGF + h, LANES + FX * h + 3 + i] = 1.0
            const[0, FX * h + 3 + i] = 1.0
            const[0, LANES + FX * h + i] = -1.0
    return jnp.asarray(place, BF16), jnp.asarray(const, F32)


def _in_proj_kernel(x_ref, shift_ref, scale_ref, g1_ref, w_ref, gains_ref, bsm_ref, gmat_ref,
                    tri_ref, place_ref, fconst_ref,
                    ka_ref, va_ref, kb_ref, vb_ref, small_ref,
                    qa_bf, ka_bf, vat_bf, qb_bf, kb_bf, vbt_bf, qi_bf, ki2_bf, smallt_ref, sga_ref, sgb_ref,
                    kx_ref, qx_ref, carry_ref):
    x = x_ref[...]
    h = x * lax.rsqrt(jnp.mean(x * x, axis=-1, keepdims=True) + EPS) * g1_ref[...]
    h = h * (1.0 + scale_ref[...]) + shift_ref[...]
    hb = h.astype(BF16)

    def seg(col, width=WIDTH):
        return _dot(hb, w_ref[:, col:col + width])

    def head_norm(z, gi):
        msq = _dot((z * z).astype(BF16), gmat_ref[...])
        return z * lax.rsqrt(msq + EPS) * gains_ref[gi:gi + 1, :]

    qa = head_norm(seg(C_QA), 0)
    qa_bf[...] = (qa * QK_SCALE).astype(BF16)
    ka = head_norm(seg(C_KA), 1)
    ka_ref[...] = ka
    ka_bf[...] = ka.astype(BF16)
    va = seg(C_VA)
    va_ref[...] = va
    vat_bf[...] = va.T.astype(BF16)
    qb = head_norm(seg(C_QB), 2)
    qb_bf[...] = (qb * QK_SCALE).astype(BF16)
    kb = head_norm(seg(C_KB), 3)
    kb_ref[...] = kb
    kb_bf[...] = kb.astype(BF16)
    vb = seg(C_VB)
    vb_ref[...] = vb
    vbt_bf[...] = vb.T.astype(BF16)
    qi_bf[...] = seg(C_QI).astype(BF16)
    ki2_bf[...] = seg(C_KI2, LANES).astype(BF16)

    zs = seg(C_SMALL, LANES)
    lane = lax.broadcasted_iota(I32, zs.shape, 1)
    logf = _log_sigmoid(zs + bsm_ref[...])
    small = jnp.where(lane < SM_WI, zs,
                      jnp.where(lane < SM_LOGF, zs * WI_SCALE,
                                jnp.where(lane < SM_LOGF + N_HEADS, logf, 0.0)))
    small_ref[...] = small
    smallt_ref[...] = small.T

    @pl.when(pl.program_id(1) == 0)
    def _():
        carry_ref[...] = jnp.zeros_like(carry_ref)

    tri = tri_ref[...]
    p1, p2, p3 = _split3(small)
    cs = _dot(tri, p1) + _dot(tri, p2) + _dot(tri, p3) + carry_ref[...]
    n = cs.shape[0]
    carry_ref[...] = cs[n - 1:n, :]
    f1, f2, f3 = _split3(cs)
    ext = _dot(f1, place_ref[0]) + _dot(f2, place_ref[1]) + _dot(f3, place_ref[2]) + fconst_ref[...]
    kx_ref[...] = ext[:, :LANES].astype(BF16)
    qx_ref[...] = ext[:, LANES:].astype(BF16)

    sga_ref[...] = jax.nn.sigmoid(seg(C_GA, D_MODEL)).astype(BF16)
    sgb_ref[...] = jax.nn.sigmoid(seg(C_GB, D_MODEL)).astype(BF16)


def _in_proj(x, mod, g1, w_r, gains, bsm, gmat, tm):
    n_g, seq, _ = x.shape
    n_r = mod.shape[1]
    rb = 1 if n_r == 1 else tm
    nblk = seq // tm
    row = lambda width: pl.BlockSpec((None, tm, width), lambda g, i: (g, i, 0))
    modspec = lambda j: pl.BlockSpec((None, rb, D_MODEL), lambda g, i: (g, i if n_r > 1 else 0, j))
    sds = lambda shape, dt: jax.ShapeDtypeStruct(shape, dt)
    out_shape = (
        sds((n_g, seq, WIDTH), F32), sds((n_g, seq, WIDTH), F32), sds((n_g, seq, WIDTH), F32),
        sds((n_g, seq, WIDTH), F32), sds((n_g, seq, LANES), F32),
        sds((n_g, seq, WIDTH), BF16), sds((n_g, seq, WIDTH), BF16), sds((n_g, nblk, WIDTH, tm), BF16),
        sds((n_g, seq, WIDTH), BF16), sds((n_g, seq, WIDTH), BF16), sds((n_g, nblk, WIDTH, tm), BF16),
        sds((n_g, seq, WIDTH), BF16), sds((n_g, seq, LANES), BF16), sds((n_g, LANES, seq), F32),
        sds((n_g, seq, D_MODEL), BF16), sds((n_g, seq, D_MODEL), BF16),
        sds((n_g, seq, LANES), BF16), sds((n_g, seq, LANES), BF16),
    )
    chunk_t = pl.BlockSpec((None, None, WIDTH, tm), lambda g, i: (g, i, 0, 0))
    out_specs = (
        row(WIDTH), row(WIDTH), row(WIDTH), row(WIDTH), row(LANES),
        row(WIDTH), row(WIDTH), chunk_t,
        row(WIDTH), row(WIDTH), chunk_t,
        row(WIDTH), row(LANES), pl.BlockSpec((None, LANES, tm), lambda g, i: (g, 0, i)),
        row(D_MODEL), row(D_MODEL),
        row(LANES), row(LANES),
    )
    place, fconst = _forget_placement()
    return pl.pallas_call(
        _in_proj_kernel,
        out_shape=out_shape,
        grid=(n_g, nblk),
        in_specs=[row(D_MODEL), modspec(0), modspec(1), _const_spec((1, D_MODEL)),
                  _const_spec((D_MODEL, C_TOTAL)), _const_spec((4, WIDTH)), _const_spec((1, LANES)),
                  _const_spec((WIDTH, WIDTH)), _const_spec((tm, tm)), _const_spec((3, LANES, 2 * LANES)),
                  _const_spec((1, 2 * LANES))],
        out_specs=out_specs,
        scratch_shapes=[pltpu.VMEM((1, LANES), F32)],
        compiler_params=_params("arbitrary", "arbitrary"),
        name="in_proj",
    )(x, mod, mod, g1, w_r, gains, bsm, gmat, _lower_tri(tm), place, fconst)


def _flash_chunk(k_aug, qaug_scr, vt_of, x_scr, m_scr, acc_scr, bias=None, visible=None):
    n_slots, ck = x_scr.shape[0], x_scr.shape[1]
    ones = jnp.ones((BF16_ROWS, ck), BF16)
    alphas = {}

    def logits(s):
        x = _nt_dot(k_aug(s), qaug_scr[s])
        if bias is not None:
            x = x + bias
        if visible is not None:
            x = jnp.where(visible, x, MASK_VALUE)
        x_scr[s] = x
        m_old = m_scr[s]
        m_new = jnp.maximum(m_old, jnp.max(x, axis=0, keepdims=True))
        alphas[s] = jnp.exp(m_old - m_new)
        m_scr[s] = m_new

    def values(s):
        p = jnp.exp(x_scr[s] - m_scr[s]).astype(BF16)
        acc_scr[s] = alphas[s] * acc_scr[s] + _dot(jnp.concatenate([vt_of(s), ones], axis=0), p)

    for phase in (logits, values):
        for s in range(n_slots):
            phase(s)


def _flash_scratch(n_slots, tq, ck):
    return [pltpu.VMEM((n_slots, tq, 2 * LANES), BF16), pltpu.VMEM((n_slots, ck, tq), F32),
            pltpu.VMEM((n_slots, 1, tq), F32), pltpu.VMEM((n_slots, HEAD_DIM + BF16_ROWS, tq), F32)]


def _flash_init(m_scr, acc_scr):
    m_scr[...] = jnp.full(m_scr.shape, MASK_VALUE, F32)
    acc_scr[...] = jnp.zeros_like(acc_scr)


def _flash_output(o_ref, acc_scr):
    out_t = jnp.concatenate([acc_scr[s, :HEAD_DIM] / acc_scr[s, HEAD_DIM:HEAD_DIM + 1]
                             for s in range(acc_scr.shape[0])], axis=0)
    o_ref[...] = out_t.T.astype(BF16)


def _head_pick(tq, h):
    lane = lax.broadcasted_iota(I32, (tq, LANES), 1)
    return (lane >= HEAD_DIM) == (h % 2 == 1)


def _pair_lanes(h):
    return slice(LANES * (h // 2), LANES * (h // 2 + 1))


def _ordered_float(u):
    key = u ^ jnp.int32(INT_MIN)
    return pltpu.bitcast(key ^ ((key >> 31) & jnp.int32(0x7FFFFFFF)), F32)


BISECT_STEPS = 22
CUT_UNKNOWN, CUT_EXACT, CUT_RANKED = 0, 1, 2


def _selection_bias(score_scr, bias_scr, tri_ref, n_chunks, n_beyond, valid_fn, *, ck, tq, k_top, try_bisect=None,
                    bounds=None):
    n_acc = 4

    def partials(blk, reduce):
        return reduce(blk.reshape(ck // (8 * n_acc), n_acc, 8, tq), axis=0)

    def finish(acc, reduce):
        return reduce(reduce(acc, axis=0), axis=0, keepdims=True)

    def count(cmp):
        def body(c, acc):
            c0 = pl.multiple_of(c * ck, ck)
            return acc + partials(jnp.where(cmp(score_scr[pl.ds(c0, ck), :]), 1, 0), jnp.sum)
        return finish(lax.fori_loop(0, n_chunks, body, jnp.zeros((n_acc, 8, tq), I32)), jnp.sum)

    def reaching(x):
        return count(lambda blk: blk >= x) + jnp.where(x <= MASK_VALUE, n_beyond, 0)

    def exact_cut(thr):
        def body(c, carry):
            c0 = pl.multiple_of(c * ck, ck)
            keep = jnp.where(score_scr[pl.ds(c0, ck), :] >= thr, 0.0, MASK_VALUE)
            bias_scr[pl.ds(c0, ck), :] = jnp.where(valid_fn(c0), keep, MASK_VALUE)
            return carry
        lax.fori_loop(0, n_chunks, body, 0)

    def ranked_cut(thr):
        n_greater = count(lambda blk: blk > thr) + jnp.where(thr < MASK_VALUE, n_beyond, 0)
        ties_wanted = (k_top - n_greater).astype(F32)

        def body(c, ties_before):
            c0 = pl.multiple_of(c * ck, ck)
            blk = score_scr[pl.ds(c0, ck), :]
            tie = blk == thr
            rank = _dot(tri_ref[...], jnp.where(tie, 1.0, 0.0).astype(BF16)) + ties_before
            keep = jnp.where(blk > thr, 0.0,
                             jnp.where(tie, jnp.where(rank <= ties_wanted, 0.0, MASK_VALUE), MASK_VALUE))
            bias_scr[pl.ds(c0, ck), :] = jnp.where(valid_fn(c0), keep, MASK_VALUE)
            return rank[ck - 1:ck, :]
        lax.fori_loop(0, n_chunks, body, jnp.zeros((1, tq), F32))

    def bitwise():
        def bit_step(i, carry):
            code, reached = carry
            cand_code = code | lax.shift_left(jnp.int32(1), 31 - i)
            cnt = reaching(_ordered_float(cand_code))
            take = cnt >= k_top
            return jnp.where(take, cand_code, code), jnp.where(take, cnt, reached)

        code, reached = lax.fori_loop(0, 32, bit_step, (jnp.zeros((1, tq), I32), jnp.full((1, tq), k_top, I32)))
        thr = _ordered_float(code)
        lax.cond(jnp.max(reached) > k_top, lambda: ranked_cut(thr), lambda: exact_cut(thr))

    if try_bisect is None:
        bitwise()
        return

    def unsettled(reached):
        return jnp.max(jnp.abs(reached - k_top)) > 0

    def bisect():
        lo, top = bounds

        def halve(_, state):
            lo, hi, reached = state
            mid = 0.5 * lo + 0.5 * hi
            cnt = reaching(mid)
            up = cnt >= k_top
            return jnp.where(up, mid, lo), jnp.where(up, hi, mid), jnp.where(up, cnt, reached)

        above = top + (jnp.maximum(top - lo, jnp.abs(top)) * 2.0 ** -10 + 1e-30)
        lo, hi, reached = lax.fori_loop(0, BISECT_STEPS, halve, (lo, above, jnp.full((1, tq), k_top + 1, I32)))

        def tied():
            def below(c, acc):
                c0 = pl.multiple_of(c * ck, ck)
                blk = score_scr[pl.ds(c0, ck), :]
                return jnp.maximum(acc, partials(jnp.where(blk < hi, blk, -jnp.inf), jnp.max))
            top = finish(lax.fori_loop(0, n_chunks, below, jnp.full((n_acc, 8, tq), -jnp.inf, F32)), jnp.max)
            settled = reached == k_top
            confirmed = jnp.min(jnp.where(settled | (reaching(top) >= k_top), 1, 0)) > 0
            return jnp.where(settled, lo, top), jnp.where(confirmed, CUT_RANKED, CUT_UNKNOWN)

        return lax.cond(unsettled(reached), tied, lambda: (lo, jnp.int32(CUT_EXACT)))

    thr, cut = lax.cond(try_bisect, bisect, lambda: (jnp.zeros((1, tq), F32), jnp.int32(CUT_UNKNOWN)))
    lax.switch(cut, [bitwise, lambda: exact_cut(thr), lambda: ranked_cut(thr)])


POS_SPLIT = 256


def _alibi_lanes(pos, slope=None):
    lane = lax.broadcasted_iota(I32, pos.shape, 1)
    lo = pos & (POS_SPLIT - 1)
    hi_f, lo_f = (pos - lo).astype(F32), lo.astype(F32)
    if slope is None:
        val = jnp.where(lane == 0, hi_f, jnp.where(lane == 1, lo_f, jnp.where(lane < 4, 1.0, 0.0)))
    else:
        val = jnp.where(lane < 2, slope, jnp.where(lane == 2, hi_f * -slope, jnp.where(lane == 3, lo_f * -slope, 0.0)))
    return val.astype(BF16)


def _dsa_kernel(qi_ref, ki2_ref, w_ref, qa_ref, ka_ref, vat_ref, tri_ref, o_ref,
                qm_scr, score_scr, bias_scr, kx_scr, qaug_scr, x_scr, m_scr, acc_scr, *, tq, ck, k_top, seq):
    q0 = pl.program_id(1) * tq
    n_chunks = (q0 + tq + ck - 1) // ck
    n_beyond = seq - n_chunks * ck

    zeros = jnp.zeros((tq, LANES), BF16)
    qpos = lax.broadcasted_iota(I32, (tq, LANES), 0) + q0
    for h in range(N_HEADS):
        pick = _head_pick(tq, h)
        qm_scr[h] = jnp.where(pick, qi_ref[:, _pair_lanes(h)], zeros)
        qaug_scr[h] = jnp.concatenate([jnp.where(pick, qa_ref[:, _pair_lanes(h)], zeros),
                                       _alibi_lanes(qpos, ALIBI_SLOPES[h])], axis=1)

    w = w_ref[...] * QK_SCALE
    krow = lax.broadcasted_iota(I32, (ck, tq), 0)
    qcol = lax.broadcasted_iota(I32, (ck, tq), 1)

    def visible(c0):
        return (krow + c0) <= (qcol + q0)

    def score_chunk(c, carry):
        lo, hi = carry
        c0 = pl.multiple_of(c * ck, ck)
        kblk = ki2_ref[pl.ds(c0, ck), :]
        acc = jnp.zeros((ck, tq), F32)
        for h in range(N_HEADS):
            acc = acc + jnp.maximum(_nt_dot(kblk, qm_scr[h]), 0.0) * w[h:h + 1, :]
        seen = visible(c0)
        score_scr[pl.ds(c0, ck), :] = jnp.where(seen, acc, MASK_VALUE)
        lo = jnp.minimum(lo, jnp.min(jnp.where(seen, acc, jnp.inf).reshape(ck // 8, 8, tq), axis=0))
        hi = jnp.maximum(hi, jnp.max(jnp.where(seen, acc, MASK_VALUE).reshape(ck // 8, 8, tq), axis=0))
        return lo, hi

    lo, hi = lax.fori_loop(0, n_chunks, score_chunk,
                           (jnp.full((8, tq), jnp.inf, F32), jnp.full((8, tq), -jnp.inf, F32)))
    bounds = (jnp.min(lo, axis=0, keepdims=True), jnp.max(hi, axis=0, keepdims=True))

    _selection_bias(score_scr, bias_scr, tri_ref, n_chunks, n_beyond, visible, ck=ck, tq=tq, k_top=k_top,
                    try_bisect=q0 >= k_top, bounds=bounds)

    _flash_init(m_scr, acc_scr)
    kpos0 = lax.broadcasted_iota(I32, (ck, LANES), 0)

    def attend_chunk(c, carry):
        c0 = pl.multiple_of(c * ck, ck)
        kx_scr[...] = _alibi_lanes(kpos0 + c0)
        k_aug = lambda h: jnp.concatenate([ka_ref[pl.ds(c0, ck), _pair_lanes(h)], kx_scr[...]], axis=1)
        _flash_chunk(k_aug, qaug_scr, lambda h: vat_ref[c, HEAD_DIM * h:HEAD_DIM * (h + 1), :],
                     x_scr, m_scr, acc_scr, bias=bias_scr[pl.ds(c0, ck), :])
        return carry

    lax.fori_loop(0, n_chunks, attend_chunk, 0)
    _flash_output(o_ref, acc_scr)


def _dsa_prompt(qi, ki2, small_t, qa, ka, va_t, tri, tq, ck):
    n_b, seq, _ = qi.shape
    nblk = seq // tq
    kern = functools.partial(_dsa_kernel, tq=tq, ck=ck, k_top=_topk_count(seq), seq=seq)
    return pl.pallas_call(
        kern,
        out_shape=jax.ShapeDtypeStruct((n_b, seq, WIDTH), BF16),
        grid=(n_b, nblk),
        in_specs=[pl.BlockSpec((None, tq, WIDTH), lambda b, i: (b, i, 0)),
                  pl.BlockSpec((None, seq, LANES), lambda b, i: (b, 0, 0)),
                  pl.BlockSpec((None, N_HEADS, tq), lambda b, i: (b, SM_WI // N_HEADS, i)),
                  pl.BlockSpec((None, tq, WIDTH), lambda b, i: (b, i, 0)),
                  pl.BlockSpec((None, seq, WIDTH), lambda b, i: (b, 0, 0)),
                  pl.BlockSpec((None, seq // ck, WIDTH, ck), lambda b, i: (b, 0, 0, 0)),
                  _const_spec((ck, ck))],
        out_specs=pl.BlockSpec((None, tq, WIDTH), lambda b, i: (b, i, 0)),
        scratch_shapes=[pltpu.VMEM((N_HEADS, tq, LANES), BF16), pltpu.VMEM((seq, tq), F32),
                        pltpu.VMEM((seq, tq), F32), pltpu.VMEM((ck, LANES), BF16)]
                       + _flash_scratch(N_HEADS, tq, ck),
        compiler_params=_params("arbitrary", "arbitrary"),
        name="dsa_prompt",
    )(qi, ki2, small_t, qa, ka, va_t, tri)


def _head_diag(n_rows):
    lane = lax.broadcasted_iota(I32, (n_rows, WIDTH), 1)
    sub = lax.broadcasted_iota(I32, (n_rows, WIDTH), 0)
    return (lane >= sub * HEAD_DIM) & (lane < (sub + 1) * HEAD_DIM)


def _sidx_kernel(pt_ref, q_ref, w_ref, knew_ref, *rest, n_seq, n_pages, page):
    page_refs, o_ref = rest[:n_seq * n_pages], rest[n_seq * n_pages]
    lane = lax.broadcasted_iota(I32, (1, LANES), 1)
    for g in range(n_seq):
        q = q_ref[g]
        qp = jnp.concatenate([q, jnp.zeros((BF16_ROWS - N_HEADS, HEAD_DIM), F32)], axis=0).astype(BF16)
        w = w_ref[g] * QK_SCALE
        for p in range(n_pages):
            s = _dot(qp, page_refs[g * n_pages + p][...].astype(BF16))[:N_HEADS]
            o_ref[g, :, page * p:page * (p + 1)] = jnp.sum(jnp.maximum(s, 0.0) * w, axis=0, keepdims=True)
        knew = knew_ref[g].astype(BF16).astype(F32)
        s_new = jnp.sum(q * knew, axis=1, keepdims=True)
        sc_new = jnp.sum(jnp.maximum(s_new, 0.0) * w, axis=0, keepdims=True)
        o_ref[g, :, page * n_pages:] = jnp.where(lane == 0, sc_new, 0.0)


def _sample_index_scores(layer, page_table, cache_k_idx, qi_s, wi_s, ki_s):
    n_db, n_pages = page_table.shape
    page = cache_k_idx.shape[2]
    l_pad = n_pages * page + LANES
    n_seq = next(g for g in (8, 4, 2, 1) if n_db % g == 0)
    page_spec = lambda g, p: pl.BlockSpec((None, None, HEAD_DIM, page),
                                          lambda b, pt: (layer, pt[b * n_seq + g, p], 0, 0))
    k_idx_t = jnp.swapaxes(cache_k_idx, 2, 3)
    grid_spec = pltpu.PrefetchScalarGridSpec(
        num_scalar_prefetch=1,
        grid=(n_db // n_seq,),
        in_specs=[pl.BlockSpec((n_seq, N_HEADS, HEAD_DIM), lambda b, pt: (b, 0, 0)),
                  pl.BlockSpec((n_seq, N_HEADS, 1), lambda b, pt: (b, 0, 0)),
                  pl.BlockSpec((n_seq, 1, HEAD_DIM), lambda b, pt: (b, 0, 0))]
                 + [page_spec(g, p) for g in range(n_seq) for p in range(n_pages)],
        out_specs=pl.BlockSpec((n_seq, 1, l_pad), lambda b, pt: (b, 0, 0)),
    )
    out = pl.pallas_call(
        functools.partial(_sidx_kernel, n_seq=n_seq, n_pages=n_pages, page=page),
        out_shape=jax.ShapeDtypeStruct((n_db, 1, l_pad), F32),
        grid_spec=grid_spec,
        compiler_params=_params("arbitrary"),
        name="sample_index_scores",
    )(page_table, qi_s.astype(F32).reshape(n_db, N_HEADS, HEAD_DIM), wi_s.reshape(n_db, N_HEADS, 1),
      ki_s.reshape(n_db, 1, HEAD_DIM), *([k_idx_t] * (n_seq * n_pages)))
    return out.reshape(n_db, l_pad)


def _ssel_kernel(s_ref, tri_ref, o_ref, score_scr, *, n_keys, ck, k_top):
    l_pad, tq = s_ref.shape
    krow = lax.broadcasted_iota(I32, (ck, tq), 0)

    def valid(c0):
        return (krow + c0) < n_keys

    def drop_padding(c, carry):
        c0 = pl.multiple_of(c * ck, ck)
        score_scr[pl.ds(c0, ck), :] = jnp.where(valid(c0), s_ref[pl.ds(c0, ck), :], -jnp.inf)
        return carry

    lax.fori_loop(0, l_pad // ck, drop_padding, 0)
    _selection_bias(score_scr, o_ref, tri_ref, l_pad // ck, 0, valid, ck=ck, tq=tq, k_top=k_top)


def _sample_selection(scores_t, tri, n_keys):
    l_pad, n_db = scores_t.shape
    return pl.pallas_call(
        functools.partial(_ssel_kernel, n_keys=n_keys, ck=LANES, k_top=_topk_count(n_keys)),
        out_shape=jax.ShapeDtypeStruct((l_pad, n_db), F32),
        grid=(1,),
        in_specs=[pl.BlockSpec((l_pad, n_db), lambda i: (0, 0)), pl.BlockSpec((LANES, LANES), lambda i: (0, 0))],
        out_specs=pl.BlockSpec((l_pad, n_db), lambda i: (0, 0)),
        scratch_shapes=[pltpu.VMEM((l_pad, n_db), F32)],
        compiler_params=_params("arbitrary"),
        name="sample_selection",
    )(scores_t, tri)


def _paged_attend(q_ref, knew_ref, vnew_ref, k_pages, v_pages, logit_terms, o_ref):
    n_pages, page = len(k_pages), k_pages[0].shape[-1]
    diag = _head_diag(N_HEADS)
    qs = jnp.where(_head_diag(BF16_ROWS), jnp.broadcast_to(q_ref[...], (BF16_ROWS, WIDTH)), 0.0)
    qb = qs.astype(BF16)
    s = jnp.concatenate([_dot(qb, k_pages[p][...].astype(BF16))[:N_HEADS] for p in range(n_pages)], axis=1)
    s_new = jnp.sum(qs[:N_HEADS] * knew_ref[...].astype(BF16).astype(F32), axis=1, keepdims=True)
    x, x_new = logit_terms(s, s_new)
    m = jnp.maximum(jnp.max(x, axis=1, keepdims=True), x_new)
    p = jnp.exp(x - m)
    p_new = jnp.exp(x_new - m)
    denom = jnp.sum(p, axis=1, keepdims=True) + p_new
    pb = jnp.concatenate([p, jnp.zeros_like(p)], axis=0).astype(BF16)
    acc = p_new.astype(BF16).astype(F32) * vnew_ref[...].astype(BF16).astype(F32)
    for q in range(n_pages):
        acc = acc + _nt_dot(pb[:, page * q:page * (q + 1)], v_pages[q][...].astype(BF16))[:N_HEADS]
    o_ref[...] = jnp.sum(jnp.where(diag, acc / denom, 0.0), axis=0, keepdims=True)


def _sattn_kernel(pt_ref, qa_ref, qb_ref, kan_ref, van_ref, kbn_ref, vbn_ref, lfn_ref, bias_ref, slope_ref, *rest,
                  n_pages):
    ka_p, va_p, kb_p, vb_p, lf_p = (rest[g * n_pages:(g + 1) * n_pages] for g in range(5))
    oa_ref, ob_ref = rest[5 * n_pages:]
    past = n_pages * ka_p[0].shape[-1]

    def alibi_and_mask(s, s_new):
        rel = (lax.broadcasted_iota(I32, (1, past), 1) - past).astype(F32)
        bias = bias_ref[...]
        return s + slope_ref[...] * rel + bias[:, :past], s_new + bias[:, past:past + 1]

    _paged_attend(qa_ref, kan_ref, van_ref, ka_p, va_p, alibi_and_mask, oa_ref)

    lf = jnp.concatenate([lf_p[p][...] for p in range(n_pages)], axis=1)
    idx = lax.broadcasted_iota(I32, lf.shape, 1)
    d = 1
    while d < past:
        lf = lf + jnp.where(idx >= d, pltpu.roll(lf, d, 1), 0.0)
        d *= 2
    f_new = lf[:, past - 1:past] + lfn_ref[...]
    _paged_attend(qb_ref, kbn_ref, vbn_ref, kb_p, vb_p, lambda s, s_new: (s + (f_new - lf), s_new), ob_ref)


def _sample_attention(layer, page_table, caches, logf_t, qa_s, qb_s, new_rows, logf_new, bias):
    cache_k_a, cache_v_a, cache_k_b, cache_v_b = caches
    n_db, n_pages = page_table.shape
    n_pool, page = cache_k_a.shape[1], cache_k_a.shape[2]
    l_pad = bias.shape[1]
    flat = lambda c: jnp.transpose(c, (0, 1, 3, 4, 2)).reshape(c.shape[0], n_pool, WIDTH, page)
    row = lambda width: pl.BlockSpec((None, 1, width), lambda b, pt: (b, 0, 0))
    kv_spec = lambda p: pl.BlockSpec((None, None, WIDTH, page), lambda b, pt: (layer, pt[b, p], 0, 0))
    lf_spec = lambda p: pl.BlockSpec((None, N_HEADS, page), lambda b, pt: (pt[b, p], 0, 0))
    pages = range(n_pages)
    grid_spec = pltpu.PrefetchScalarGridSpec(
        num_scalar_prefetch=1,
        grid=(n_db,),
        in_specs=[row(WIDTH), row(WIDTH), row(WIDTH), row(WIDTH), row(WIDTH), row(WIDTH),
                  pl.BlockSpec((None, N_HEADS, 1), lambda b, pt: (b, 0, 0)), row(l_pad),
                  pl.BlockSpec((N_HEADS, 1), lambda b, pt: (0, 0))]
                 + [kv_spec(p) for p in pages] * 4 + [lf_spec(p) for p in pages],
        out_specs=(row(WIDTH), row(WIDTH)),
    )
    r3 = lambda a: a.reshape(n_db, 1, a.shape[-1])
    slopes = jnp.asarray(ALIBI_SLOPES, F32).reshape(N_HEADS, 1)
    return pl.pallas_call(
        functools.partial(_sattn_kernel, n_pages=n_pages),
        out_shape=(jax.ShapeDtypeStruct((n_db, 1, WIDTH), F32), jax.ShapeDtypeStruct((n_db, 1, WIDTH), F32)),
        grid_spec=grid_spec,
        compiler_params=_params("arbitrary"),
        name="sample_attention",
    )(page_table, r3(qa_s.astype(F32)), r3(qb_s.astype(F32)), *[r3(a) for a in new_rows],
      logf_new.reshape(n_db, N_HEADS, 1), r3(bias), slopes,
      *([flat(cache_k_a)] * n_pages), *([flat(cache_v_a)] * n_pages), *([flat(cache_k_b)] * n_pages),
      *([flat(cache_v_b)] * n_pages), *([logf_t] * n_pages))


def _fox_kernel(q_ref, qx_ref, k_ref, kx_ref, vt_ref, o_ref, qaug_scr, x_scr, m_scr, acc_scr, *, tq, ck):
    q0 = pl.program_id(1) * tq
    n_chunks = (q0 + tq + ck - 1) // ck
    lane = lax.broadcasted_iota(I32, (tq, LANES), 1)
    zeros = jnp.zeros((tq, LANES), BF16)
    qx = qx_ref[...]
    for h in range(N_HEADS):
        own = (lane >= FX * h) & (lane < FX * (h + 1))
        qaug_scr[h] = jnp.concatenate([jnp.where(_head_pick(tq, h), q_ref[:, _pair_lanes(h)], zeros),
                                       jnp.where(own, qx, zeros)], axis=1)
    _flash_init(m_scr, acc_scr)
    krow = lax.broadcasted_iota(I32, (ck, tq), 0)
    qcol = lax.broadcasted_iota(I32, (ck, tq), 1)

    def chunk(c, masked):
        c0 = pl.multiple_of(c * ck, ck)
        k_aug = lambda h: jnp.concatenate([k_ref[pl.ds(c0, ck), _pair_lanes(h)], kx_ref[pl.ds(c0, ck), :]], axis=1)
        visible = ((krow + c0) <= (qcol + q0)) if masked else None
        _flash_chunk(k_aug, qaug_scr, lambda h: vt_ref[c, HEAD_DIM * h:HEAD_DIM * (h + 1), :],
                     x_scr, m_scr, acc_scr, visible=visible)

    def full_chunk(c, carry):
        chunk(c, False)
        return carry

    lax.fori_loop(0, n_chunks - 1, full_chunk, 0)
    chunk(n_chunks - 1, True)
    _flash_output(o_ref, acc_scr)


def _fox_prompt(qb, qx, kb, kx, vb_t, tq, ck):
    n_b, seq, _ = qb.shape
    per_q = lambda width: pl.BlockSpec((None, tq, width), lambda b, i: (b, i, 0))
    whole = lambda width: pl.BlockSpec((None, seq, width), lambda b, i: (b, 0, 0))
    return pl.pallas_call(
        functools.partial(_fox_kernel, tq=tq, ck=ck),
        out_shape=jax.ShapeDtypeStruct((n_b, seq, WIDTH), BF16),
        grid=(n_b, seq // tq),
        in_specs=[per_q(WIDTH), per_q(LANES), whole(WIDTH), whole(LANES),
                  pl.BlockSpec((None, seq // ck, WIDTH, ck), lambda b, i: (b, 0, 0, 0))],
        out_specs=per_q(WIDTH),
        scratch_shapes=_flash_scratch(N_HEADS, tq, ck),
        compiler_params=_params("arbitrary", "arbitrary"),
        name="fox_prompt",
    )(qb, qx, kb, kx, vb_t)


def _post_kernel(x_ref, oa_ref, ob_ref, sga_ref, sgb_ref, gate1_ref, shift2_ref, scale2_ref, gate2_ref, g2_ref,
                 woa_ref, wob_ref, wout_ref, wup_ref, wdn_ref, y_ref, *, ff_chunk):
    merged = (sga_ref[...].astype(F32) * _dot(oa_ref[...], woa_ref[...])
              + sgb_ref[...].astype(F32) * _dot(ob_ref[...], wob_ref[...]))
    x1 = x_ref[...] + gate1_ref[...] * _dot(merged.astype(BF16), wout_ref[...])
    h2 = x1 * lax.rsqrt(jnp.mean(x1 * x1, axis=-1, keepdims=True) + EPS) * g2_ref[...]
    h2 = (h2 * (1.0 + scale2_ref[...]) + shift2_ref[...]).astype(BF16)
    mlp = jnp.zeros(x1.shape, F32)
    for c in range(0, D_FF, ff_chunk):
        u = jnp.maximum(_dot(h2, wup_ref[:, c:c + ff_chunk]), 0.0)
        mlp = mlp + _dot((u * u).astype(BF16), wdn_ref[c:c + ff_chunk, :])
    y_ref[...] = x1 + gate2_ref[...] * mlp


def _post(x, oa, ob, sga, sgb, mod, g2, woa, wob, wout, wup, wdn, tm):
    n_g, seq, _ = x.shape
    n_r = mod.shape[1]
    rb = 1 if n_r == 1 else tm
    row = lambda width: pl.BlockSpec((None, tm, width), lambda g, i: (g, i, 0))
    modspec = lambda j: pl.BlockSpec((None, rb, D_MODEL), lambda g, i: (g, i if n_r > 1 else 0, j))
    return pl.pallas_call(
        functools.partial(_post_kernel, ff_chunk=1024),
        out_shape=jax.ShapeDtypeStruct((n_g, seq, D_MODEL), F32),
        grid=(n_g, seq // tm),
        in_specs=[row(D_MODEL), row(WIDTH), row(WIDTH), row(D_MODEL), row(D_MODEL),
                  modspec(2), modspec(3), modspec(4), modspec(5), _const_spec((1, D_MODEL)),
                  _const_spec((WIDTH, D_MODEL)), _const_spec((WIDTH, D_MODEL)), _const_spec((D_MODEL, D_MODEL)),
                  _const_spec((D_MODEL, D_FF)), _const_spec((D_FF, D_MODEL))],
        out_specs=row(D_MODEL),
        compiler_params=_params("arbitrary", "arbitrary"),
        name="merge_out_mlp",
    )(x, oa, ob, sga, sgb, mod, mod, mod, mod, g2, woa, wob, wout, wup, wdn)


def _lower_tri(n):
    return jnp.asarray(np.tril(np.ones((n, n), np.float32)), BF16)


def _layer(l, x_p, x_s, caches, cache_k_idx, logf_t, page_table, mod_p, mod_s, lw, tiles):
    (g_norm1, w_r, gains, bsm, gmat, woa, wob, wout, g_norm2, wup, wdn) = lw
    tm, tq = tiles
    n_b, seq, _ = x_p.shape
    n_db = x_s.shape[0]

    (ka_p, va_p, kb_p, vb_p, small_p, qa_bf, ka_bf, vat_bf, qb_bf, kb_bf, vbt_bf, qi_bf, ki2_bf, smallt_p,
     sga_p, sgb_p, kx_b, qx_b) = _in_proj(x_p, mod_p, g_norm1, w_r, gains, bsm, gmat, tm)
    xs = x_s.reshape(1, n_db, D_MODEL)
    (ka_s, va_s, kb_s, vb_s, small_s, qa_s, _, _, qb_s, _, _, qi_s, _, _, sga_s, sgb_s, _, _) = _in_proj(
        xs, mod_s, g_norm1, w_r, gains, bsm, gmat, n_db)
    small_s2 = small_s[0]
    ki_s, wi_s = small_s2[:, :SM_WI], small_s2[:, SM_WI:SM_LOGF]
    logf_s = small_s2[:, SM_LOGF:SM_LOGF + N_HEADS]

    n_keys = page_table.shape[1] * cache_k_idx.shape[2] + 1
    scores = _sample_index_scores(l, page_table, cache_k_idx, qi_s[0], wi_s, ki_s)
    bias_t = _sample_selection(scores.T, _lower_tri(LANES), n_keys)

    oa_s, ob_s = _sample_attention(l, page_table, caches, logf_t, qa_s[0], qb_s[0],
                                   (ka_s[0], va_s[0], kb_s[0], vb_s[0]), logf_s, bias_t.T)
    ob_p = _fox_prompt(qb_bf, qx_b, kb_bf, kx_b, vbt_bf, tq, tm)
    oa_p = _dsa_prompt(qi_bf, ki2_bf, smallt_p, qa_bf, ka_bf, vat_bf, _lower_tri(tm), tq, tm)

    y_p = _post(x_p, oa_p, ob_p, sga_p, sgb_p, mod_p, g_norm2, woa, wob, wout, wup, wdn, tm)
    y_s = _post(xs, oa_s.reshape(1, n_db, WIDTH).astype(BF16), ob_s.reshape(1, n_db, WIDTH).astype(BF16),
                sga_s, sgb_s, mod_s, g_norm2, woa, wob, wout, wup, wdn, n_db)

    heads = lambda a, lead: a.reshape(*lead, N_HEADS, HEAD_DIM)
    lp, ls = (n_b, seq), (n_db, 1)
    from_t = lambda lo, hi: jnp.swapaxes(smallt_p[:, lo:hi, :], 1, 2)
    state_p = (heads(ka_p, lp), heads(va_p, lp), from_t(0, SM_WI), heads(kb_p, lp), heads(vb_p, lp),
               from_t(SM_LOGF, SM_LOGF + N_HEADS))
    state_s = (heads(ka_s[0], ls), heads(va_s[0], ls), ki_s.reshape(n_db, 1, HEAD_DIM), heads(kb_s[0], ls),
               heads(vb_s[0], ls), logf_s.reshape(n_db, 1, N_HEADS))
    return y_p, y_s.reshape(n_db, 1, D_MODEL), state_p, state_s


def kernel(x_prompt, x_sample, cache_k_a, cache_v_a, cache_k_idx, cache_k_b, cache_v_b, cache_logf_b, page_table,
           c_prompt, c_sample, w_ada, b_ada, g_norm1, w_in, b_forget, g_qa, g_ka, g_qb, g_kb, w_o_a, w_o_b, w_out,
           g_norm2, w_up, w_down):
    depth = w_in.shape[0]
    n_b, seq, _ = x_prompt.shape
    n_db, dec_seq, _ = x_sample.shape
    assert dec_seq == 1 and x_prompt.shape[-1] == D_MODEL
    assert cache_k_a.shape[3:] == (N_HEADS, HEAD_DIM) and cache_k_idx.shape[3] == HEAD_DIM
    tiles = (min(512, seq), min(256, seq))

    c_all = jnp.concatenate([c_prompt, c_sample], axis=0)
    pad = (-c_all.shape[0]) % 8
    c_all = jnp.pad(c_all, ((0, pad), (0, 0)))
    gmat = jnp.asarray(np.kron(np.eye(N_HEADS, dtype=np.float32),
                               np.full((HEAD_DIM, HEAD_DIM), 1.0 / HEAD_DIM, np.float32)), BF16)
    split = 7 * WIDTH
    y_p, y_s = x_prompt, x_sample.reshape(n_db, D_MODEL)
    states_p, states_s = [], []
    for l in range(depth):
        mod = _ada_mod(c_all, w_ada[l], b_ada[l])
        mod_p = mod[:n_b].reshape(n_b, 1, 6 * D_MODEL)
        mod_s = mod[n_b:n_b + n_db].reshape(1, n_db, 6 * D_MODEL)
        w = w_in[l]
        k_idx_cols = w[:, split:split + HEAD_DIM]
        w_r = jnp.concatenate(
            [w[:, :split], w[:, split:split + HEAD_DIM + 2 * N_HEADS],
             jnp.zeros((D_MODEL, LANES - HEAD_DIM - 2 * N_HEADS), w.dtype),
             k_idx_cols, k_idx_cols, w[:, split + HEAD_DIM + 2 * N_HEADS:]], axis=1).astype(BF16)
        gains = jnp.stack([jnp.tile(g[l], N_HEADS) for g in (g_qa, g_ka, g_qb, g_kb)])
        bsm = jnp.zeros((1, LANES), F32).at[0, SM_LOGF:SM_LOGF + N_HEADS].set(b_forget[l])
        lw = (g_norm1[l].reshape(1, D_MODEL), w_r, gains, bsm, gmat, w_o_a[l].astype(BF16), w_o_b[l].astype(BF16),
              w_out[l].astype(BF16), g_norm2[l].reshape(1, D_MODEL), w_up[l].astype(BF16), w_down[l].astype(BF16))
        logf_t = jnp.swapaxes(cache_logf_b[l], 1, 2)
        y_p, y_s, st_p, st_s = _layer(l, y_p, y_s, (cache_k_a, cache_v_a, cache_k_b, cache_v_b), cache_k_idx,
                                      logf_t, page_table, mod_p, mod_s, lw, tiles)
        y_s = y_s.reshape(n_db, D_MODEL)
        states_p.append(st_p)
        states_s.append(st_s)
    k_a_p, v_a_p, k_idx_p, k_b_p, v_b_p, logf_p = (jnp.stack(s) for s in zip(*states_p))
    k_a_s, v_a_s, k_idx_s, k_b_s, v_b_s, logf_s = (jnp.stack(s) for s in zip(*states_s))
    return (y_p, y_s.reshape(n_db, 1, D_MODEL), k_a_p, v_a_p, k_idx_p, k_b_p, v_b_p, logf_p,
            k_a_s, v_a_s, k_idx_s, k_b_s, v_b_s, logf_s)
```

```python
import functools

import numpy as np
import jax
import jax.numpy as jnp
from jax import lax
from jax.experimental import pallas as pl
from jax.experimental.pallas import tpu as pltpu

F32 = jnp.float32
BF16 = jnp.bfloat16
I32 = jnp.int32

D_MODEL = 1024
HEAD_DIM = 64
N_HEADS = 8
WIDTH = N_HEADS * HEAD_DIM
D_FF = 4 * D_MODEL
TOPK_MAX = 256
EPS = 1e-6
MASK_VALUE = -1e30
LANES = 128
BF16_ROWS = 16
QK_SCALE = HEAD_DIM ** -0.5
WI_SCALE = N_HEADS ** -0.5
ALIBI_SLOPES = tuple(2.0 ** (-8.0 * (h + 1) / N_HEADS) for h in range(N_HEADS))
INT_MIN = -(2 ** 31)
VMEM_LIMIT = 56 * 1024 * 1024

C_QA, C_KA, C_VA, C_QB, C_KB, C_VB, C_QI = (i * WIDTH for i in range(7))
C_SMALL = 7 * WIDTH
C_KI2 = C_SMALL + LANES
C_GA = C_KI2 + LANES
C_GB = C_GA + D_MODEL
C_TOTAL = C_GB + D_MODEL
SM_WI = HEAD_DIM
SM_LOGF = HEAD_DIM + N_HEADS


def _topk_count(n_keys):
    return max(1, min(TOPK_MAX, n_keys // 4))


def _const_spec(shape):
    zeros = (0,) * len(shape)
    return pl.BlockSpec(shape, lambda *_: zeros, pipeline_mode=pl.Buffered(1))


def _params(*sem):
    return pltpu.CompilerParams(dimension_semantics=sem, vmem_limit_bytes=VMEM_LIMIT)


def _nt_dot(a, b):
    return lax.dot_general(a, b, (((1,), (1,)), ((), ())), preferred_element_type=F32)


def _dot(a, b):
    return jnp.dot(a, b, preferred_element_type=F32)


def _split3(x):
    p1 = x.astype(BF16)
    r1 = x - p1.astype(F32)
    p2 = r1.astype(BF16)
    p3 = (r1 - p2.astype(F32)).astype(BF16)
    return p1, p2, p3


def _ada_kernel(c_ref, w_ref, b_ref, o_ref):
    c = c_ref[...]
    s = c * jax.nn.sigmoid(c)
    o_ref[...] = _dot(s.astype(BF16), w_ref[...].astype(BF16)) + b_ref[...]


def _ada_mod(c, w_ada, b_ada):
    rows = c.shape[0]
    n_out = w_ada.shape[1]
    tn = D_MODEL
    return pl.pallas_call(
        _ada_kernel,
        out_shape=jax.ShapeDtypeStruct((rows, n_out), F32),
        grid=(n_out // tn,),
        in_specs=[pl.BlockSpec((rows, D_MODEL), lambda j: (0, 0)),
                  pl.BlockSpec((D_MODEL, tn), lambda j: (0, j)),
                  pl.BlockSpec((1, tn), lambda j: (0, j))],
        out_specs=pl.BlockSpec((rows, tn), lambda j: (0, j)),
        compiler_params=_params("arbitrary"),
        name="ada_mod",
    )(c, w_ada, b_ada.reshape(1, n_out))


def _log_sigmoid(x):
    return jnp.minimum(x, 0.0) - jnp.log(1.0 + jnp.exp(-jnp.abs(x)))


FX = 6


def _forget_placement():
    place = np.zeros((3, LANES, 2 * LANES), np.float32)
    const = np.zeros((1, 2 * LANES), np.float32)
    for h in range(N_HEADS):
        for i in range(3):
            place[i, SM_LOGF + h, FX * h + i] = 1.0
            place[i, SM_LOGF + h, LANES + FX * h + 3 + i] = 1.0
            const[0, FX * h + 3 + i] = 1.0
            const[0, LANES + FX * h + i] = -1.0
    return jnp.asarray(place, BF16), jnp.asarray(const, F32)


def _in_proj_kernel(x_ref, shift_ref, scale_ref, g1_ref, w_ref, gains_ref, bsm_ref, gmat_ref,
                    tri_ref, place_ref, fconst_ref,
                    ka_ref, va_ref, kb_ref, vb_ref, small_ref,
                    qa_bf, ka_bf, vat_bf, qb_bf, kb_bf, vbt_bf, qi_bf, ki2_bf, smallt_ref, sga_ref, sgb_ref,
                    kx_ref, qx_ref, carry_ref):
    x = x_ref[...]
    h = x * lax.rsqrt(jnp.mean(x * x, axis=-1, keepdims=True) + EPS) * g1_ref[...]
    h = h * (1.0 + scale_ref[...]) + shift_ref[...]
    hb = h.astype(BF16)

    def seg(col, width=WIDTH):
        return _dot(hb, w_ref[:, col:col + width])

    def head_norm(z, gi):
        msq = _dot((z * z).astype(BF16), gmat_ref[...])
        return z * lax.rsqrt(msq + EPS) * gains_ref[gi:gi + 1, :]

    qa = head_norm(seg(C_QA), 0)
    qa_bf[...] = (qa * QK_SCALE).astype(BF16)
    ka = head_norm(seg(C_KA), 1)
    ka_ref[...] = ka
    ka_bf[...] = ka.astype(BF16)
    va = seg(C_VA)
    va_ref[...] = va
    vat_bf[...] = va.T.astype(BF16)
    qb = head_norm(seg(C_QB), 2)
    qb_bf[...] = (qb * QK_SCALE).astype(BF16)
    kb = head_norm(seg(C_KB), 3)
    kb_ref[...] = kb
    kb_bf[...] = kb.astype(BF16)
    vb = seg(C_VB)
    vb_ref[...] = vb
    vbt_bf[...] = vb.T.astype(BF16)
    qi_bf[...] = seg(C_QI).astype(BF16)
    ki2_bf[...] = seg(C_KI2, LANES).astype(BF16)

    zs = seg(C_SMALL, LANES)
    lane = lax.broadcasted_iota(I32, zs.shape, 1)
    logf = _log_sigmoid(zs + bsm_ref[...])
    small = jnp.where(lane < SM_WI, zs,
                      jnp.where(lane < SM_LOGF, zs * WI_SCALE,
                                jnp.where(lane < SM_LOGF + N_HEADS, logf, 0.0)))
    small_ref[...] = small
    smallt_ref[...] = small.T

    @pl.when(pl.program_id(1) == 0)
    def _():
        carry_ref[...] = jnp.zeros_like(carry_ref)

    tri = tri_ref[...]
    p1, p2, p3 = _split3(small)
    cs = _dot(tri, p1) + _dot(tri, p2) + _dot(tri, p3) + carry_ref[...]
    n = cs.shape[0]
    carry_ref[...] = cs[n - 1:n, :]
    f1, f2, f3 = _split3(cs)
    ext = _dot(f1, place_ref[0]) + _dot(f2, place_ref[1]) + _dot(f3, place_ref[2]) + fconst_ref[...]
    kx_ref[...] = ext[:, :LANES].astype(BF16)
    qx_ref[...] = ext[:, LANES:].astype(BF16)

    sga_ref[...] = jax.nn.sigmoid(seg(C_GA, D_MODEL)).astype(BF16)
    sgb_ref[...] = jax.nn.sigmoid(seg(C_GB, D_MODEL)).astype(BF16)


def _in_proj(x, mod, g1, w_r, gains, bsm, gmat, tm):
    n_g, seq, _ = x.shape
    n_r = mod.shape[1]
    rb = 1 if n_r == 1 else tm
    nblk = seq // tm
    row = lambda width: pl.BlockSpec((None, tm, width), lambda g, i: (g, i, 0))
    modspec = lambda j: pl.BlockSpec((None, rb, D_MODEL), lambda g, i: (g, i if n_r > 1 else 0, j))
    sds = lambda shape, dt: jax.ShapeDtypeStruct(shape, dt)
    out_shape = (
        sds((n_g, seq, WIDTH), F32), sds((n_g, seq, WIDTH), F32), sds((n_g, seq, WIDTH), F32),
        sds((n_g, seq, WIDTH), F32), sds((n_g, seq, LANES), F32),
        sds((n_g, seq, WIDTH), BF16), sds((n_g, seq, WIDTH), BF16), sds((n_g, nblk, WIDTH, tm), BF16),
        sds((n_g, seq, WIDTH), BF16), sds((n_g, seq, WIDTH), BF16), sds((n_g, nblk, WIDTH, tm), BF16),
        sds((n_g, seq, WIDTH), BF16), sds((n_g, seq, LANES), BF16), sds((n_g, LANES, seq), F32),
        sds((n_g, seq, D_MODEL), BF16), sds((n_g, seq, D_MODEL), BF16),
        sds((n_g, seq, LANES), BF16), sds((n_g, seq, LANES), BF16),
    )
    chunk_t = pl.BlockSpec((None, None, WIDTH, tm), lambda g, i: (g, i, 0, 0))
    out_specs = (
        row(WIDTH), row(WIDTH), row(WIDTH), row(WIDTH), row(LANES),
        row(WIDTH), row(WIDTH), chunk_t,
        row(WIDTH), row(WIDTH), chunk_t,
        row(WIDTH), row(LANES), pl.BlockSpec((None, LANES, tm), lambda g, i: (g, 0, i)),
        row(D_MODEL), row(D_MODEL),
        row(LANES), row(LANES),
    )
    place, fconst = _forget_placement()
    return pl.pallas_call(
        _in_proj_kernel,
        out_shape=out_shape,
        grid=(n_g, nblk),
        in_specs=[row(D_MODEL), modspec(0), modspec(1), _const_spec((1, D_MODEL)),
                  _const_spec((D_MODEL, C_TOTAL)), _const_spec((4, WIDTH)), _const_spec((1, LANES)),
                  _const_spec((WIDTH, WIDTH)), _const_spec((tm, tm)), _const_spec((3, LANES, 2 * LANES)),
                  _const_spec((1, 2 * LANES))],
        out_specs=out_specs,
        scratch_shapes=[pltpu.VMEM((1, LANES), F32)],
        compiler_params=_params("arbitrary", "arbitrary"),
        name="in_proj",
    )(x, mod, mod, g1, w_r, gains, bsm, gmat, _lower_tri(tm), place, fconst)


def _flash_chunk(k_aug, qaug_scr, vt_of, x_scr, m_scr, acc_scr, bias=None, visible=None, row0=0):
    n_slots, ck = x_scr.shape[0], x_scr.shape[1]
    ones = jnp.ones((BF16_ROWS, ck), BF16)
    alphas = {}
    rows = slice(None) if isinstance(row0, int) else pl.ds(pl.multiple_of(row0, ck), ck)

    def logits(s):
        x = _nt_dot(k_aug(s), qaug_scr[s])
        if bias is not None:
            x = x + bias
        if visible is not None:
            x = jnp.where(visible, x, MASK_VALUE)
        x_scr[s, rows, :] = x
        m_old = m_scr[s]
        m_new = jnp.maximum(m_old, jnp.max(x, axis=0, keepdims=True))
        alphas[s] = jnp.exp(m_old - m_new)
        m_scr[s] = m_new

    def values(s):
        p = jnp.exp(x_scr[s, rows, :] - m_scr[s]).astype(BF16)
        acc_scr[s] = alphas[s] * acc_scr[s] + _dot(jnp.concatenate([vt_of(s), ones], axis=0), p)

    for phase in (logits, values):
        for s in range(n_slots):
            phase(s)


def _flash_scratch(n_slots, tq, ck):
    return [pltpu.VMEM((n_slots, tq, 2 * LANES), BF16), pltpu.VMEM((n_slots, ck, tq), F32),
            pltpu.VMEM((n_slots, 1, tq), F32), pltpu.VMEM((n_slots, HEAD_DIM + BF16_ROWS, tq), F32)]


def _flash_init(m_scr, acc_scr):
    m_scr[...] = jnp.full(m_scr.shape, MASK_VALUE, F32)
    acc_scr[...] = jnp.zeros_like(acc_scr)


def _flash_output(o_ref, acc_scr):
    out_t = jnp.concatenate([acc_scr[s, :HEAD_DIM] / acc_scr[s, HEAD_DIM:HEAD_DIM + 1]
                             for s in range(acc_scr.shape[0])], axis=0)
    o_ref[...] = out_t.T.astype(BF16)


def _head_pick(tq, h):
    lane = lax.broadcasted_iota(I32, (tq, LANES), 1)
    return (lane >= HEAD_DIM) == (h % 2 == 1)


def _pair_lanes(h):
    return slice(LANES * (h // 2), LANES * (h // 2 + 1))


def _ordered_float(u):
    key = u ^ jnp.int32(INT_MIN)
    return pltpu.bitcast(key ^ ((key >> 31) & jnp.int32(0x7FFFFFFF)), F32)


BISECT_STEPS = 22
CUT_UNKNOWN, CUT_EXACT, CUT_RANKED = 0, 1, 2


def _selection_bias(score_scr, bias_scr, tri_ref, n_chunks, n_beyond, valid_fn, *, ck, tq, k_top, try_bisect=None,
                    bounds=None):
    n_acc = 4

    def partials(blk, reduce):
        return reduce(blk.reshape(ck // (8 * n_acc), n_acc, 8, tq), axis=0)

    def finish(acc, reduce):
        return reduce(reduce(acc, axis=0), axis=0, keepdims=True)

    def count(cmp):
        def body(c, acc):
            c0 = pl.multiple_of(c * ck, ck)
            return acc + partials(jnp.where(cmp(score_scr[pl.ds(c0, ck), :]), 1, 0), jnp.sum)
        return finish(lax.fori_loop(0, n_chunks, body, jnp.zeros((n_acc, 8, tq), I32)), jnp.sum)

    def reaching(x):
        return count(lambda blk: blk >= x) + jnp.where(x <= MASK_VALUE, n_beyond, 0)

    def exact_cut(thr):
        def body(c, carry):
            c0 = pl.multiple_of(c * ck, ck)
            keep = jnp.where(score_scr[pl.ds(c0, ck), :] >= thr, 0.0, MASK_VALUE)
            bias_scr[pl.ds(c0, ck), :] = jnp.where(valid_fn(c0), keep, MASK_VALUE)
            return carry
        lax.fori_loop(0, n_chunks, body, 0)

    def ranked_cut(thr):
        n_greater = count(lambda blk: blk > thr) + jnp.where(thr < MASK_VALUE, n_beyond, 0)
        ties_wanted = (k_top - n_greater).astype(F32)

        def body(c, ties_before):
            c0 = pl.multiple_of(c * ck, ck)
            blk = score_scr[pl.ds(c0, ck), :]
            tie = blk == thr
            rank = _dot(tri_ref[...], jnp.where(tie, 1.0, 0.0).astype(BF16)) + ties_before
            keep = jnp.where(blk > thr, 0.0,
                             jnp.where(tie, jnp.where(rank <= ties_wanted, 0.0, MASK_VALUE), MASK_VALUE))
            bias_scr[pl.ds(c0, ck), :] = jnp.where(valid_fn(c0), keep, MASK_VALUE)
            return rank[ck - 1:ck, :]
        lax.fori_loop(0, n_chunks, body, jnp.zeros((1, tq), F32))

    def bitwise():
        def bit_step(i, carry):
            code, reached = carry
            cand_code = code | lax.shift_left(jnp.int32(1), 31 - i)
            cnt = reaching(_ordered_float(cand_code))
            take = cnt >= k_top
            return jnp.where(take, cand_code, code), jnp.where(take, cnt, reached)

        code, reached = lax.fori_loop(0, 32, bit_step, (jnp.zeros((1, tq), I32), jnp.full((1, tq), k_top, I32)))
        thr = _ordered_float(code)
        lax.cond(jnp.max(reached) > k_top, lambda: ranked_cut(thr), lambda: exact_cut(thr))

    if try_bisect is None:
        bitwise()
        return

    def unsettled(reached):
        return jnp.max(jnp.abs(reached - k_top)) > 0

    def bisect():
        lo, top = bounds

        def halve(_, state):
            lo, hi, reached = state
            mid = 0.5 * lo + 0.5 * hi
            cnt = reaching(mid)
            up = cnt >= k_top
            return jnp.where(up, mid, lo), jnp.where(up, hi, mid), jnp.where(up, cnt, reached)

        above = top + (jnp.maximum(top - lo, jnp.abs(top)) * 2.0 ** -10 + 1e-30)
        lo, hi, reached = lax.fori_loop(0, BISECT_STEPS, halve, (lo, above, jnp.full((1, tq), k_top + 1, I32)))

        def tied():
            def below(c, acc):
                c0 = pl.multiple_of(c * ck, ck)
                blk = score_scr[pl.ds(c0, ck), :]
                return jnp.maximum(acc, partials(jnp.where(blk < hi, blk, -jnp.inf), jnp.max))
            top = finish(lax.fori_loop(0, n_chunks, below, jnp.full((n_acc, 8, tq), -jnp.inf, F32)), jnp.max)
            settled = reached == k_top
            confirmed = jnp.min(jnp.where(settled | (reaching(top) >= k_top), 1, 0)) > 0
            return jnp.where(settled, lo, top), jnp.where(confirmed, CUT_RANKED, CUT_UNKNOWN)

        return lax.cond(unsettled(reached), tied, lambda: (lo, jnp.int32(CUT_EXACT)))

    thr, cut = lax.cond(try_bisect, bisect, lambda: (jnp.zeros((1, tq), F32), jnp.int32(CUT_UNKNOWN)))
    lax.switch(cut, [bitwise, lambda: exact_cut(thr), lambda: ranked_cut(thr)])


POS_SPLIT = 256


def _alibi_lanes(pos, slope=None):
    lane = lax.broadcasted_iota(I32, pos.shape, 1)
    lo = pos & (POS_SPLIT - 1)
    hi_f, lo_f = (pos - lo).astype(F32), lo.astype(F32)
    if slope is None:
        val = jnp.where(lane == 0, hi_f, jnp.where(lane == 1, lo_f, jnp.where(lane < 4, 1.0, 0.0)))
    else:
        val = jnp.where(lane < 2, slope, jnp.where(lane == 2, hi_f * -slope, jnp.where(lane == 3, lo_f * -slope, 0.0)))
    return val.astype(BF16)


def _dsa_kernel(qi_ref, ki2_ref, w_ref, qa_ref, ka_ref, vat_ref, tri_ref, o_ref,
                qm_scr, score_scr, bias_scr, kx_scr, qaug_scr, x_scr, m_scr, acc_scr, *, tq, ck, k_top, seq):
    q0 = pl.program_id(1) * tq
    n_chunks = (q0 + tq + ck - 1) // ck
    n_beyond = seq - n_chunks * ck

    zeros = jnp.zeros((tq, LANES), BF16)
    qpos = lax.broadcasted_iota(I32, (tq, LANES), 0) + q0
    for h in range(N_HEADS):
        pick = _head_pick(tq, h)
        qm_scr[h] = jnp.where(pick, qi_ref[:, _pair_lanes(h)], zeros)
        qaug_scr[h] = jnp.concatenate([jnp.where(pick, qa_ref[:, _pair_lanes(h)], zeros),
                                       _alibi_lanes(qpos, ALIBI_SLOPES[h])], axis=1)

    w = w_ref[...] * QK_SCALE
    krow = lax.broadcasted_iota(I32, (ck, tq), 0)
    qcol = lax.broadcasted_iota(I32, (ck, tq), 1)

    def visible(c0):
        return (krow + c0) <= (qcol + q0)

    def score_chunk(c, carry):
        lo, hi = carry
        c0 = pl.multiple_of(c * ck, ck)
        kblk = ki2_ref[pl.ds(c0, ck), :]
        acc = jnp.zeros((ck, tq), F32)
        for h in range(N_HEADS):
            acc = acc + jnp.maximum(_nt_dot(kblk, qm_scr[h]), 0.0) * w[h:h + 1, :]
        seen = visible(c0)
        score_scr[pl.ds(c0, ck), :] = jnp.where(seen, acc, MASK_VALUE)
        lo = jnp.minimum(lo, jnp.min(jnp.where(seen, acc, jnp.inf).reshape(ck // 8, 8, tq), axis=0))
        hi = jnp.maximum(hi, jnp.max(jnp.where(seen, acc, MASK_VALUE).reshape(ck // 8, 8, tq), axis=0))
        return lo, hi

    lo, hi = lax.fori_loop(0, n_chunks, score_chunk,
                           (jnp.full((8, tq), jnp.inf, F32), jnp.full((8, tq), -jnp.inf, F32)))
    bounds = (jnp.min(lo, axis=0, keepdims=True), jnp.max(hi, axis=0, keepdims=True))

    _selection_bias(score_scr, bias_scr, tri_ref, n_chunks, n_beyond, visible, ck=ck, tq=tq, k_top=k_top,
                    try_bisect=q0 >= k_top, bounds=bounds)

    _flash_init(m_scr, acc_scr)
    kpos0 = lax.broadcasted_iota(I32, (ck, LANES), 0)

    def attend_chunk(c, carry):
        c0 = pl.multiple_of(c * ck, ck)
        kx_scr[...] = _alibi_lanes(kpos0 + c0)
        k_aug = lambda h: jnp.concatenate([ka_ref[pl.ds(c0, ck), _pair_lanes(h)], kx_scr[...]], axis=1)
        _flash_chunk(k_aug, qaug_scr, lambda h: vat_ref[c, HEAD_DIM * h:HEAD_DIM * (h + 1), :],
                     x_scr, m_scr, acc_scr, bias=bias_scr[pl.ds(c0, ck), :], row0=jnp.minimum(c, 0))
        return carry

    lax.fori_loop(0, n_chunks, attend_chunk, 0)
    _flash_output(o_ref, acc_scr)


def _dsa_prompt(qi, ki2, small_t, qa, ka, va_t, tri, tq, ck):
    n_b, seq, _ = qi.shape
    nblk = seq // tq
    kern = functools.partial(_dsa_kernel, tq=tq, ck=ck, k_top=_topk_count(seq), seq=seq)
    return pl.pallas_call(
        kern,
        out_shape=jax.ShapeDtypeStruct((n_b, seq, WIDTH), BF16),
        grid=(n_b, nblk),
        in_specs=[pl.BlockSpec((None, tq, WIDTH), lambda b, i: (b, i, 0)),
                  pl.BlockSpec((None, seq, LANES), lambda b, i: (b, 0, 0)),
                  pl.BlockSpec((None, N_HEADS, tq), lambda b, i: (b, SM_WI // N_HEADS, i)),
                  pl.BlockSpec((None, tq, WIDTH), lambda b, i: (b, i, 0)),
                  pl.BlockSpec((None, seq, WIDTH), lambda b, i: (b, 0, 0)),
                  pl.BlockSpec((None, seq // ck, WIDTH, ck), lambda b, i: (b, 0, 0, 0)),
                  _const_spec((ck, ck))],
        out_specs=pl.BlockSpec((None, tq, WIDTH), lambda b, i: (b, i, 0)),
        scratch_shapes=[pltpu.VMEM((N_HEADS, tq, LANES), BF16), pltpu.VMEM((seq, tq), F32),
                        pltpu.VMEM((seq, tq), F32), pltpu.VMEM((ck, LANES), BF16)]
                       + _flash_scratch(N_HEADS, tq, ck),
        compiler_params=_params("arbitrary", "arbitrary"),
        name="dsa_prompt",
    )(qi, ki2, small_t, qa, ka, va_t, tri)


def _head_diag(n_rows):
    lane = lax.broadcasted_iota(I32, (n_rows, WIDTH), 1)
    sub = lax.broadcasted_iota(I32, (n_rows, WIDTH), 0)
    return (lane >= sub * HEAD_DIM) & (lane < (sub + 1) * HEAD_DIM)


def _sidx_kernel(pt_ref, q_ref, w_ref, knew_ref, *rest, n_seq, n_pages, page):
    page_refs, o_ref = rest[:n_seq * n_pages], rest[n_seq * n_pages]
    lane = lax.broadcasted_iota(I32, (1, LANES), 1)
    for g in range(n_seq):
        q = q_ref[g]
        qp = jnp.concatenate([q, jnp.zeros((BF16_ROWS - N_HEADS, HEAD_DIM), F32)], axis=0).astype(BF16)
        w = w_ref[g] * QK_SCALE
        for p in range(n_pages):
            s = _dot(qp, page_refs[g * n_pages + p][...].astype(BF16))[:N_HEADS]
            o_ref[g, :, page * p:page * (p + 1)] = jnp.sum(jnp.maximum(s, 0.0) * w, axis=0, keepdims=True)
        knew = knew_ref[g].astype(BF16).astype(F32)
        s_new = jnp.sum(q * knew, axis=1, keepdims=True)
        sc_new = jnp.sum(jnp.maximum(s_new, 0.0) * w, axis=0, keepdims=True)
        o_ref[g, :, page * n_pages:] = jnp.where(lane == 0, sc_new, 0.0)


def _sample_index_scores(layer, page_table, cache_k_idx, qi_s, wi_s, ki_s):
    n_db, n_pages = page_table.shape
    page = cache_k_idx.shape[2]
    l_pad = n_pages * page + LANES
    n_seq = next(g for g in (8, 4, 2, 1) if n_db % g == 0)
    page_spec = lambda g, p: pl.BlockSpec((None, None, HEAD_DIM, page),
                                          lambda b, pt: (layer, pt[b * n_seq + g, p], 0, 0))
    k_idx_t = jnp.swapaxes(cache_k_idx, 2, 3)
    grid_spec = pltpu.PrefetchScalarGridSpec(
        num_scalar_prefetch=1,
        grid=(n_db // n_seq,),
        in_specs=[pl.BlockSpec((n_seq, N_HEADS, HEAD_DIM), lambda b, pt: (b, 0, 0)),
                  pl.BlockSpec((n_seq, N_HEADS, 1), lambda b, pt: (b, 0, 0)),
                  pl.BlockSpec((n_seq, 1, HEAD_DIM), lambda b, pt: (b, 0, 0))]
                 + [page_spec(g, p) for g in range(n_seq) for p in range(n_pages)],
        out_specs=pl.BlockSpec((n_seq, 1, l_pad), lambda b, pt: (b, 0, 0)),
    )
    out = pl.pallas_call(
        functools.partial(_sidx_kernel, n_seq=n_seq, n_pages=n_pages, page=page),
        out_shape=jax.ShapeDtypeStruct((n_db, 1, l_pad), F32),
        grid_spec=grid_spec,
        compiler_params=_params("arbitrary"),
        name="sample_index_scores",
    )(page_table, qi_s.astype(F32).reshape(n_db, N_HEADS, HEAD_DIM), wi_s.reshape(n_db, N_HEADS, 1),
      ki_s.reshape(n_db, 1, HEAD_DIM), *([k_idx_t] * (n_seq * n_pages)))
    return out.reshape(n_db, l_pad)


def _ssel_kernel(s_ref, tri_ref, o_ref, score_scr, *, n_keys, ck, k_top):
    l_pad, tq = s_ref.shape
    krow = lax.broadcasted_iota(I32, (ck, tq), 0)

    def valid(c0):
        return (krow + c0) < n_keys

    def drop_padding(c, carry):
        c0 = pl.multiple_of(c * ck, ck)
        score_scr[pl.ds(c0, ck), :] = jnp.where(valid(c0), s_ref[pl.ds(c0, ck), :], -jnp.inf)
        return carry

    lax.fori_loop(0, l_pad // ck, drop_padding, 0)
    _selection_bias(score_scr, o_ref, tri_ref, l_pad // ck, 0, valid, ck=ck, tq=tq, k_top=k_top)


def _sample_selection(scores_t, tri, n_keys):
    l_pad, n_db = scores_t.shape
    return pl.pallas_call(
        functools.partial(_ssel_kernel, n_keys=n_keys, ck=LANES, k_top=_topk_count(n_keys)),
        out_shape=jax.ShapeDtypeStruct((l_pad, n_db), F32),
        grid=(1,),
        in_specs=[pl.BlockSpec((l_pad, n_db), lambda i: (0, 0)), pl.BlockSpec((LANES, LANES), lambda i: (0, 0))],
        out_specs=pl.BlockSpec((l_pad, n_db), lambda i: (0, 0)),
        scratch_shapes=[pltpu.VMEM((l_pad, n_db), F32)],
        compiler_params=_params("arbitrary"),
        name="sample_selection",
    )(scores_t, tri)


def _paged_attend(q_ref, knew_ref, vnew_ref, k_pages, v_pages, logit_terms, o_ref):
    n_pages, page = len(k_pages), k_pages[0].shape[-1]
    diag = _head_diag(N_HEADS)
    qs = jnp.where(_head_diag(BF16_ROWS), jnp.broadcast_to(q_ref[...], (BF16_ROWS, WIDTH)), 0.0)
    qb = qs.astype(BF16)
    s = jnp.concatenate([_dot(qb, k_pages[p][...].astype(BF16))[:N_HEADS] for p in range(n_pages)], axis=1)
    s_new = jnp.sum(qs[:N_HEADS] * knew_ref[...].astype(BF16).astype(F32), axis=1, keepdims=True)
    x, x_new = logit_terms(s, s_new)
    m = jnp.maximum(jnp.max(x, axis=1, keepdims=True), x_new)
    p = jnp.exp(x - m)
    p_new = jnp.exp(x_new - m)
    denom = jnp.sum(p, axis=1, keepdims=True) + p_new
    pb = jnp.concatenate([p, jnp.zeros_like(p)], axis=0).astype(BF16)
    acc = p_new.astype(BF16).astype(F32) * vnew_ref[...].astype(BF16).astype(F32)
    for q in range(n_pages):
        acc = acc + _nt_dot(pb[:, page * q:page * (q + 1)], v_pages[q][...].astype(BF16))[:N_HEADS]
    o_ref[...] = jnp.sum(jnp.where(diag, acc / denom, 0.0), axis=0, keepdims=True)


def _sattn_kernel(pt_ref, qa_ref, qb_ref, kan_ref, van_ref, kbn_ref, vbn_ref, lfn_ref, bias_ref, slope_ref, *rest,
                  n_pages):
    ka_p, va_p, kb_p, vb_p, lf_p = (rest[g * n_pages:(g + 1) * n_pages] for g in range(5))
    oa_ref, ob_ref = rest[5 * n_pages:]
    past = n_pages * ka_p[0].shape[-1]

    def alibi_and_mask(s, s_new):
        rel = (lax.broadcasted_iota(I32, (1, past), 1) - past).astype(F32)
        bias = bias_ref[...]
        return s + slope_ref[...] * rel + bias[:, :past], s_new + bias[:, past:past + 1]

    _paged_attend(qa_ref, kan_ref, van_ref, ka_p, va_p, alibi_and_mask, oa_ref)

    lf = jnp.concatenate([lf_p[p][...] for p in range(n_pages)], axis=1)
    idx = lax.broadcasted_iota(I32, lf.shape, 1)
    d = 1
    while d < past:
        lf = lf + jnp.where(idx >= d, pltpu.roll(lf, d, 1), 0.0)
        d *= 2
    f_new = lf[:, past - 1:past] + lfn_ref[...]
    _paged_attend(qb_ref, kbn_ref, vbn_ref, kb_p, vb_p, lambda s, s_new: (s + (f_new - lf), s_new), ob_ref)


def _sample_attention(layer, page_table, caches, logf_t, qa_s, qb_s, new_rows, logf_new, bias):
    cache_k_a, cache_v_a, cache_k_b, cache_v_b = caches
    n_db, n_pages = page_table.shape
    n_pool, page = cache_k_a.shape[1], cache_k_a.shape[2]
    l_pad = bias.shape[1]
    flat = lambda c: jnp.transpose(c, (0, 1, 3, 4, 2)).reshape(c.shape[0], n_pool, WIDTH, page)
    row = lambda width: pl.BlockSpec((None, 1, width), lambda b, pt: (b, 0, 0))
    kv_spec = lambda p: pl.BlockSpec((None, None, WIDTH, page), lambda b, pt: (layer, pt[b, p], 0, 0))
    lf_spec = lambda p: pl.BlockSpec((None, N_HEADS, page), lambda b, pt: (pt[b, p], 0, 0))
    pages = range(n_pages)
    grid_spec = pltpu.PrefetchScalarGridSpec(
        num_scalar_prefetch=1,
        grid=(n_db,),
        in_specs=[row(WIDTH), row(WIDTH), row(WIDTH), row(WIDTH), row(WIDTH), row(WIDTH),
                  pl.BlockSpec((None, N_HEADS, 1), lambda b, pt: (b, 0, 0)), row(l_pad),
                  pl.BlockSpec((N_HEADS, 1), lambda b, pt: (0, 0))]
                 + [kv_spec(p) for p in pages] * 4 + [lf_spec(p) for p in pages],
        out_specs=(row(WIDTH), row(WIDTH)),
    )
    r3 = lambda a: a.reshape(n_db, 1, a.shape[-1])
    slopes = jnp.asarray(ALIBI_SLOPES, F32).reshape(N_HEADS, 1)
    return pl.pallas_call(
        functools.partial(_sattn_kernel, n_pages=n_pages),
        out_shape=(jax.ShapeDtypeStruct((n_db, 1, WIDTH), F32), jax.ShapeDtypeStruct((n_db, 1, WIDTH), F32)),
        grid_spec=grid_spec,
        compiler_params=_params("arbitrary"),
        name="sample_attention",
    )(page_table, r3(qa_s.astype(F32)), r3(qb_s.astype(F32)), *[r3(a) for a in new_rows],
      logf_new.reshape(n_db, N_HEADS, 1), r3(bias), slopes,
      *([flat(cache_k_a)] * n_pages), *([flat(cache_v_a)] * n_pages), *([flat(cache_k_b)] * n_pages),
      *([flat(cache_v_b)] * n_pages), *([logf_t] * n_pages))


def _fox_kernel(q_ref, qx_ref, k_ref, kx_ref, vt_ref, o_ref, qaug_scr, x_scr, m_scr, acc_scr, *, tq, ck):
    q0 = pl.program_id(1) * tq
    n_chunks = (q0 + tq + ck - 1) // ck
    lane = lax.broadcasted_iota(I32, (tq, LANES), 1)
    zeros = jnp.zeros((tq, LANES), BF16)
    qx = qx_ref[...]
    for h in range(N_HEADS):
        own = (lane >= FX * h) & (lane < FX * (h + 1))
        qaug_scr[h] = jnp.concatenate([jnp.where(_head_pick(tq, h), q_ref[:, _pair_lanes(h)], zeros),
                                       jnp.where(own, qx, zeros)], axis=1)
    _flash_init(m_scr, acc_scr)
    krow = lax.broadcasted_iota(I32, (ck, tq), 0)
    qcol = lax.broadcasted_iota(I32, (ck, tq), 1)

    def chunk(c, masked):
        c0 = pl.multiple_of(c * ck, ck)
        k_aug = lambda h: jnp.concatenate([k_ref[pl.ds(c0, ck), _pair_lanes(h)], kx_ref[pl.ds(c0, ck), :]], axis=1)
        visible = ((krow + c0) <= (qcol + q0)) if masked else None
        _flash_chunk(k_aug, qaug_scr, lambda h: vt_ref[c, HEAD_DIM * h:HEAD_DIM * (h + 1), :],
                     x_scr, m_scr, acc_scr, visible=visible)

    def full_chunk(c, carry):
        chunk(c, False)
        return carry

    lax.fori_loop(0, n_chunks - 1, full_chunk, 0)
    chunk(n_chunks - 1, True)
    _flash_output(o_ref, acc_scr)


def _fox_prompt(qb, qx, kb, kx, vb_t, tq, ck):
    n_b, seq, _ = qb.shape
    per_q = lambda width: pl.BlockSpec((None, tq, width), lambda b, i: (b, i, 0))
    whole = lambda width: pl.BlockSpec((None, seq, width), lambda b, i: (b, 0, 0))
    return pl.pallas_call(
        functools.partial(_fox_kernel, tq=tq, ck=ck),
        out_shape=jax.ShapeDtypeStruct((n_b, seq, WIDTH), BF16),
        grid=(n_b, seq // tq),
        in_specs=[per_q(WIDTH), per_q(LANES), whole(WIDTH), whole(LANES),
                  pl.BlockSpec((None, seq // ck, WIDTH, ck), lambda b, i: (b, 0, 0, 0))],
        out_specs=per_q(WIDTH),
        scratch_shapes=_flash_scratch(N_HEADS, tq, ck),
        compiler_params=_params("arbitrary", "arbitrary"),
        name="fox_prompt",
    )(qb, qx, kb, kx, vb_t)


def _post_kernel(x_ref, oa_ref, ob_ref, sga_ref, sgb_ref, gate1_ref, shift2_ref, scale2_ref, gate2_ref, g2_ref,
                 woa_ref, wob_ref, wout_ref, wup_ref, wdn_ref, y_ref, *, ff_chunk):
    merged = (sga_ref[...].astype(F32) * _dot(oa_ref[...], woa_ref[...])
              + sgb_ref[...].astype(F32) * _dot(ob_ref[...], wob_ref[...]))
    x1 = x_ref[...] + gate1_ref[...] * _dot(merged.astype(BF16), wout_ref[...])
    h2 = x1 * lax.rsqrt(jnp.mean(x1 * x1, axis=-1, keepdims=True) + EPS) * g2_ref[...]
    h2 = (h2 * (1.0 + scale2_ref[...]) + shift2_ref[...]).astype(BF16)
    mlp = jnp.zeros(x1.shape, F32)
    for c in range(0, D_FF, ff_chunk):
        u = jnp.maximum(_dot(h2, wup_ref[:, c:c + ff_chunk]), 0.0)
        mlp = mlp + _dot((u * u).astype(BF16), wdn_ref[c:c + ff_chunk, :])
    y_ref[...] = x1 + gate2_ref[...] * mlp


def _post(x, oa, ob, sga, sgb, mod, g2, woa, wob, wout, wup, wdn, tm):
    n_g, seq, _ = x.shape
    n_r = mod.shape[1]
    rb = 1 if n_r == 1 else tm
    row = lambda width: pl.BlockSpec((None, tm, width), lambda g, i: (g, i, 0))
    modspec = lambda j: pl.BlockSpec((None, rb, D_MODEL), lambda g, i: (g, i if n_r > 1 else 0, j))
    return pl.pallas_call(
        functools.partial(_post_kernel, ff_chunk=1024),
        out_shape=jax.ShapeDtypeStruct((n_g, seq, D_MODEL), F32),
        grid=(n_g, seq // tm),
        in_specs=[row(D_MODEL), row(WIDTH), row(WIDTH), row(D_MODEL), row(D_MODEL),
                  modspec(2), modspec(3), modspec(4), modspec(5), _const_spec((1, D_MODEL)),
                  _const_spec((WIDTH, D_MODEL)), _const_spec((WIDTH, D_MODEL)), _const_spec((D_MODEL, D_MODEL)),
                  _const_spec((D_MODEL, D_FF)), _const_spec((D_FF, D_MODEL))],
        out_specs=row(D_MODEL),
        compiler_params=_params("arbitrary", "arbitrary"),
        name="merge_out_mlp",
    )(x, oa, ob, sga, sgb, mod, mod, mod, mod, g2, woa, wob, wout, wup, wdn)


def _lower_tri(n):
    return jnp.asarray(np.tril(np.ones((n, n), np.float32)), BF16)


def _layer(l, x_p, x_s, caches, cache_k_idx, logf_t, page_table, mod_p, mod_s, lw, tiles):
    (g_norm1, w_r, gains, bsm, gmat, woa, wob, wout, g_norm2, wup, wdn) = lw
    tm, tq = tiles
    n_b, seq, _ = x_p.shape
    n_db = x_s.shape[0]

    (ka_p, va_p, kb_p, vb_p, small_p, qa_bf, ka_bf, vat_bf, qb_bf, kb_bf, vbt_bf, qi_bf, ki2_bf, smallt_p,
     sga_p, sgb_p, kx_b, qx_b) = _in_proj(x_p, mod_p, g_norm1, w_r, gains, bsm, gmat, tm)
    xs = x_s.reshape(1, n_db, D_MODEL)
    (ka_s, va_s, kb_s, vb_s, small_s, qa_s, _, _, qb_s, _, _, qi_s, _, _, sga_s, sgb_s, _, _) = _in_proj(
        xs, mod_s, g_norm1, w_r, gains, bsm, gmat, n_db)
    small_s2 = small_s[0]
    ki_s, wi_s = small_s2[:, :SM_WI], small_s2[:, SM_WI:SM_LOGF]
    logf_s = small_s2[:, SM_LOGF:SM_LOGF + N_HEADS]

    n_keys = page_table.shape[1] * cache_k_idx.shape[2] + 1
    scores = _sample_index_scores(l, page_table, cache_k_idx, qi_s[0], wi_s, ki_s)
    bias_t = _sample_selection(scores.T, _lower_tri(LANES), n_keys)

    oa_s, ob_s = _sample_attention(l, page_table, caches, logf_t, qa_s[0], qb_s[0],
                                   (ka_s[0], va_s[0], kb_s[0], vb_s[0]), logf_s, bias_t.T)
    ob_p = _fox_prompt(qb_bf, qx_b, kb_bf, kx_b, vbt_bf, tq, tm)
    oa_p = _dsa_prompt(qi_bf, ki2_bf, smallt_p, qa_bf, ka_bf, vat_bf, _lower_tri(tm), tq, tm)

    y_p = _post(x_p, oa_p, ob_p, sga_p, sgb_p, mod_p, g_norm2, woa, wob, wout, wup, wdn, tm)
    y_s = _post(xs, oa_s.reshape(1, n_db, WIDTH).astype(BF16), ob_s.reshape(1, n_db, WIDTH).astype(BF16),
                sga_s, sgb_s, mod_s, g_norm2, woa, wob, wout, wup, wdn, n_db)

    heads = lambda a, lead: a.reshape(*lead, N_HEADS, HEAD_DIM)
    lp, ls = (n_b, seq), (n_db, 1)
    from_t = lambda lo, hi: jnp.swapaxes(smallt_p[:, lo:hi, :], 1, 2)
    state_p = (heads(ka_p, lp), heads(va_p, lp), from_t(0, SM_WI), heads(kb_p, lp), heads(vb_p, lp),
               from_t(SM_LOGF, SM_LOGF + N_HEADS))
    state_s = (heads(ka_s[0], ls), heads(va_s[0], ls), ki_s.reshape(n_db, 1, HEAD_DIM), heads(kb_s[0], ls),
               heads(vb_s[0], ls), logf_s.reshape(n_db, 1, N_HEADS))
    return y_p, y_s.reshape(n_db, 1, D_MODEL), state_p, state_s


def kernel(x_prompt, x_sample, cache_k_a, cache_v_a, cache_k_idx, cache_k_b, cache_v_b, cache_logf_b, page_table,
           c_prompt, c_sample, w_ada, b_ada, g_norm1, w_in, b_forget, g_qa, g_ka, g_qb, g_kb, w_o_a, w_o_b, w_out,
           g_norm2, w_up, w_down):
    depth = w_in.shape[0]
    n_b, seq, _ = x_prompt.shape
    n_db, dec_seq, _ = x_sample.shape
    assert dec_seq == 1 and x_prompt.shape[-1] == D_MODEL
    assert cache_k_a.shape[3:] == (N_HEADS, HEAD_DIM) and cache_k_idx.shape[3] == HEAD_DIM
    tiles = (min(512, seq), min(256, seq))

    c_all = jnp.concatenate([c_prompt, c_sample], axis=0)
    pad = (-c_all.shape[0]) % 8
    c_all = jnp.pad(c_all, ((0, pad), (0, 0)))
    gmat = jnp.asarray(np.kron(np.eye(N_HEADS, dtype=np.float32),
                               np.full((HEAD_DIM, HEAD_DIM), 1.0 / HEAD_DIM, np.float32)), BF16)
    split = 7 * WIDTH
    y_p, y_s = x_prompt, x_sample.reshape(n_db, D_MODEL)
    states_p, states_s = [], []
    for l in range(depth):
        mod = _ada_mod(c_all, w_ada[l], b_ada[l])
        mod_p = mod[:n_b].reshape(n_b, 1, 6 * D_MODEL)
        mod_s = mod[n_b:n_b + n_db].reshape(1, n_db, 6 * D_MODEL)
        w = w_in[l]
        k_idx_cols = w[:, split:split + HEAD_DIM]
        w_r = jnp.concatenate(
            [w[:, :split], w[:, split:split + HEAD_DIM + 2 * N_HEADS],
             jnp.zeros((D_MODEL, LANES - HEAD_DIM - 2 * N_HEADS), w.dtype),
             k_idx_cols, k_idx_cols, w[:, split + HEAD_DIM + 2 * N_HEADS:]], axis=1).astype(BF16)
        gains = jnp.stack([jnp.tile(g[l], N_HEADS) for g in (g_qa, g_ka, g_qb, g_kb)])
        bsm = jnp.zeros((1, LANES), F32).at[0, SM_LOGF:SM_LOGF + N_HEADS].set(b_forget[l])
        lw = (g_norm1[l].reshape(1, D_MODEL), w_r, gains, bsm, gmat, w_o_a[l].astype(BF16), w_o_b[l].astype(BF16),
              w_out[l].astype(BF16), g_norm2[l].reshape(1, D_MODEL), w_up[l].astype(BF16), w_down[l].astype(BF16))
        logf_t = jnp.swapaxes(cache_logf_b[l], 1, 2)
        y_p, y_s, st_p, st_s = _layer(l, y_p, y_s, (cache_k_a, cache_v_a, cache_k_b, cache_v_b), cache_k_idx,
                                      logf_t, page_table, mod_p, mod_s, lw, tiles)
        y_s = y_s.reshape(n_db, D_MODEL)
        states_p.append(st_p)
        states_s.append(st_s)
    k_a_p, v_a_p, k_idx_p, k_b_p, v_b_p, logf_p = (jnp.stack(s) for s in zip(*states_p))
    k_a_s, v_a_s, k_idx_s, k_b_s, v_b_s, logf_s = (jnp.stack(s) for s in zip(*states_s))
    return (y_p, y_s.reshape(n_db, 1, D_MODEL), k_a_p, v_a_p, k_idx_p, k_b_p, v_b_p, logf_p,
            k_a_s, v_a_s, k_idx_s, k_b_s, v_b_s, logf_s)
```

```python
import functools

import numpy as np
import jax
import jax.numpy as jnp
from jax import lax
from jax.experimental import pallas as pl
from jax.experimental.pallas import tpu as pltpu

F32 = jnp.float32
BF16 = jnp.bfloat16
I32 = jnp.int32

D_MODEL = 1024
HEAD_DIM = 64
N_HEADS = 8
WIDTH = N_HEADS * HEAD_DIM
D_FF = 4 * D_MODEL
TOPK_MAX = 256
EPS = 1e-6
MASK_VALUE = -1e30
LANES = 128
BF16_ROWS = 16
QK_SCALE = HEAD_DIM ** -0.5
WI_SCALE = N_HEADS ** -0.5
ALIBI_SLOPES = tuple(2.0 ** (-8.0 * (h + 1) / N_HEADS) for h in range(N_HEADS))
INT_MIN = -(2 ** 31)
VMEM_LIMIT = 56 * 1024 * 1024

C_QA, C_KA, C_VA, C_QB, C_KB, C_VB, C_QI = (i * WIDTH for i in range(7))
C_SMALL = 7 * WIDTH
C_KI2 = C_SMALL + LANES
C_GA = C_KI2 + LANES
C_GB = C_GA + D_MODEL
C_TOTAL = C_GB + D_MODEL
SM_WI = HEAD_DIM
SM_LOGF = HEAD_DIM + N_HEADS


def _topk_count(n_keys):
    return max(1, min(TOPK_MAX, n_keys // 4))


def _const_spec(shape):
    zeros = (0,) * len(shape)
    return pl.BlockSpec(shape, lambda *_: zeros, pipeline_mode=pl.Buffered(1))


def _params(*sem):
    return pltpu.CompilerParams(dimension_semantics=sem, vmem_limit_bytes=VMEM_LIMIT)


def _nt_dot(a, b):
    return lax.dot_general(a, b, (((1,), (1,)), ((), ())), preferred_element_type=F32)


def _dot(a, b):
    return jnp.dot(a, b, preferred_element_type=F32)


def _split3(x):
    p1 = x.astype(BF16)
    r1 = x - p1.astype(F32)
    p2 = r1.astype(BF16)
    p3 = (r1 - p2.astype(F32)).astype(BF16)
    return p1, p2, p3


def _ada_kernel(c_ref, w_ref, b_ref, o_ref):
    c = c_ref[...]
    s = c * jax.nn.sigmoid(c)
    o_ref[...] = _dot(s.astype(BF16), w_ref[...].astype(BF16)) + b_ref[...]


def _ada_mod(c, w_ada, b_ada):
    rows = c.shape[0]
    n_out = w_ada.shape[1]
    tn = D_MODEL
    return pl.pallas_call(
        _ada_kernel,
        out_shape=jax.ShapeDtypeStruct((rows, n_out), F32),
        grid=(n_out // tn,),
        in_specs=[pl.BlockSpec((rows, D_MODEL), lambda j: (0, 0)),
                  pl.BlockSpec((D_MODEL, tn), lambda j: (0, j)),
                  pl.BlockSpec((1, tn), lambda j: (0, j))],
        out_specs=pl.BlockSpec((rows, tn), lambda j: (0, j)),
        compiler_params=_params("arbitrary"),
        name="ada_mod",
    )(c, w_ada, b_ada.reshape(1, n_out))


def _log_sigmoid(x):
    return jnp.minimum(x, 0.0) - jnp.log(1.0 + jnp.exp(-jnp.abs(x)))


FX = 6


def _forget_placement():
    place = np.zeros((3, LANES, 2 * LANES), np.float32)
    const = np.zeros((1, 2 * LANES), np.float32)
    for h in range(N_HEADS):
        for i in range(3):
            place[i, SM_LOGF + h, FX * h + i] = 1.0
            place[i, SM_LOGF + h, LANES + FX * h + 3 + i] = 1.0
            const[0, FX * h + 3 + i] = 1.0
            const[0, LANES + FX * h + i] = -1.0
    return jnp.asarray(place, BF16), jnp.asarray(const, F32)


def _in_proj_kernel(x_ref, shift_ref, scale_ref, g1_ref, w_ref, gains_ref, bsm_ref, gmat_ref,
                    tri_ref, place_ref, fconst_ref,
                    ka_ref, va_ref, kb_ref, vb_ref, small_ref,
                    qa_bf, ka_bf, vat_bf, qb_bf, kb_bf, vbt_bf, qi_bf, ki2_bf, smallt_ref, sga_ref, sgb_ref,
                    kx_ref, qx_ref, carry_ref):
    x = x_ref[...]
    h = x * lax.rsqrt(jnp.mean(x * x, axis=-1, keepdims=True) + EPS) * g1_ref[...]
    h = h * (1.0 + scale_ref[...]) + shift_ref[...]
    hb = h.astype(BF16)

    def seg(col, width=WIDTH):
        return _dot(hb, w_ref[:, col:col + width])

    def head_norm(z, gi):
        msq = _dot((z * z).astype(BF16), gmat_ref[...])
        return z * lax.rsqrt(msq + EPS) * gains_ref[gi:gi + 1, :]

    qa = head_norm(seg(C_QA), 0)
    qa_bf[...] = (qa * QK_SCALE).astype(BF16)
    ka = head_norm(seg(C_KA), 1)
    ka_ref[...] = ka
    ka_bf[...] = ka.astype(BF16)
    va = seg(C_VA)
    va_ref[...] = va
    vat_bf[...] = va.T.astype(BF16)
    qb = head_norm(seg(C_QB), 2)
    qb_bf[...] = (qb * QK_SCALE).astype(BF16)
    kb = head_norm(seg(C_KB), 3)
    kb_ref[...] = kb
    kb_bf[...] = kb.astype(BF16)
    vb = seg(C_VB)
    vb_ref[...] = vb
    vbt_bf[...] = vb.T.astype(BF16)
    qi_bf[...] = seg(C_QI).astype(BF16)
    ki2_bf[...] = seg(C_KI2, LANES).astype(BF16)

    zs = seg(C_SMALL, LANES)
    lane = lax.broadcasted_iota(I32, zs.shape, 1)
    logf = _log_sigmoid(zs + bsm_ref[...])
    small = jnp.where(lane < SM_WI, zs,
                      jnp.where(lane < SM_LOGF, zs * WI_SCALE,
                                jnp.where(lane < SM_LOGF + N_HEADS, logf, 0.0)))
    small_ref[...] = small
    smallt_ref[...] = small.T

    @pl.when(pl.program_id(1) == 0)
    def _():
        carry_ref[...] = jnp.zeros_like(carry_ref)

    tri = tri_ref[...]
    p1, p2, p3 = _split3(small)
    cs = _dot(tri, p1) + _dot(tri, p2) + _dot(tri, p3) + carry_ref[...]
    n = cs.shape[0]
    carry_ref[...] = cs[n - 1:n, :]
    f1, f2, f3 = _split3(cs)
    ext = _dot(f1, place_ref[0]) + _dot(f2, place_ref[1]) + _dot(f3, place_ref[2]) + fconst_ref[...]
    kx_ref[...] = ext[:, :LANES].astype(BF16)
    qx_ref[...] = ext[:, LANES:].astype(BF16)

    sga_ref[...] = jax.nn.sigmoid(seg(C_GA, D_MODEL)).astype(BF16)
    sgb_ref[...] = jax.nn.sigmoid(seg(C_GB, D_MODEL)).astype(BF16)


def _in_proj(x, mod, g1, w_r, gains, bsm, gmat, tm):
    n_g, seq, _ = x.shape
    n_r = mod.shape[1]
    rb = 1 if n_r == 1 else tm
    nblk = seq // tm
    row = lambda width: pl.BlockSpec((None, tm, width), lambda g, i: (g, i, 0))
    modspec = lambda j: pl.BlockSpec((None, rb, D_MODEL), lambda g, i: (g, i if n_r > 1 else 0, j))
    sds = lambda shape, dt: jax.ShapeDtypeStruct(shape, dt)
    out_shape = (
        sds((n_g, seq, WIDTH), F32), sds((n_g, seq, WIDTH), F32), sds((n_g, seq, WIDTH), F32),
        sds((n_g, seq, WIDTH), F32), sds((n_g, seq, LANES), F32),
        sds((n_g, seq, WIDTH), BF16), sds((n_g, seq, WIDTH), BF16), sds((n_g, nblk, WIDTH, tm), BF16),
        sds((n_g, seq, WIDTH), BF16), sds((n_g, seq, WIDTH), BF16), sds((n_g, nblk, WIDTH, tm), BF16),
        sds((n_g, seq, WIDTH), BF16), sds((n_g, seq, LANES), BF16), sds((n_g, LANES, seq), F32),
        sds((n_g, seq, D_MODEL), BF16), sds((n_g, seq, D_MODEL), BF16),
        sds((n_g, seq, LANES), BF16), sds((n_g, seq, LANES), BF16),
    )
    chunk_t = pl.BlockSpec((None, None, WIDTH, tm), lambda g, i: (g, i, 0, 0))
    out_specs = (
        row(WIDTH), row(WIDTH), row(WIDTH), row(WIDTH), row(LANES),
        row(WIDTH), row(WIDTH), chunk_t,
        row(WIDTH), row(WIDTH), chunk_t,
        row(WIDTH), row(LANES), pl.BlockSpec((None, LANES, tm), lambda g, i: (g, 0, i)),
        row(D_MODEL), row(D_MODEL),
        row(LANES), row(LANES),
    )
    place, fconst = _forget_placement()
    return pl.pallas_call(
        _in_proj_kernel,
        out_shape=out_shape,
        grid=(n_g, nblk),
        in_specs=[row(D_MODEL), modspec(0), modspec(1), _const_spec((1, D_MODEL)),
                  _const_spec((D_MODEL, C_TOTAL)), _const_spec((4, WIDTH)), _const_spec((1, LANES)),
                  _const_spec((WIDTH, WIDTH)), _const_spec((tm, tm)), _const_spec((3, LANES, 2 * LANES)),
                  _const_spec((1, 2 * LANES))],
        out_specs=out_specs,
        scratch_shapes=[pltpu.VMEM((1, LANES), F32)],
        compiler_params=_params("arbitrary", "arbitrary"),
        name="in_proj",
    )(x, mod, mod, g1, w_r, gains, bsm, gmat, _lower_tri(tm), place, fconst)


def _flash_chunk(k_aug, qaug_scr, vt_of, x_scr, m_scr, acc_scr, bias=None, visible=None, row0=0):
    n_slots, ck = x_scr.shape[0], x_scr.shape[1]
    ones = jnp.ones((BF16_ROWS, ck), BF16)
    alphas = {}
    rows = slice(None) if isinstance(row0, int) else pl.ds(pl.multiple_of(row0, ck), ck)

    def logits(s):
        x = _nt_dot(k_aug(s), qaug_scr[s])
        if bias is not None:
            x = x + bias
        if visible is not None:
            x = jnp.where(visible, x, MASK_VALUE)
        x_scr[s, rows, :] = x
        m_old = m_scr[s]
        m_new = jnp.maximum(m_old, jnp.max(x, axis=0, keepdims=True))
        alphas[s] = jnp.exp(m_old - m_new)
        m_scr[s] = m_new

    def values(s):
        p = jnp.exp(x_scr[s, rows, :] - m_scr[s]).astype(BF16)
        acc_scr[s] = alphas[s] * acc_scr[s] + _dot(jnp.concatenate([vt_of(s), ones], axis=0), p)

    for phase in (logits, values):
        for s in range(n_slots):
            phase(s)


def _flash_scratch(n_slots, tq, ck):
    return [pltpu.VMEM((n_slots, tq, 2 * LANES), BF16), pltpu.VMEM((n_slots, ck, tq), F32),
            pltpu.VMEM((n_slots, 1, tq), F32), pltpu.VMEM((n_slots, HEAD_DIM + BF16_ROWS, tq), F32)]


def _flash_init(m_scr, acc_scr):
    m_scr[...] = jnp.full(m_scr.shape, MASK_VALUE, F32)
    acc_scr[...] = jnp.zeros_like(acc_scr)


def _flash_output(o_ref, acc_scr):
    out_t = jnp.concatenate([acc_scr[s, :HEAD_DIM] / acc_scr[s, HEAD_DIM:HEAD_DIM + 1]
                             for s in range(acc_scr.shape[0])], axis=0)
    o_ref[...] = out_t.T.astype(BF16)


def _head_pick(tq, h):
    lane = lax.broadcasted_iota(I32, (tq, LANES), 1)
    return (lane >= HEAD_DIM) == (h % 2 == 1)


def _pair_lanes(h):
    return slice(LANES * (h // 2), LANES * (h // 2 + 1))


def _ordered_float(u):
    key = u ^ jnp.int32(INT_MIN)
    return pltpu.bitcast(key ^ ((key >> 31) & jnp.int32(0x7FFFFFFF)), F32)


BISECT_STEPS = 22
CUT_UNKNOWN, CUT_EXACT, CUT_RANKED = 0, 1, 2


def _selection_bias(score_scr, bias_scr, tri_ref, n_chunks, n_beyond, valid_fn, *, ck, tq, k_top, try_bisect=None,
                    bounds=None):
    n_acc = 4

    def partials(blk, reduce):
        return reduce(blk.reshape(ck // (8 * n_acc), n_acc, 8, tq), axis=0)

    def finish(acc, reduce):
        return reduce(reduce(acc, axis=0), axis=0, keepdims=True)

    def count(cmp):
        def body(c, acc):
            c0 = pl.multiple_of(c * ck, ck)
            return acc + partials(jnp.where(cmp(score_scr[pl.ds(c0, ck), :]), 1, 0), jnp.sum)
        return finish(lax.fori_loop(0, n_chunks, body, jnp.zeros((n_acc, 8, tq), I32)), jnp.sum)

    def reaching(x):
        return count(lambda blk: blk >= x) + jnp.where(x <= MASK_VALUE, n_beyond, 0)

    def exact_cut(thr):
        def body(c, carry):
            c0 = pl.multiple_of(c * ck, ck)
            keep = jnp.where(score_scr[pl.ds(c0, ck), :] >= thr, 0.0, MASK_VALUE)
            bias_scr[pl.ds(c0, ck), :] = jnp.where(valid_fn(c0), keep, MASK_VALUE)
            return carry
        lax.fori_loop(0, n_chunks, body, 0)

    def ranked_cut(thr):
        n_greater = count(lambda blk: blk > thr) + jnp.where(thr < MASK_VALUE, n_beyond, 0)
        ties_wanted = (k_top - n_greater).astype(F32)

        def body(c, ties_before):
            c0 = pl.multiple_of(c * ck, ck)
            blk = score_scr[pl.ds(c0, ck), :]
            tie = blk == thr
            rank = _dot(tri_ref[...], jnp.where(tie, 1.0, 0.0).astype(BF16)) + ties_before
            keep = jnp.where(blk > thr, 0.0,
                             jnp.where(tie, jnp.where(rank <= ties_wanted, 0.0, MASK_VALUE), MASK_VALUE))
            bias_scr[pl.ds(c0, ck), :] = jnp.where(valid_fn(c0), keep, MASK_VALUE)
            return rank[ck - 1:ck, :]
        lax.fori_loop(0, n_chunks, body, jnp.zeros((1, tq), F32))

    def bitwise():
        def bit_step(i, carry):
            code, reached = carry
            cand_code = code | lax.shift_left(jnp.int32(1), 31 - i)
            cnt = reaching(_ordered_float(cand_code))
            take = cnt >= k_top
            return jnp.where(take, cand_code, code), jnp.where(take, cnt, reached)

        code, reached = lax.fori_loop(0, 32, bit_step, (jnp.zeros((1, tq), I32), jnp.full((1, tq), k_top, I32)))
        thr = _ordered_float(code)
        lax.cond(jnp.max(reached) > k_top, lambda: ranked_cut(thr), lambda: exact_cut(thr))

    if try_bisect is None:
        bitwise()
        return

    def unsettled(reached):
        return jnp.max(jnp.abs(reached - k_top)) > 0

    def bisect():
        lo, top = bounds

        def halve(_, state):
            lo, hi, reached = state
            mid = 0.5 * lo + 0.5 * hi
            cnt = reaching(mid)
            up = cnt >= k_top
            return jnp.where(up, mid, lo), jnp.where(up, hi, mid), jnp.where(up, cnt, reached)

        above = top + (jnp.maximum(top - lo, jnp.abs(top)) * 2.0 ** -10 + 1e-30)
        lo, hi, reached = lax.fori_loop(0, BISECT_STEPS, halve, (lo, above, jnp.full((1, tq), k_top + 1, I32)))

        def tied():
            def below(c, acc):
                c0 = pl.multiple_of(c * ck, ck)
                blk = score_scr[pl.ds(c0, ck), :]
                return jnp.maximum(acc, partials(jnp.where(blk < hi, blk, -jnp.inf), jnp.max))
            top = finish(lax.fori_loop(0, n_chunks, below, jnp.full((n_acc, 8, tq), -jnp.inf, F32)), jnp.max)
            settled = reached == k_top
            confirmed = jnp.min(jnp.where(settled | (reaching(top) >= k_top), 1, 0)) > 0
            return jnp.where(settled, lo, top), jnp.where(confirmed, CUT_RANKED, CUT_UNKNOWN)

        return lax.cond(unsettled(reached), tied, lambda: (lo, jnp.int32(CUT_EXACT)))

    thr, cut = lax.cond(try_bisect, bisect, lambda: (jnp.zeros((1, tq), F32), jnp.int32(CUT_UNKNOWN)))
    lax.switch(cut, [bitwise, lambda: exact_cut(thr), lambda: ranked_cut(thr)])


POS_SPLIT = 256


def _alibi_lanes(pos, slope=None):
    lane = lax.broadcasted_iota(I32, pos.shape, 1)
    lo = pos & (POS_SPLIT - 1)
    hi_f, lo_f = (pos - lo).astype(F32), lo.astype(F32)
    if slope is None:
        val = jnp.where(lane == 0, hi_f, jnp.where(lane == 1, lo_f, jnp.where(lane < 4, 1.0, 0.0)))
    else:
        val = jnp.where(lane < 2, slope, jnp.where(lane == 2, hi_f * -slope, jnp.where(lane == 3, lo_f * -slope, 0.0)))
    return val.astype(BF16)


def _dsa_kernel(qi_ref, ki2_ref, w_ref, qa_ref, ka_ref, vat_ref, tri_ref, o_ref,
                qm_scr, score_scr, bias_scr, kx_scr, qaug_scr, x_scr, m_scr, acc_scr, *, tq, ck, k_top, seq):
    q0 = pl.program_id(1) * tq
    n_chunks = (q0 + tq + ck - 1) // ck
    n_beyond = seq - n_chunks * ck

    zeros = jnp.zeros((tq, LANES), BF16)
    qpos = lax.broadcasted_iota(I32, (tq, LANES), 0) + q0
    for h in range(N_HEADS):
        pick = _head_pick(tq, h)
        qm_scr[h] = jnp.where(pick, qi_ref[:, _pair_lanes(h)], zeros)
        qaug_scr[h] = jnp.concatenate([jnp.where(pick, qa_ref[:, _pair_lanes(h)], zeros),
                                       _alibi_lanes(qpos, ALIBI_SLOPES[h])], axis=1)

    w = w_ref[...] * QK_SCALE
    krow = lax.broadcasted_iota(I32, (ck, tq), 0)
    qcol = lax.broadcasted_iota(I32, (ck, tq), 1)

    def visible(c0):
        return (krow + c0) <= (qcol + q0)

    def score_chunk(c, carry):
        lo, hi = carry
        c0 = pl.multiple_of(c * ck, ck)
        kblk = ki2_ref[pl.ds(c0, ck), :]
        acc = jnp.zeros((ck, tq), F32)
        for h in range(N_HEADS):
            acc = acc + jnp.maximum(_nt_dot(kblk, qm_scr[h]), 0.0) * w[h:h + 1, :]
        seen = visible(c0)
        score_scr[pl.ds(c0, ck), :] = jnp.where(seen, acc, MASK_VALUE)
        lo = jnp.minimum(lo, jnp.min(jnp.where(seen, acc, jnp.inf).reshape(ck // 8, 8, tq), axis=0))
        hi = jnp.maximum(hi, jnp.max(jnp.where(seen, acc, MASK_VALUE).reshape(ck // 8, 8, tq), axis=0))
        return lo, hi

    lo, hi = lax.fori_loop(0, n_chunks, score_chunk,
                           (jnp.full((8, tq), jnp.inf, F32), jnp.full((8, tq), -jnp.inf, F32)))
    bounds = (jnp.min(lo, axis=0, keepdims=True), jnp.max(hi, axis=0, keepdims=True))

    _selection_bias(score_scr, bias_scr, tri_ref, n_chunks, n_beyond, visible, ck=ck, tq=tq, k_top=k_top,
                    try_bisect=q0 >= k_top, bounds=bounds)

    _flash_init(m_scr, acc_scr)
    kpos0 = lax.broadcasted_iota(I32, (ck, LANES), 0)

    def attend_chunk(c, carry):
        c0 = pl.multiple_of(c * ck, ck)
        kx_scr[...] = _alibi_lanes(kpos0 + c0)
        k_aug = lambda h: jnp.concatenate([ka_ref[pl.ds(c0, ck), _pair_lanes(h)], kx_scr[...]], axis=1)
        _flash_chunk(k_aug, qaug_scr, lambda h: vat_ref[c, HEAD_DIM * h:HEAD_DIM * (h + 1), :],
                     x_scr, m_scr, acc_scr, bias=bias_scr[pl.ds(c0, ck), :], row0=jnp.minimum(c, 0))
        return carry

    lax.fori_loop(0, n_chunks, attend_chunk, 0)
    _flash_output(o_ref, acc_scr)


def _dsa_prompt(qi, ki2, small_t, qa, ka, va_t, tri, tq, ck):
    n_b, seq, _ = qi.shape
    nblk = seq // tq
    kern = functools.partial(_dsa_kernel, tq=tq, ck=ck, k_top=_topk_count(seq), seq=seq)
    return pl.pallas_call(
        kern,
        out_shape=jax.ShapeDtypeStruct((n_b, seq, WIDTH), BF16),
        grid=(n_b, nblk),
        in_specs=[pl.BlockSpec((None, tq, WIDTH), lambda b, i: (b, i, 0)),
                  pl.BlockSpec((None, seq, LANES), lambda b, i: (b, 0, 0)),
                  pl.BlockSpec((None, N_HEADS, tq), lambda b, i: (b, SM_WI // N_HEADS, i)),
                  pl.BlockSpec((None, tq, WIDTH), lambda b, i: (b, i, 0)),
                  pl.BlockSpec((None, seq, WIDTH), lambda b, i: (b, 0, 0)),
                  pl.BlockSpec((None, seq // ck, WIDTH, ck), lambda b, i: (b, 0, 0, 0)),
                  _const_spec((ck, ck))],
        out_specs=pl.BlockSpec((None, tq, WIDTH), lambda b, i: (b, i, 0)),
        scratch_shapes=[pltpu.VMEM((N_HEADS, tq, LANES), BF16), pltpu.VMEM((seq, tq), F32),
                        pltpu.VMEM((seq, tq), F32), pltpu.VMEM((ck, LANES), BF16)]
                       + _flash_scratch(N_HEADS, tq, ck),
        compiler_params=_params("arbitrary", "arbitrary"),
        name="dsa_prompt",
    )(qi, ki2, small_t, qa, ka, va_t, tri)


def _head_diag(n_rows):
    lane = lax.broadcasted_iota(I32, (n_rows, WIDTH), 1)
    sub = lax.broadcasted_iota(I32, (n_rows, WIDTH), 0)
    return (lane >= sub * HEAD_DIM) & (lane < (sub + 1) * HEAD_DIM)


def _sidx_kernel(pt_ref, q_ref, w_ref, knew_ref, *rest, n_seq, n_pages, page):
    page_refs, o_ref = rest[:n_seq * n_pages], rest[n_seq * n_pages]
    lane = lax.broadcasted_iota(I32, (1, LANES), 1)
    for g in range(n_seq):
        q = q_ref[g]
        qp = jnp.concatenate([q, jnp.zeros((BF16_ROWS - N_HEADS, HEAD_DIM), F32)], axis=0).astype(BF16)
        w = w_ref[g] * QK_SCALE
        for p in range(n_pages):
            s = _dot(qp, page_refs[g * n_pages + p][...].astype(BF16))[:N_HEADS]
            o_ref[g, :, page * p:page * (p + 1)] = jnp.sum(jnp.maximum(s, 0.0) * w, axis=0, keepdims=True)
        knew = knew_ref[g].astype(BF16).astype(F32)
        s_new = jnp.sum(q * knew, axis=1, keepdims=True)
        sc_new = jnp.sum(jnp.maximum(s_new, 0.0) * w, axis=0, keepdims=True)
        o_ref[g, :, page * n_pages:] = jnp.where(lane == 0, sc_new, 0.0)


def _sample_index_scores(layer, page_table, cache_k_idx, qi_s, wi_s, ki_s):
    n_db, n_pages = page_table.shape
    page = cache_k_idx.shape[2]
    l_pad = n_pages * page + LANES
    n_seq = next(g for g in (8, 4, 2, 1) if n_db % g == 0)
    page_spec = lambda g, p: pl.BlockSpec((None, None, HEAD_DIM, page),
                                          lambda b, pt: (layer, pt[b * n_seq + g, p], 0, 0))
    k_idx_t = jnp.swapaxes(cache_k_idx, 2, 3)
    grid_spec = pltpu.PrefetchScalarGridSpec(
        num_scalar_prefetch=1,
        grid=(n_db // n_seq,),
        in_specs=[pl.BlockSpec((n_seq, N_HEADS, HEAD_DIM), lambda b, pt: (b, 0, 0)),
                  pl.BlockSpec((n_seq, N_HEADS, 1), lambda b, pt: (b, 0, 0)),
                  pl.BlockSpec((n_seq, 1, HEAD_DIM), lambda b, pt: (b, 0, 0))]
                 + [page_spec(g, p) for g in range(n_seq) for p in range(n_pages)],
        out_specs=pl.BlockSpec((n_seq, 1, l_pad), lambda b, pt: (b, 0, 0)),
    )
    out = pl.pallas_call(
        functools.partial(_sidx_kernel, n_seq=n_seq, n_pages=n_pages, page=page),
        out_shape=jax.ShapeDtypeStruct((n_db, 1, l_pad), F32),
        grid_spec=grid_spec,
        compiler_params=_params("arbitrary"),
        name="sample_index_scores",
    )(page_table, qi_s.astype(F32).reshape(n_db, N_HEADS, HEAD_DIM), wi_s.reshape(n_db, N_HEADS, 1),
      ki_s.reshape(n_db, 1, HEAD_DIM), *([k_idx_t] * (n_seq * n_pages)))
    return out.reshape(n_db, l_pad)


def _ssel_kernel(s_ref, tri_ref, o_ref, score_scr, *, n_keys, ck, k_top):
    l_pad, tq = s_ref.shape
    krow = lax.broadcasted_iota(I32, (ck, tq), 0)

    def valid(c0):
        return (krow + c0) < n_keys

    def drop_padding(c, carry):
        c0 = pl.multiple_of(c * ck, ck)
        score_scr[pl.ds(c0, ck), :] = jnp.where(valid(c0), s_ref[pl.ds(c0, ck), :], -jnp.inf)
        return carry

    lax.fori_loop(0, l_pad // ck, drop_padding, 0)
    _selection_bias(score_scr, o_ref, tri_ref, l_pad // ck, 0, valid, ck=ck, tq=tq, k_top=k_top)


def _sample_selection(scores_t, tri, n_keys):
    l_pad, n_db = scores_t.shape
    return pl.pallas_call(
        functools.partial(_ssel_kernel, n_keys=n_keys, ck=LANES, k_top=_topk_count(n_keys)),
        out_shape=jax.ShapeDtypeStruct((l_pad, n_db), F32),
        grid=(1,),
        in_specs=[pl.BlockSpec((l_pad, n_db), lambda i: (0, 0)), pl.BlockSpec((LANES, LANES), lambda i: (0, 0))],
        out_specs=pl.BlockSpec((l_pad, n_db), lambda i: (0, 0)),
        scratch_shapes=[pltpu.VMEM((l_pad, n_db), F32)],
        compiler_params=_params("arbitrary"),
        name="sample_selection",
    )(scores_t, tri)


def _paged_attend(q_ref, knew_ref, vnew_ref, k_pages, v_pages, logit_terms, o_ref):
    n_pages, page = len(k_pages), k_pages[0].shape[-1]
    diag = _head_diag(N_HEADS)
    qs = jnp.where(_head_diag(BF16_ROWS), jnp.broadcast_to(q_ref[...], (BF16_ROWS, WIDTH)), 0.0)
    qb = qs.astype(BF16)
    s = jnp.concatenate([_dot(qb, k_pages[p][...].astype(BF16))[:N_HEADS] for p in range(n_pages)], axis=1)
    s_new = jnp.sum(qs[:N_HEADS] * knew_ref[...].astype(BF16).astype(F32), axis=1, keepdims=True)
    x, x_new = logit_terms(s, s_new)
    m = jnp.maximum(jnp.max(x, axis=1, keepdims=True), x_new)
    p = jnp.exp(x - m)
    p_new = jnp.exp(x_new - m)
    denom = jnp.sum(p, axis=1, keepdims=True) + p_new
    pb = jnp.concatenate([p, jnp.zeros_like(p)], axis=0).astype(BF16)
    acc = p_new.astype(BF16).astype(F32) * vnew_ref[...].astype(BF16).astype(F32)
    for q in range(n_pages):
        acc = acc + _nt_dot(pb[:, page * q:page * (q + 1)], v_pages[q][...].astype(BF16))[:N_HEADS]
    o_ref[...] = jnp.sum(jnp.where(diag, acc / denom, 0.0), axis=0, keepdims=True)


def _sattn_kernel(pt_ref, qa_ref, qb_ref, kan_ref, van_ref, kbn_ref, vbn_ref, lfn_ref, bias_ref, slope_ref, *rest,
                  n_pages):
    ka_p, va_p, kb_p, vb_p, lf_p = (rest[g * n_pages:(g + 1) * n_pages] for g in range(5))
    oa_ref, ob_ref = rest[5 * n_pages:]
    past = n_pages * ka_p[0].shape[-1]

    def alibi_and_mask(s, s_new):
        rel = (lax.broadcasted_iota(I32, (1, past), 1) - past).astype(F32)
        bias = bias_ref[...]
        return s + slope_ref[...] * rel + bias[:, :past], s_new + bias[:, past:past + 1]

    _paged_attend(qa_ref, kan_ref, van_ref, ka_p, va_p, alibi_and_mask, oa_ref)

    lf = jnp.concatenate([lf_p[p][...] for p in range(n_pages)], axis=1)
    idx = lax.broadcasted_iota(I32, lf.shape, 1)
    d = 1
    while d < past:
        lf = lf + jnp.where(idx >= d, pltpu.roll(lf, d, 1), 0.0)
        d *= 2
    f_new = lf[:, past - 1:past] + lfn_ref[...]
    _paged_attend(qb_ref, kbn_ref, vbn_ref, kb_p, vb_p, lambda s, s_new: (s + (f_new - lf), s_new), ob_ref)


def _sample_attention(layer, page_table, caches, logf_t, qa_s, qb_s, new_rows, logf_new, bias):
    cache_k_a, cache_v_a, cache_k_b, cache_v_b = caches
    n_db, n_pages = page_table.shape
    n_pool, page = cache_k_a.shape[1], cache_k_a.shape[2]
    l_pad = bias.shape[1]
    flat = lambda c: jnp.transpose(c, (0, 1, 3, 4, 2)).reshape(c.shape[0], n_pool, WIDTH, page)
    row = lambda width: pl.BlockSpec((None, 1, width), lambda b, pt: (b, 0, 0))
    kv_spec = lambda p: pl.BlockSpec((None, None, WIDTH, page), lambda b, pt: (layer, pt[b, p], 0, 0))
    lf_spec = lambda p: pl.BlockSpec((None, N_HEADS, page), lambda b, pt: (pt[b, p], 0, 0))
    pages = range(n_pages)
    grid_spec = pltpu.PrefetchScalarGridSpec(
        num_scalar_prefetch=1,
        grid=(n_db,),
        in_specs=[row(WIDTH), row(WIDTH), row(WIDTH), row(WIDTH), row(WIDTH), row(WIDTH),
                  pl.BlockSpec((None, N_HEADS, 1), lambda b, pt: (b, 0, 0)), row(l_pad),
                  pl.BlockSpec((N_HEADS, 1), lambda b, pt: (0, 0))]
                 + [kv_spec(p) for p in pages] * 4 + [lf_spec(p) for p in pages],
        out_specs=(row(WIDTH), row(WIDTH)),
    )
    r3 = lambda a: a.reshape(n_db, 1, a.shape[-1])
    slopes = jnp.asarray(ALIBI_SLOPES, F32).reshape(N_HEADS, 1)
    return pl.pallas_call(
        functools.partial(_sattn_kernel, n_pages=n_pages),
        out_shape=(jax.ShapeDtypeStruct((n_db, 1, WIDTH), F32), jax.ShapeDtypeStruct((n_db, 1, WIDTH), F32)),
        grid_spec=grid_spec,
        compiler_params=_params("arbitrary"),
        name="sample_attention",
    )(page_table, r3(qa_s.astype(F32)), r3(qb_s.astype(F32)), *[r3(a) for a in new_rows],
      logf_new.reshape(n_db, N_HEADS, 1), r3(bias), slopes,
      *([flat(cache_k_a)] * n_pages), *([flat(cache_v_a)] * n_pages), *([flat(cache_k_b)] * n_pages),
      *([flat(cache_v_b)] * n_pages), *([logf_t] * n_pages))


def _fox_kernel(q_ref, qx_ref, k_ref, kx_ref, vt_ref, o_ref, qaug_scr, x_scr, m_scr, acc_scr, *, tq, ck):
    q0 = pl.program_id(1) * tq
    n_chunks = (q0 + tq + ck - 1) // ck
    lane = lax.broadcasted_iota(I32, (tq, LANES), 1)
    zeros = jnp.zeros((tq, LANES), BF16)
    qx = qx_ref[...]
    for h in range(N_HEADS):
        own = (lane >= FX * h) & (lane < FX * (h + 1))
        qaug_scr[h] = jnp.concatenate([jnp.where(_head_pick(tq, h), q_ref[:, _pair_lanes(h)], zeros),
                                       jnp.where(own, qx, zeros)], axis=1)
    _flash_init(m_scr, acc_scr)
    krow = lax.broadcasted_iota(I32, (ck, tq), 0)
    qcol = lax.broadcasted_iota(I32, (ck, tq), 1)

    def chunk(c, masked):
        c0 = pl.multiple_of(c * ck, ck)
        k_aug = lambda h: jnp.concatenate([k_ref[pl.ds(c0, ck), _pair_lanes(h)], kx_ref[pl.ds(c0, ck), :]], axis=1)
        visible = ((krow + c0) <= (qcol + q0)) if masked else None
        _flash_chunk(k_aug, qaug_scr, lambda h: vt_ref[c, HEAD_DIM * h:HEAD_DIM * (h + 1), :],
                     x_scr, m_scr, acc_scr, visible=visible)

    def full_chunk(c, carry):
        chunk(c, False)
        return carry

    lax.fori_loop(0, n_chunks - 1, full_chunk, 0)
    chunk(n_chunks - 1, True)
    _flash_output(o_ref, acc_scr)


def _fox_prompt(qb, qx, kb, kx, vb_t, tq, ck):
    n_b, seq, _ = qb.shape
    per_q = lambda width: pl.BlockSpec((None, tq, width), lambda b, i: (b, i, 0))
    whole = lambda width: pl.BlockSpec((None, seq, width), lambda b, i: (b, 0, 0))
    return pl.pallas_call(
        functools.partial(_fox_kernel, tq=tq, ck=ck),
        out_shape=jax.ShapeDtypeStruct((n_b, seq, WIDTH), BF16),
        grid=(n_b, seq // tq),
        in_specs=[per_q(WIDTH), per_q(LANES), whole(WIDTH), whole(LANES),
                  pl.BlockSpec((None, seq // ck, WIDTH, ck), lambda b, i: (b, 0, 0, 0))],
        out_specs=per_q(WIDTH),
        scratch_shapes=_flash_scratch(N_HEADS, tq, ck),
        compiler_params=_params("arbitrary", "arbitrary"),
        name="fox_prompt",
    )(qb, qx, kb, kx, vb_t)


def _post_kernel(x_ref, oa_ref, ob_ref, sga_ref, sgb_ref, gate1_ref, shift2_ref, scale2_ref, gate2_ref, g2_ref,
                 woa_ref, wob_ref, wout_ref, wup_ref, wdn_ref, y_ref, *, ff_chunk):
    merged = (sga_ref[...].astype(F32) * _dot(oa_ref[...], woa_ref[...])
              + sgb_ref[...].astype(F32) * _dot(ob_ref[...], wob_ref[...]))
    x1 = x_ref[...] + gate1_ref[...] * _dot(merged.astype(BF16), wout_ref[...])
    h2 = x1 * lax.rsqrt(jnp.mean(x1 * x1, axis=-1, keepdims=True) + EPS) * g2_ref[...]
    h2 = (h2 * (1.0 + scale2_ref[...]) + shift2_ref[...]).astype(BF16)
    mlp = jnp.zeros(x1.shape, F32)
    for c in range(0, D_FF, ff_chunk):
        u = jnp.maximum(_dot(h2, wup_ref[:, c:c + ff_chunk]), 0.0)
        mlp = mlp + _dot((u * u).astype(BF16), wdn_ref[c:c + ff_chunk, :])
    y_ref[...] = x1 + gate2_ref[...] * mlp


def _post(x, oa, ob, sga, sgb, mod, g2, woa, wob, wout, wup, wdn, tm):
    n_g, seq, _ = x.shape
    n_r = mod.shape[1]
    rb = 1 if n_r == 1 else tm
    row = lambda width: pl.BlockSpec((None, tm, width), lambda g, i: (g, i, 0))
    modspec = lambda j: pl.BlockSpec((None, rb, D_MODEL), lambda g, i: (g, i if n_r > 1 else 0, j))
    return pl.pallas_call(
        functools.partial(_post_kernel, ff_chunk=1024),
        out_shape=jax.ShapeDtypeStruct((n_g, seq, D_MODEL), F32),
        grid=(n_g, seq // tm),
        in_specs=[row(D_MODEL), row(WIDTH), row(WIDTH), row(D_MODEL), row(D_MODEL),
                  modspec(2), modspec(3), modspec(4), modspec(5), _const_spec((1, D_MODEL)),
                  _const_spec((WIDTH, D_MODEL)), _const_spec((WIDTH, D_MODEL)), _const_spec((D_MODEL, D_MODEL)),
                  _const_spec((D_MODEL, D_FF)), _const_spec((D_FF, D_MODEL))],
        out_specs=row(D_MODEL),
        compiler_params=_params("arbitrary", "arbitrary"),
        name="merge_out_mlp",
    )(x, oa, ob, sga, sgb, mod, mod, mod, mod, g2, woa, wob, wout, wup, wdn)


def _lower_tri(n):
    return jnp.asarray(np.tril(np.ones((n, n), np.float32)), BF16)


def _layer(l, x_p, x_s, caches, cache_k_idx, logf_t, page_table, mod_p, mod_s, lw, tiles):
    (g_norm1, w_r, gains, bsm, gmat, woa, wob, wout, g_norm2, wup, wdn) = lw
    tm, tq = tiles
    n_b, seq, _ = x_p.shape
    n_db = x_s.shape[0]

    (ka_p, va_p, kb_p, vb_p, small_p, qa_bf, ka_bf, vat_bf, qb_bf, kb_bf, vbt_bf, qi_bf, ki2_bf, smallt_p,
     sga_p, sgb_p, kx_b, qx_b) = _in_proj(x_p, mod_p, g_norm1, w_r, gains, bsm, gmat, tm)
    xs = x_s.reshape(1, n_db, D_MODEL)
    (ka_s, va_s, kb_s, vb_s, small_s, qa_s, _, _, qb_s, _, _, qi_s, _, _, sga_s, sgb_s, _, _) = _in_proj(
        xs, mod_s, g_norm1, w_r, gains, bsm, gmat, n_db)
    small_s2 = small_s[0]
    ki_s, wi_s = small_s2[:, :SM_WI], small_s2[:, SM_WI:SM_LOGF]
    logf_s = small_s2[:, SM_LOGF:SM_LOGF + N_HEADS]

    n_keys = page_table.shape[1] * cache_k_idx.shape[2] + 1
    scores = _sample_index_scores(l, page_table, cache_k_idx, qi_s[0], wi_s, ki_s)
    bias_t = _sample_selection(scores.T, _lower_tri(LANES), n_keys)

    oa_s, ob_s = _sample_attention(l, page_table, caches, logf_t, qa_s[0], qb_s[0],
                                   (ka_s[0], va_s[0], kb_s[0], vb_s[0]), logf_s, bias_t.T)
    ob_p = _fox_prompt(qb_bf, qx_b, kb_bf, kx_b, vbt_bf, tq, tm)
    oa_p = _dsa_prompt(qi_bf, ki2_bf, smallt_p, qa_bf, ka_bf, vat_bf, _lower_tri(tm), tq, tm)

    y_p = _post(x_p, oa_p, ob_p, sga_p, sgb_p, mod_p, g_norm2, woa, wob, wout, wup, wdn, tm)
    y_s = _post(xs, oa_s.reshape(1, n_db, WIDTH).astype(BF16), ob_s.reshape(1, n_db, WIDTH).astype(BF16),
                sga_s, sgb_s, mod_s, g_norm2, woa, wob, wout, wup, wdn, n_db)

    heads = lambda a, lead: a.reshape(*lead, N_HEADS, HEAD_DIM)
    lp, ls = (n_b, seq), (n_db, 1)
    from_t = lambda lo, hi: jnp.swapaxes(smallt_p[:, lo:hi, :], 1, 2)
    state_p = (heads(ka_p, lp), heads(va_p, lp), from_t(0, SM_WI), heads(kb_p, lp), heads(vb_p, lp),
               from_t(SM_LOGF, SM_LOGF + N_HEADS))
    state_s = (heads(ka_s[0], ls), heads(va_s[0], ls), ki_s.reshape(n_db, 1, HEAD_DIM), heads(kb_s[0], ls),
               heads(vb_s[0], ls), logf_s.reshape(n_db, 1, N_HEADS))
    return y_p, y_s.reshape(n_db, 1, D_MODEL), state_p, state_s


def kernel(x_prompt, x_sample, cache_k_a, cache_v_a, cache_k_idx, cache_k_b, cache_v_b, cache_logf_b, page_table,
           c_prompt, c_sample, w_ada, b_ada, g_norm1, w_in, b_forget, g_qa, g_ka, g_qb, g_kb, w_o_a, w_o_b, w_out,
           g_norm2, w_up, w_down):
    depth = w_in.shape[0]
    n_b, seq, _ = x_prompt.shape
    n_db, dec_seq, _ = x_sample.shape
    assert dec_seq == 1 and x_prompt.shape[-1] == D_MODEL
    assert cache_k_a.shape[3:] == (N_HEADS, HEAD_DIM) and cache_k_idx.shape[3] == HEAD_DIM
    tiles = (min(512, seq), min(512, seq))

    c_all = jnp.concatenate([c_prompt, c_sample], axis=0)
    pad = (-c_all.shape[0]) % 8
    c_all = jnp.pad(c_all, ((0, pad), (0, 0)))
    gmat = jnp.asarray(np.kron(np.eye(N_HEADS, dtype=np.float32),
                               np.full((HEAD_DIM, HEAD_DIM), 1.0 / HEAD_DIM, np.float32)), BF16)
    split = 7 * WIDTH
    y_p, y_s = x_prompt, x_sample.reshape(n_db, D_MODEL)
    states_p, states_s = [], []
    for l in range(depth):
        mod = _ada_mod(c_all, w_ada[l], b_ada[l])
        mod_p = mod[:n_b].reshape(n_b, 1, 6 * D_MODEL)
        mod_s = mod[n_b:n_b + n_db].reshape(1, n_db, 6 * D_MODEL)
        w = w_in[l]
        k_idx_cols = w[:, split:split + HEAD_DIM]
        w_r = jnp.concatenate(
            [w[:, :split], w[:, split:split + HEAD_DIM + 2 * N_HEADS],
             jnp.zeros((D_MODEL, LANES - HEAD_DIM - 2 * N_HEADS), w.dtype),
             k_idx_cols, k_idx_cols, w[:, split + HEAD_DIM + 2 * N_HEADS:]], axis=1).astype(BF16)
        gains = jnp.stack([jnp.tile(g[l], N_HEADS) for g in (g_qa, g_ka, g_qb, g_kb)])
        bsm = jnp.zeros((1, LANES), F32).at[0, SM_LOGF:SM_LOGF + N_HEADS].set(b_forget[l])
        lw = (g_norm1[l].reshape(1, D_MODEL), w_r, gains, bsm, gmat, w_o_a[l].astype(BF16), w_o_b[l].astype(BF16),
              w_out[l].astype(BF16), g_norm2[l].reshape(1, D_MODEL), w_up[l].astype(BF16), w_down[l].astype(BF16))
        logf_t = jnp.swapaxes(cache_logf_b[l], 1, 2)
        y_p, y_s, st_p, st_s = _layer(l, y_p, y_s, (cache_k_a, cache_v_a, cache_k_b, cache_v_b), cache_k_idx,
                                      logf_t, page_table, mod_p, mod_s, lw, tiles)
        y_s = y_s.reshape(n_db, D_MODEL)
        states_p.append(st_p)
        states_s.append(st_s)
    k_a_p, v_a_p, k_idx_p, k_b_p, v_b_p, logf_p = (jnp.stack(s) for s in zip(*states_p))
    k_a_s, v_a_s, k_idx_s, k_b_s, v_b_s, logf_s = (jnp.stack(s) for s in zip(*states_s))
    return (y_p, y_s.reshape(n_db, 1, D_MODEL), k_a_p, v_a_p, k_idx_p, k_b_p, v_b_p, logf_p,
            k_a_s, v_a_s, k_idx_s, k_b_s, v_b_s, logf_s)
```
